```python
import jax, jax.numpy as jnp
from jax import lax
import numpy as np

D_MODEL = 2048
BATCH = 4
SEQ = 8192
DEPTH = 1

PLE_DIM = 256
RMS_EPS = 1e-6

MLA_HEADS = 8
MLA_NOPE = 128
MLA_ROPE = 64
MLA_QK = MLA_NOPE + MLA_ROPE
MLA_V = 128
MLA_Q_LORA = 512
MLA_KV_LORA = 256
ROPE_THETA = 10000.0
Q_BLOCK = 128

RW_HEADS = 16
RW_HEAD = 64
RW_C = RW_HEADS * RW_HEAD
RW_DECAY_LORA = 64
RW_A_LORA = 64
RW_GATE_LORA = 160
RW_GN_EPS = 64e-5

MIX_WIDTH = MLA_HEADS * MLA_V + RW_C

MLA_COLS = MLA_Q_LORA + MLA_KV_LORA + MLA_ROPE
RW_COLS = 3 * RW_C + RW_DECAY_LORA + RW_A_LORA + RW_GATE_LORA
IN_COLS = MLA_COLS + RW_COLS

N_EXPERTS = 64
TOP_K = 8
N_GROUPS = 8
TOPK_GROUPS = 4
EXPERT_DIM = 512
SHARED_DIM = 512
ROUTED_SCALE = 2.5

kernel_name = "hymba_mla_rwkv7_moe_ple_layer"


def rms_norm(x, g, eps=RMS_EPS):
    x32 = x.astype(jnp.float32)
    y = x32 * lax.rsqrt(jnp.mean(x32 * x32, axis=-1, keepdims=True) + eps)
    return (y * g.astype(jnp.float32)).astype(x.dtype)


def rope(x, positions):
    half = x.shape[-1] // 2
    inv = ROPE_THETA ** (-jnp.arange(half, dtype=jnp.float32) / half)
    ang = positions.astype(jnp.float32)[..., None] * inv
    cos = jnp.cos(ang)[:, :, None, :]
    sin = jnp.sin(ang)[:, :, None, :]
    x1 = x[..., :half].astype(jnp.float32)
    x2 = x[..., half:].astype(jnp.float32)
    out = jnp.concatenate([x1 * cos - x2 * sin, x2 * cos + x1 * sin], axis=-1)
    return out.astype(x.dtype)


def causal_block_attention(q, k, v):
    B, T, H, _ = q.shape
    dv = v.shape[-1]
    scale = MLA_QK ** -0.5
    kidx = jnp.arange(T)

    def one_block(start):
        qb = lax.dynamic_slice_in_dim(q, start, Q_BLOCK, axis=1)
        s = jnp.einsum('bqhd,bkhd->bhqk', qb, k, preferred_element_type=jnp.float32) * scale
        qidx = start + jnp.arange(Q_BLOCK)
        s = jnp.where(kidx[None, :] <= qidx[:, None], s, -jnp.inf)
        pr = jax.nn.softmax(s, axis=-1)
        return jnp.einsum('bhqk,bkhd->bqhd', pr.astype(v.dtype), v)

    out = lax.map(one_block, jnp.arange(T // Q_BLOCK) * Q_BLOCK)
    return out.transpose(1, 0, 2, 3, 4).reshape(B, T, H * dv)


def mla_mixer(c_q, c_kv, k_pe, positions, g_qa, w_uq, g_kva, w_ukv, g_qn, g_kn):
    B, T, _ = c_q.shape
    q = (rms_norm(c_q, g_qa) @ w_uq).reshape(B, T, MLA_HEADS, MLA_QK)
    kv = (rms_norm(c_kv, g_kva) @ w_ukv).reshape(B, T, MLA_HEADS, MLA_NOPE + MLA_V)
    k_nope, v = kv[..., :MLA_NOPE], kv[..., MLA_NOPE:]
    k_pe_h = jnp.broadcast_to(k_pe[:, :, None, :], (B, T, MLA_HEADS, MLA_ROPE))
    k = jnp.concatenate([k_nope, k_pe_h], axis=-1)
    q = rms_norm(q, g_qn)
    k = rms_norm(k, g_kn)
    q = jnp.concatenate([q[..., :MLA_NOPE], rope(q[..., MLA_NOPE:], positions)], axis=-1)
    k = jnp.concatenate([k[..., :MLA_NOPE], rope(k[..., MLA_NOPE:], positions)], axis=-1)
    return causal_block_attention(q, k, v)


def token_shift(z):
    return jnp.pad(z, ((0, 0), (1, 0), (0, 0)))[:, :-1]


def rwkv7_mixer(z, mu, w0, w_w2, a0, a_w2, g_w2, k_k, k_a, r_k, ln_w, ln_b):
    B, T, _ = z.shape
    f32 = jnp.float32
    z = z + (token_shift(z) - z) * mu
    cuts = [RW_C, 2 * RW_C, 3 * RW_C, 3 * RW_C + RW_DECAY_LORA, 3 * RW_C + RW_DECAY_LORA + RW_A_LORA]
    r, k, v, wl, al, gl = jnp.split(z, cuts, axis=-1)
    w = -jax.nn.softplus(-(w0 + jnp.tanh(wl) @ w_w2)) - 0.5
    decay = jnp.exp(-jnp.exp(w.astype(f32)))
    a = jax.nn.sigmoid(a0 + al @ a_w2)
    g = jax.nn.sigmoid(gl) @ g_w2
    heads = lambda t: t.reshape(B, T, RW_HEADS, RW_HEAD).astype(f32)
    kk = heads(k * k_k)
    kk = kk / jnp.maximum(jnp.sqrt(jnp.sum(kk * kk, axis=-1, keepdims=True)), 1e-12)
    k = k * (1 + (a - 1) * k_a)
    r_h, k_h, v_h, a_h, w_h = heads(r), heads(k), heads(v), heads(a), heads(decay)

    def step(S, inp):
        r_t, w_t, k_t, v_t, kk_t, a_t = inp
        sa = jnp.einsum('bhij,bhj->bhi', S, -kk_t)
        S = S * w_t[:, :, None, :] + sa[..., None] * (kk_t * a_t)[:, :, None, :] + v_t[..., None] * k_t[:, :, None, :]
        y = jnp.einsum('bhij,bhj->bhi', S, r_t)
        return S, y

    xs = tuple(t.transpose(1, 0, 2, 3) for t in (r_h, w_h, k_h, v_h, kk, a_h))
    S0 = jnp.zeros((B, RW_HEADS, RW_HEAD, RW_HEAD), f32)
    _, y = lax.scan(step, S0, xs)
    y = y.transpose(1, 0, 2, 3)
    mean = jnp.mean(y, axis=-1, keepdims=True)
    var = jnp.mean(jnp.square(y - mean), axis=-1, keepdims=True)
    y = ((y - mean) * lax.rsqrt(var + RW_GN_EPS)).reshape(B, T, RW_C)
    y = y * ln_w.astype(f32) + ln_b.astype(f32)
    bonus = jnp.sum(r_h * k_h * r_k.astype(f32), axis=-1, keepdims=True) * v_h
    out = (y + bonus.reshape(B, T, RW_C)) * g.astype(f32)
    return out.astype(z.dtype)


def moe(hn, w_router, router_bias, w_gate, w_up, w_down, ws_gate, ws_up, ws_down):
    f32 = jnp.float32
    shp = hn.shape
    x = hn.reshape(-1, shp[-1])
    scores = jax.nn.sigmoid((x @ w_router).astype(f32))
    biased = scores + router_bias.astype(f32)
    grp = biased.reshape(-1, N_GROUPS, N_EXPERTS // N_GROUPS)
    group_score = jnp.sum(lax.top_k(grp, 2)[0], axis=-1)
    _, gidx = lax.top_k(group_score, TOPK_GROUPS)
    gmask = jnp.sum(jax.nn.one_hot(gidx, N_GROUPS, dtype=f32), axis=1) > 0
    emask = jnp.repeat(gmask, N_EXPERTS // N_GROUPS, axis=-1)
    _, eidx = lax.top_k(jnp.where(emask, biased, -jnp.inf), TOP_K)
    wts = jnp.take_along_axis(scores, eidx, axis=-1)
    wts = wts / jnp.sum(wts, axis=-1, keepdims=True) * ROUTED_SCALE
    combine = jnp.einsum('nk,nke->ne', wts, jax.nn.one_hot(eidx, N_EXPERTS, dtype=f32))

    def expert(acc, ew):
        wg, wu, wd, c = ew
        hmid = jax.nn.silu(x @ wg) * (x @ wu)
        return acc + c[:, None] * (hmid @ wd).astype(f32), None

    routed, _ = lax.scan(expert, jnp.zeros(x.shape, f32), (w_gate, w_up, w_down, combine.T))
    shared = (jax.nn.silu(x @ ws_gate) * (x @ ws_up)) @ ws_down
    return (routed.astype(x.dtype) + shared).reshape(shp)


def setup_inputs(seed: int = 0) -> dict:
    key = jax.random.key(seed)
    ks = iter(jax.random.split(key, 48))
    f32 = jnp.float32
    L, D = DEPTH, D_MODEL

    def nrm(shape, scale):
        return jax.random.normal(next(ks), shape, f32) * scale

    x = nrm((BATCH, SEQ, D), 1.0)
    p = nrm((DEPTH, BATCH, SEQ, PLE_DIM), 1.0)
    offs = jax.random.randint(next(ks), (BATCH, 1), 0, 1024, dtype=jnp.int32)
    positions = offs + jnp.arange(SEQ, dtype=jnp.int32)[None, :]
    return {
        "x": x,
        "p": p,
        "positions": positions,
        "g_mix": 1.0 + nrm((L, D), 0.02),
        "w_in": nrm((L, D, IN_COLS), D ** -0.5),
        "mla_g_qa": 1.0 + nrm((L, MLA_Q_LORA), 0.02),
        "mla_w_uq": nrm((L, MLA_Q_LORA, MLA_HEADS * MLA_QK), MLA_Q_LORA ** -0.5),
        "mla_g_kva": 1.0 + nrm((L, MLA_KV_LORA), 0.02),
        "mla_w_ukv": nrm((L, MLA_KV_LORA, MLA_HEADS * (MLA_NOPE + MLA_V)), MLA_KV_LORA ** -0.5),
        "mla_g_qn": 1.0 + nrm((L, MLA_QK), 0.02),
        "mla_g_kn": 1.0 + nrm((L, MLA_QK), 0.02),
        "rw_mu": jax.random.uniform(next(ks), (L, RW_COLS), f32, 0.0, 1.0),
        "rw_w0": jax.random.uniform(next(ks), (L, RW_C), f32, -6.0, -1.0),
        "rw_w_w2": nrm((L, RW_DECAY_LORA, RW_C), 0.1 * RW_DECAY_LORA ** -0.5),
        "rw_a0": nrm((L, RW_C), 0.1),
        "rw_a_w2": nrm((L, RW_A_LORA, RW_C), 0.1 * RW_A_LORA ** -0.5),
        "rw_g_w2": nrm((L, RW_GATE_LORA, RW_C), RW_GATE_LORA ** -0.5),
        "rw_k_k": 0.85 + nrm((L, RW_C), 0.02),
        "rw_k_a": 1.0 + nrm((L, RW_C), 0.02),
        "rw_r_k": nrm((L, RW_HEADS, RW_HEAD), 0.1),
        "rw_ln_w": 1.0 + nrm((L, RW_C), 0.02),
        "rw_ln_b": nrm((L, RW_C), 0.02),
        "w_out": nrm((L, MIX_WIDTH, D), MIX_WIDTH ** -0.5),
        "g_ffn": 1.0 + nrm((L, D), 0.02),
        "w_router": nrm((L, D, N_EXPERTS), D ** -0.5),
        "router_bias": nrm((L, N_EXPERTS), 0.01),
        "w_gate": nrm((L, N_EXPERTS, D, EXPERT_DIM), D ** -0.5),
        "w_up": nrm((L, N_EXPERTS, D, EXPERT_DIM), D ** -0.5),
        "w_down": nrm((L, N_EXPERTS, EXPERT_DIM, D), EXPERT_DIM ** -0.5),
        "ws_gate": nrm((L, D, SHARED_DIM), D ** -0.5),
        "ws_up": nrm((L, D, SHARED_DIM), D ** -0.5),
        "ws_down": nrm((L, SHARED_DIM, D), SHARED_DIM ** -0.5),
        "g_ple_in": 1.0 + nrm((L, D), 0.02),
        "w_ple_gate": nrm((L, D, D), D ** -0.5),
        "b_ple_gate": nrm((L, D), 0.02),
        "w_ple_proj": nrm((L, PLE_DIM, D), PLE_DIM ** -0.5),
        "g_ple_out": 1.0 + nrm((L, D), 0.02),
    }


def reference(x, p, positions, g_mix, w_in, mla_g_qa, mla_w_uq, mla_g_kva, mla_w_ukv, mla_g_qn, mla_g_kn,
              rw_mu, rw_w0, rw_w_w2, rw_a0, rw_a_w2, rw_g_w2, rw_k_k, rw_k_a, rw_r_k, rw_ln_w, rw_ln_b,
              w_out, g_ffn, w_router, router_bias, w_gate, w_up, w_down, ws_gate, ws_up, ws_down,
              g_ple_in, w_ple_gate, b_ple_gate, w_ple_proj, g_ple_out):
    h = x
    for i in range(DEPTH):
        xn = rms_norm(h, g_mix[i])
        proj = xn @ w_in[i]
        c_q = proj[..., :MLA_Q_LORA]
        c_kv = proj[..., MLA_Q_LORA:MLA_Q_LORA + MLA_KV_LORA]
        k_pe = proj[..., MLA_Q_LORA + MLA_KV_LORA:MLA_COLS]
        z_rw = proj[..., MLA_COLS:]
        y_mla = mla_mixer(c_q, c_kv, k_pe, positions, mla_g_qa[i], mla_w_uq[i], mla_g_kva[i],
                          mla_w_ukv[i], mla_g_qn[i], mla_g_kn[i])
        y_rw = rwkv7_mixer(z_rw, rw_mu[i], rw_w0[i], rw_w_w2[i], rw_a0[i], rw_a_w2[i], rw_g_w2[i],
                           rw_k_k[i], rw_k_a[i], rw_r_k[i], rw_ln_w[i], rw_ln_b[i])
        h = h + jnp.concatenate([y_mla, y_rw], axis=-1) @ w_out[i]
        h = h + moe(rms_norm(h, g_ffn[i]), w_router[i], router_bias[i], w_gate[i], w_up[i], w_down[i],
                    ws_gate[i], ws_up[i], ws_down[i])
        gate = jax.nn.sigmoid(rms_norm(h, g_ple_in[i]) @ w_ple_gate[i] + b_ple_gate[i])
        h = h + rms_norm((p[i] @ w_ple_proj[i]) * gate, g_ple_out[i])
    return h
```

```python
import functools

import jax
import jax.numpy as jnp
from jax import lax
from jax.experimental import pallas as pl
from jax.experimental.pallas import tpu as pltpu

F32 = jnp.float32
BF16 = jnp.bfloat16

D_MODEL = 2048
PLE_DIM = 256
RMS_EPS = 1e-6

MLA_HEADS = 8
MLA_NOPE = 128
MLA_ROPE = 64
MLA_QK = MLA_NOPE + MLA_ROPE
MLA_V = 128
MLA_Q_LORA = 512
MLA_KV_LORA = 256
ROPE_THETA = 10000.0
MLA_COLS = MLA_Q_LORA + MLA_KV_LORA + MLA_ROPE
MLA_COLS_PAD = 896

RW_HEADS = 16
RW_HEAD = 64
RW_C = RW_HEADS * RW_HEAD
RW_DECAY_LORA = 64
RW_A_LORA = 64
RW_GATE_LORA = 160
RW_GN_EPS = 64e-5
RW_LORA_PAD = 512
RW_COLS_PAD = 3 * RW_C + RW_LORA_PAD
RW_CHUNK = 64

N_EXPERTS = 64
TOP_K = 8
N_GROUPS = 8
TOPK_GROUPS = 4
EXPERT_DIM = 512
ROUTED_SCALE = 2.5

VMEM_LIMIT = 56 * 1024 * 1024


def _cparams(n_axes):
    return pltpu.CompilerParams(dimension_semantics=("arbitrary",) * n_axes,
                                vmem_limit_bytes=VMEM_LIMIT)


def _rms(x, g):
    ms = jnp.mean(x * x, axis=-1, keepdims=True)
    return x * lax.rsqrt(ms + RMS_EPS) * g


def _dot(a, b):
    return jnp.dot(a, b, preferred_element_type=F32)


def _dot_nt(a, b):
    return lax.dot_general(a, b, (((1,), (1,)), ((), ())), preferred_element_type=F32)


def _dot_tn(a, b):
    return lax.dot_general(a, b, (((0,), (0,)), ((), ())), preferred_element_type=F32)


def _split2(x):
    hi = x.astype(BF16)
    lo = (x - hi.astype(F32)).astype(BF16)
    return hi, lo


def _dot3(a, b):
    ah, al = _split2(a)
    bh, bl = _split2(b)
    return _dot(ah, bh) + _dot(ah, bl) + _dot(al, bh)


def _norm_matmul_kernel(x_ref, g_ref, w_ref, o_ref, xn_ref):
    @pl.when(pl.program_id(1) == 0)
    def _():
        xn_ref[...] = _rms(x_ref[...], g_ref[...]).astype(BF16)

    o_ref[...] = _dot(xn_ref[...], w_ref[...]).astype(o_ref.dtype)


def _norm_matmul(x, g, w, tm, tn, out_dtype, name):
    m, k = x.shape
    n = w.shape[1]
    tm = min(tm, m)
    return pl.pallas_call(
        _norm_matmul_kernel,
        grid=(m // tm, n // tn),
        in_specs=[pl.BlockSpec((tm, k), lambda i, j: (i, 0)),
                  pl.BlockSpec((1, k), lambda i, j: (0, 0)),
                  pl.BlockSpec((k, tn), lambda i, j: (0, j))],
        out_specs=pl.BlockSpec((tm, tn), lambda i, j: (i, j)),
        out_shape=jax.ShapeDtypeStruct((m, n), out_dtype),
        scratch_shapes=[pltpu.VMEM((tm, k), BF16)],
        compiler_params=_cparams(2),
        name=name,
    )(x, g, w)


def _mla_prep_kernel(pm_ref, pos_ref, invf_ref, sgn_ref, gqa_ref, wuq_ref, gkva_ref, wukv_ref,
                     gqn_ref, gkn_ref, q_ref, k_ref, v_ref):
    pm = pm_ref[0]
    ang = pos_ref[0].astype(F32) * invf_ref[...]
    cos = jnp.cos(ang)
    sin = jnp.sin(ang) * sgn_ref[...]

    def rope(xr):
        half = MLA_ROPE // 2
        swapped = jnp.concatenate([xr[:, half:], xr[:, :half]], axis=1)
        return xr * cos + swapped * sin

    scale = MLA_QK ** -0.5
    gqn = gqn_ref[...]
    gkn = gkn_ref[...]

    cq = _rms(pm[:, :MLA_Q_LORA], gqa_ref[...]).astype(BF16)
    q = _dot(cq, wuq_ref[...])
    ckv = _rms(pm[:, MLA_Q_LORA:MLA_Q_LORA + MLA_KV_LORA], gkva_ref[...]).astype(BF16)
    kv = _dot(ckv, wukv_ref[...])
    kpe = pm[:, MLA_Q_LORA + MLA_KV_LORA:MLA_COLS]
    kpe_ssq = jnp.sum(kpe * kpe, axis=-1, keepdims=True)
    kpe_rot = rope(kpe * gkn[:, MLA_NOPE:])

    for h in range(MLA_HEADS):
        qn = q[:, h * MLA_NOPE:(h + 1) * MLA_NOPE]
        qr = q[:, MLA_HEADS * MLA_NOPE + h * MLA_ROPE:MLA_HEADS * MLA_NOPE + (h + 1) * MLA_ROPE]
        ssq = jnp.sum(qn * qn, axis=-1, keepdims=True) + jnp.sum(qr * qr, axis=-1, keepdims=True)
        inv = lax.rsqrt(ssq * (1.0 / MLA_QK) + RMS_EPS) * scale
        q_ref[0, h, :, 0:MLA_NOPE] = (qn * inv * gqn[:, :MLA_NOPE]).astype(BF16)
        q_ref[0, h, :, MLA_NOPE:MLA_QK] = (rope(qr * gqn[:, MLA_NOPE:]) * inv).astype(BF16)

        kn = kv[:, h * 256:h * 256 + MLA_NOPE]
        ssq = jnp.sum(kn * kn, axis=-1, keepdims=True) + kpe_ssq
        inv = lax.rsqrt(ssq * (1.0 / MLA_QK) + RMS_EPS)
        k_ref[0, h, :, 0:MLA_NOPE] = (kn * inv * gkn[:, :MLA_NOPE]).astype(BF16)
        k_ref[0, h, :, MLA_NOPE:MLA_QK] = (kpe_rot * inv).astype(BF16)
        v_ref[0, h] = kv[:, h * 256 + MLA_NOPE:(h + 1) * 256].astype(BF16)


def _mla_prep(pm, pos, invf, sgn, g_qa, w_uq, g_kva, w_ukv, g_qn, g_kn, tm):
    b, t, _ = pm.shape
    tm = min(tm, t)
    full = lambda a: pl.BlockSpec(a.shape, lambda i, j: (0,) * a.ndim)
    qk_shape = jax.ShapeDtypeStruct((b, MLA_HEADS, t, MLA_QK), BF16)
    return pl.pallas_call(
        _mla_prep_kernel,
        grid=(b, t // tm),
        in_specs=[pl.BlockSpec((1, tm, MLA_COLS_PAD), lambda i, j: (i, j, 0)),
                  pl.BlockSpec((1, tm, 1), lambda i, j: (i, j, 0)),
                  full(invf), full(sgn), full(g_qa), full(w_uq), full(g_kva), full(w_ukv),
                  full(g_qn), full(g_kn)],
        out_specs=[pl.BlockSpec((1, MLA_HEADS, tm, MLA_QK), lambda i, j: (i, 0, j, 0)),
                   pl.BlockSpec((1, MLA_HEADS, tm, MLA_QK), lambda i, j: (i, 0, j, 0)),
                   pl.BlockSpec((1, MLA_HEADS, tm, MLA_V), lambda i, j: (i, 0, j, 0))],
        out_shape=[qk_shape, qk_shape, jax.ShapeDtypeStruct((b, MLA_HEADS, t, MLA_V), BF16)],
        compiler_params=_cparams(2),
        name="mla_prep",
    )(pm, pos, invf, sgn, g_qa, w_uq, g_kva, w_ukv, g_qn, g_kn)


def _attn_kernel(q_ref, k_ref, v_ref, o_ref, m_ref, l_ref, acc_ref, *, tq, tk):
    qi = pl.program_id(2)
    q = q_ref[0, 0]
    m_ref[...] = jnp.full(m_ref.shape, -jnp.inf, F32)
    l_ref[...] = jnp.zeros(l_ref.shape, F32)
    acc_ref[...] = jnp.zeros(acc_ref.shape, F32)

    def step(j, masked):
        start = pl.multiple_of(j * tk, tk)
        s = _dot_nt(q, k_ref[0, 0, pl.ds(start, tk), :])
        if masked:
            row = qi * tq + lax.broadcasted_iota(jnp.int32, (tq, tk), 0)
            col = j * tk + lax.broadcasted_iota(jnp.int32, (tq, tk), 1)
            s = jnp.where(col <= row, s, -jnp.inf)
        m_old = m_ref[...]
        m_new = jnp.maximum(m_old, jnp.max(s, axis=-1, keepdims=True))
        alpha = jnp.exp(m_old - m_new)
        p = jnp.exp(s - m_new)
        l_ref[...] = alpha * l_ref[...] + jnp.sum(p, axis=-1, keepdims=True)
        acc_ref[...] = alpha * acc_ref[...] + _dot(p.astype(BF16), v_ref[0, 0, pl.ds(start, tk), :])
        m_ref[...] = m_new

    n_full = (qi * tq) // tk

    def body(j, c):
        step(j, False)
        return c

    lax.fori_loop(0, n_full, body, 0)
    for d in range(tq // tk):
        step(n_full + d, True)
    o_ref[0] = (acc_ref[...] / l_ref[...]).astype(o_ref.dtype)


def _attention(q, k, v, tq, tk):
    b, h, t, _ = q.shape
    tq = min(tq, t)
    tk = min(tk, tq)
    return pl.pallas_call(
        functools.partial(_attn_kernel, tq=tq, tk=tk),
        grid=(b, h, t // tq),
        in_specs=[pl.BlockSpec((1, 1, tq, MLA_QK), lambda i, j, n: (i, j, n, 0)),
                  pl.BlockSpec((1, 1, t, MLA_QK), lambda i, j, n: (i, j, 0, 0)),
                  pl.BlockSpec((1, 1, t, MLA_V), lambda i, j, n: (i, j, 0, 0))],
        out_specs=pl.BlockSpec((1, tq, MLA_V), lambda i, j, n: (i, n, j)),
        out_shape=jax.ShapeDtypeStruct((b, t, h * MLA_V), BF16),
        scratch_shapes=[pltpu.VMEM((tq, 1), F32), pltpu.VMEM((tq, 1), F32),
                        pltpu.VMEM((tq, MLA_V), F32)],
        compiler_params=_cparams(3),
        name="mla_attention",
    )(q, k, v)


def _rw_prep_kernel(z_ref, zp_ref, mu_ref, w0_ref, ww2_ref, a0_ref, aw2_ref, gw2_ref, kk_ref,
                    ka_ref, bd_ref, r_o, w_o, k_o, v_o, kk_o, b_o, g_o):
    z = z_ref[0]
    prev = zp_ref[0][7:8, :]
    prev = jnp.where(pl.program_id(1) == 0, jnp.zeros_like(prev), prev)
    row = lax.broadcasted_iota(jnp.int32, z.shape, 0)
    zs = jnp.where(row == 0, prev, pltpu.roll(z, 1, axis=0))
    z = z + (zs - z) * mu_ref[...]

    c = RW_C
    r = z[:, 0:c]
    k = z[:, c:2 * c]
    v = z[:, 2 * c:3 * c]
    wl = z[:, 3 * c:3 * c + 128]
    al = z[:, 3 * c + 128:3 * c + 256]
    gl = z[:, 3 * c + 256:3 * c + 512]

    f = w0_ref[...] + _dot(jnp.tanh(wl).astype(BF16), ww2_ref[...])
    softplus_neg_f = jnp.maximum(-f, 0.0) + jnp.log(1.0 + jnp.exp(-jnp.abs(f)))
    w = -softplus_neg_f - 0.5
    w_o[0] = -jnp.exp(w)
    a = jax.nn.sigmoid(a0_ref[...] + _dot(al.astype(BF16), aw2_ref[...]))
    g_o[0] = _dot(jax.nn.sigmoid(gl).astype(BF16), gw2_ref[...])

    kk = k * kk_ref[...]
    hi, lo = _split2(kk * kk)
    ssq = _dot(hi, bd_ref[...]) + _dot(lo, bd_ref[...])
    kk = kk / jnp.maximum(jnp.sqrt(ssq), 1e-12)
    r_o[0] = r
    k_o[0] = k * (1.0 + (a - 1.0) * ka_ref[...])
    v_o[0] = v
    kk_o[0] = kk
    b_o[0] = kk * a


def _rw_prep(z, mu, w0, ww2, a0, aw2, gw2, k_k, k_a, bd, tm):
    b, t, cols = z.shape
    tm = min(tm, t)
    full = lambda a: pl.BlockSpec(a.shape, lambda i, j: (0,) * a.ndim)
    out = jax.ShapeDtypeStruct((b, t, RW_C), F32)
    ospec = pl.BlockSpec((1, tm, RW_C), lambda i, j: (i, j, 0))
    return pl.pallas_call(
        _rw_prep_kernel,
        grid=(b, t // tm),
        in_specs=[pl.BlockSpec((1, tm, cols), lambda i, j: (i, j, 0)),
                  pl.BlockSpec((1, 8, cols), lambda i, j: (i, jnp.maximum(j * (tm // 8) - 1, 0), 0)),
                  full(mu), full(w0), full(ww2), full(a0), full(aw2), full(gw2), full(k_k),
                  full(k_a), full(bd)],
        out_specs=[ospec] * 7,
        out_shape=[out] * 7,
        compiler_params=_cparams(2),
        name="rwkv_prep",
    )(z, z, mu, w0, ww2, a0, aw2, gw2, k_k, k_a, bd)


def _rw_chunk_kernel(r_ref, w_ref, k_ref, v_ref, kk_ref, b_ref, g_ref, rk_ref, lnw_ref, lnb_ref,
                     o_ref, s_ref):
    cs = RW_CHUNK
    hd = RW_HEAD

    @pl.when(pl.program_id(1) == 0)
    def _():
        s_ref[...] = jnp.zeros(s_ref.shape, F32)

    r = r_ref[0]
    w = w_ref[0]
    k = k_ref[0]
    v = v_ref[0]
    kk = kk_ref[0]
    b = b_ref[0]
    g = g_ref[0]

    row = lax.broadcasted_iota(jnp.int32, (cs, cs), 0)
    col = lax.broadcasted_iota(jnp.int32, (cs, cs), 1)
    incl = row >= col
    strict = row > col
    tri = jnp.where(incl, 1.0, 0.0).astype(BF16)
    eye = jnp.where(row == col, 1.0, 0.0).astype(F32)

    w1 = w.astype(BF16)
    w2 = (w - w1.astype(F32)).astype(BF16)
    w3 = (w - w1.astype(F32) - w2.astype(F32)).astype(BF16)
    logp = _dot(tri, w1) + _dot(tri, w2) + _dot(tri, w3)
    logp_end = logp[cs - 1:cs, :]
    p_incl = jnp.exp(logp)
    p_inv = jnp.exp(-logp)
    p_prev = jnp.exp(logp - w)
    p_tail = jnp.exp(logp_end - logp)
    p_end = jnp.exp(logp_end)

    a_t = -kk * p_prev
    r_t = r * p_incl
    b_t = b * p_inv
    k_t = k * p_inv
    b_h = b * p_tail
    k_h = k * p_tail
    rk_all = r * k * rk_ref[...]

    row2 = lax.broadcasted_iota(jnp.int32, (cs, 2 * cs), 0)
    col2 = lax.broadcasted_iota(jnp.int32, (cs, 2 * cs), 1)
    mask2 = row2 >= jnp.bitwise_and(col2, cs - 1)

    for h in range(RW_HEADS):
        sl = slice(h * hd, (h + 1) * hd)
        vv = v[:, sl]
        lhs = jnp.concatenate([a_t[:, sl], r_t[:, sl]], axis=0).astype(BF16)
        rhs = jnp.concatenate([b_t[:, sl], k_t[:, sl]], axis=0).astype(BF16)
        gm = _dot_nt(lhs, rhs)
        a_ab = jnp.where(strict, gm[:cs, :cs], 0.0)
        a_ak = jnp.where(strict, gm[:cs, cs:], 0.0)
        a_r = jnp.where(mask2, gm[cs:, :], 0.0)

        x = a_ab
        tm = eye + a_ab
        for _ in range(5):
            x = _dot3(x, x)
            tm = tm + _dot3(tm, x)

        s0 = s_ref[h]
        ls = _dot_nt(lhs, s0.astype(BF16))
        u = _dot3(tm, ls[:cs] + _dot(a_ak.astype(BF16), vv.astype(BF16)))
        uv = jnp.concatenate([u, vv], axis=0).astype(BF16)
        y = ls[cs:] + _dot(a_r.astype(BF16), uv)
        bk = jnp.concatenate([b_h[:, sl], k_h[:, sl]], axis=0).astype(BF16)
        s_ref[h] = s0 * p_end[:, sl] + _dot_tn(uv, bk)

        mean = jnp.mean(y, axis=-1, keepdims=True)
        yc = y - mean
        var = jnp.mean(yc * yc, axis=-1, keepdims=True)
        yn = yc * lax.rsqrt(var + RW_GN_EPS) * lnw_ref[:, sl] + lnb_ref[:, sl]
        bonus = jnp.sum(rk_all[:, sl], axis=-1, keepdims=True) * vv
        o_ref[0, :, sl] = ((yn + bonus) * g[:, sl]).astype(o_ref.dtype)


def _rw_chunk(r, w, k, v, kk, bb, g, r_k, ln_w, ln_b):
    b, t, c = r.shape
    cs = RW_CHUNK
    full = lambda a: pl.BlockSpec(a.shape, lambda i, j: (0,) * a.ndim)
    spec = pl.BlockSpec((1, cs, c), lambda i, j: (i, j, 0))
    return pl.pallas_call(
        _rw_chunk_kernel,
        grid=(b, t // cs),
        in_specs=[spec] * 7 + [full(r_k), full(ln_w), full(ln_b)],
        out_specs=spec,
        out_shape=jax.ShapeDtypeStruct((b, t, c), BF16),
        scratch_shapes=[pltpu.VMEM((RW_HEADS, RW_HEAD, RW_HEAD), F32)],
        compiler_params=_cparams(2),
        name="rwkv_chunk",
    )(r, w, k, v, kk, bb, g, r_k, ln_w, ln_b)


def _out_proj_kernel(ya_ref, yr_ref, x_ref, w_ref, g_ref, wr_ref, h_ref, hn_ref, lg_ref):
    half = ya_ref.shape[1]
    acc = _dot(ya_ref[...], w_ref[0:half, :]) + _dot(yr_ref[...], w_ref[half:, :])
    h = x_ref[...] + acc
    h_ref[...] = h
    hn = _rms(h, g_ref[...])
    hn_ref[...] = hn.astype(BF16)
    lg_ref[...] = lax.dot_general(wr_ref[...], hn, (((1,), (1,)), ((), ())),
                                  precision=lax.Precision.HIGHEST, preferred_element_type=F32)


def _out_proj(ya, yr, x, w, g, wr_t, tm):
    m, d = x.shape
    tm = min(tm, m)
    full = lambda a: pl.BlockSpec(a.shape, lambda i: (0,) * a.ndim)
    return pl.pallas_call(
        _out_proj_kernel,
        grid=(m // tm,),
        in_specs=[pl.BlockSpec((tm, ya.shape[1]), lambda i: (i, 0)),
                  pl.BlockSpec((tm, yr.shape[1]), lambda i: (i, 0)),
                  pl.BlockSpec((tm, d), lambda i: (i, 0)),
                  full(w), full(g), full(wr_t)],
        out_specs=[pl.BlockSpec((tm, d), lambda i: (i, 0)),
                   pl.BlockSpec((tm, d), lambda i: (i, 0)),
                   pl.BlockSpec((N_EXPERTS, tm), lambda i: (0, i))],
        out_shape=[jax.ShapeDtypeStruct((m, d), F32), jax.ShapeDtypeStruct((m, d), BF16),
                   jax.ShapeDtypeStruct((N_EXPERTS, m), F32)],
        compiler_params=_cparams(1),
        name="out_proj_router_logits",
    )(ya, yr, x, w, g, wr_t)


def _router_kernel(lg_ref, bias_ref, c_ref):
    tm = lg_ref.shape[1]
    gsz = N_EXPERTS // N_GROUPS
    scores = jax.nn.sigmoid(lg_ref[...]).reshape(N_GROUPS, gsz, tm)
    biased = scores + bias_ref[...].reshape(N_GROUPS, gsz, 1)
    neg = -jnp.inf

    eidx = lax.broadcasted_iota(jnp.int32, biased.shape, 1)
    m1 = jnp.max(biased, axis=1, keepdims=True)
    first = jnp.min(jnp.where(biased == m1, eidx, gsz), axis=1, keepdims=True)
    m2 = jnp.max(jnp.where(eidx == first, neg, biased), axis=1, keepdims=True)
    gscore = m1 + m2

    gidx = lax.broadcasted_iota(jnp.int32, gscore.shape, 0)
    gsel = jnp.zeros(gscore.shape, jnp.bool_)
    work = gscore
    for _ in range(TOPK_GROUPS):
        best = jnp.max(work, axis=0, keepdims=True)
        pick = jnp.min(jnp.where(work == best, gidx, N_GROUPS), axis=0, keepdims=True)
        hit = gidx == pick
        gsel = jnp.logical_or(gsel, hit)
        work = jnp.where(hit, neg, work)

    flat = lax.broadcasted_iota(jnp.int32, biased.shape, 0) * gsz + eidx
    work = jnp.where(gsel, biased, neg)
    sel = jnp.zeros(biased.shape, jnp.bool_)
    for _ in range(TOP_K):
        best = jnp.max(jnp.max(work, axis=1, keepdims=True), axis=0, keepdims=True)
        cand = jnp.where(work == best, flat, N_EXPERTS)
        pick = jnp.min(jnp.min(cand, axis=1, keepdims=True), axis=0, keepdims=True)
        hit = flat == pick
        sel = jnp.logical_or(sel, hit)
        work = jnp.where(hit, neg, work)

    wts = jnp.where(sel, scores, 0.0)
    tot = jnp.sum(jnp.sum(wts, axis=1, keepdims=True), axis=0, keepdims=True)
    comb = (wts / tot * ROUTED_SCALE).reshape(N_EXPERTS, tm)
    c_ref[...] = comb.T


def _router(lg_t, bias, tm):
    e, m = lg_t.shape
    tm = min(tm, m)
    return pl.pallas_call(
        _router_kernel,
        grid=(m // tm,),
        in_specs=[pl.BlockSpec((e, tm), lambda i: (0, i)),
                  pl.BlockSpec((e, 1), lambda i: (0, 0))],
        out_specs=pl.BlockSpec((tm, e), lambda i: (i, 0)),
        out_shape=jax.ShapeDtypeStruct((m, e), F32),
        compiler_params=_cparams(1),
        name="router_topk",
    )(lg_t, bias)


def _moe_dense_kernel(hn_ref, h_ref, c_ref, wg_ref, wu_ref, wd_ref, o_ref):
    e = pl.program_id(1)

    @pl.when(e == 0)
    def _():
        o_ref[...] = h_ref[...]

    x = hn_ref[...]
    gate = _dot(x, wg_ref[0])
    up = _dot(x, wu_ref[0])
    mid = (gate * jax.nn.sigmoid(gate) * up).astype(BF16)
    y = _dot(mid, wd_ref[0])
    lane = lax.broadcasted_iota(jnp.int32, c_ref.shape, 1)
    cw = jnp.sum(jnp.where(lane == e, c_ref[...], 0.0), axis=-1, keepdims=True)
    cw = jnp.where(e == N_EXPERTS, 1.0, cw)
    o_ref[...] += cw * y


def _moe_dense(hn, h, comb, wg, wu, wd, tm):
    m, d = h.shape
    tm = min(tm, m)
    ne = wg.shape[0]
    return pl.pallas_call(
        _moe_dense_kernel,
        grid=(m // tm, ne),
        in_specs=[pl.BlockSpec((tm, d), lambda i, e: (i, 0)),
                  pl.BlockSpec((tm, d), lambda i, e: (i, 0)),
                  pl.BlockSpec((tm, N_EXPERTS), lambda i, e: (i, 0)),
                  pl.BlockSpec((1, d, EXPERT_DIM), lambda i, e: (e, 0, 0)),
                  pl.BlockSpec((1, d, EXPERT_DIM), lambda i, e: (e, 0, 0)),
                  pl.BlockSpec((1, EXPERT_DIM, d), lambda i, e: (e, 0, 0))],
        out_specs=pl.BlockSpec((tm, d), lambda i, e: (i, 0)),
        out_shape=jax.ShapeDtypeStruct((m, d), F32),
        compiler_params=_cparams(2),
        name="moe_dense",
    )(hn, h, comb, wg, wu, wd)


def _ple_kernel(h_ref, p_ref, gin_ref, wg_ref, bg_ref, wp_ref, gout_ref, o_ref):
    h = h_ref[...]
    gate = jax.nn.sigmoid(_dot(_rms(h, gin_ref[...]).astype(BF16), wg_ref[...]) + bg_ref[...])
    pp = _dot(p_ref[...].astype(BF16), wp_ref[...])
    o_ref[...] = h + _rms(pp * gate, gout_ref[...])


def _ple(h, p, g_in, w_g, b_g, w_p, g_out, tm):
    m, d = h.shape
    tm = min(tm, m)
    full = lambda a: pl.BlockSpec(a.shape, lambda i: (0,) * a.ndim)
    return pl.pallas_call(
        _ple_kernel,
        grid=(m // tm,),
        in_specs=[pl.BlockSpec((tm, d), lambda i: (i, 0)),
                  pl.BlockSpec((tm, p.shape[1]), lambda i: (i, 0)),
                  full(g_in), full(w_g), full(b_g), full(w_p), full(g_out)],
        out_specs=pl.BlockSpec((tm, d), lambda i: (i, 0)),
        out_shape=jax.ShapeDtypeStruct((m, d), F32),
        compiler_params=_cparams(1),
        name="ple",
    )(h, p, g_in, w_g, b_g, w_p, g_out)


def _pad_cols(a, width):
    return jnp.pad(a, ((0, 0), (0, width - a.shape[1])))


def _pad_rows(a, rows):
    return jnp.pad(a, ((0, rows - a.shape[0]), (0, 0)))


def _layer(h, p, positions, g_mix, w_in, mla_g_qa, mla_w_uq, mla_g_kva, mla_w_ukv, mla_g_qn,
           mla_g_kn, rw_mu, rw_w0, rw_w_w2, rw_a0, rw_a_w2, rw_g_w2, rw_k_k, rw_k_a, rw_r_k,
           rw_ln_w, rw_ln_b, w_out, g_ffn, w_router, router_bias, w_gate, w_up, w_down, ws_gate,
           ws_up, ws_down, g_ple_in, w_ple_gate, b_ple_gate, w_ple_proj, g_ple_out):
    b, t, d = h.shape
    n = b * t
    x2 = h.reshape(n, d)
    row = lambda a: a.reshape(1, -1)

    w_mla = _pad_cols(w_in[:, :MLA_COLS], MLA_COLS_PAD).astype(BF16)
    c = RW_C
    o = MLA_COLS
    lora = [(RW_DECAY_LORA, 128), (RW_A_LORA, 128), (RW_GATE_LORA, 256)]
    rw_parts = [w_in[:, o:o + 3 * c]]
    mu_parts = [rw_mu[:3 * c]]
    off = 3 * c
    for width, padded in lora:
        rw_parts.append(_pad_cols(w_in[:, o + off:o + off + width], padded))
        mu_parts.append(jnp.pad(rw_mu[off:off + width], (0, padded - width)))
        off += width
    w_rw = jnp.concatenate(rw_parts, axis=1).astype(BF16)
    mu = row(jnp.concatenate(mu_parts))
    ww2 = _pad_rows(rw_w_w2, 128).astype(BF16)
    aw2 = _pad_rows(rw_a_w2, 128).astype(BF16)
    gw2 = _pad_rows(rw_g_w2, 256).astype(BF16)
    head_of = jnp.arange(c) // RW_HEAD
    bd = (head_of[:, None] == head_of[None, :]).astype(BF16)

    wq = mla_w_uq.reshape(MLA_Q_LORA, MLA_HEADS, MLA_QK)
    w_uq = jnp.concatenate([wq[:, :, :MLA_NOPE].reshape(MLA_Q_LORA, -1),
                            wq[:, :, MLA_NOPE:].reshape(MLA_Q_LORA, -1)], axis=1).astype(BF16)
    w_ukv = mla_w_ukv.astype(BF16)
    half = MLA_ROPE // 2
    inv = ROPE_THETA ** (-jnp.arange(half, dtype=F32) / half)
    invf = row(jnp.concatenate([inv, inv]))
    sgn = row(jnp.concatenate([-jnp.ones(half, F32), jnp.ones(half, F32)]))

    pm = _norm_matmul(x2, row(g_mix), w_mla, 512, MLA_COLS_PAD, F32, "in_proj_mla")
    z = _norm_matmul(x2, row(g_mix), w_rw, 512, RW_COLS_PAD // 2, F32, "in_proj_rwkv")

    q, k, v = _mla_prep(pm.reshape(b, t, -1), positions.reshape(b, t, 1), invf, sgn,
                        row(mla_g_qa), w_uq, row(mla_g_kva), w_ukv, row(mla_g_qn), row(mla_g_kn),
                        512)
    y_mla = _attention(q, k, v, 1024, 512)

    rr, ww, kx, vx, kkx, bx, gx = _rw_prep(z.reshape(b, t, -1), mu, row(rw_w0), ww2, row(rw_a0),
                                           aw2, gw2, row(rw_k_k), row(rw_k_a), bd, 256)
    y_rw = _rw_chunk(rr, ww, kx, vx, kkx, bx, gx, row(rw_r_k), row(rw_ln_w), row(rw_ln_b))

    h1, hn, lg_t = _out_proj(y_mla.reshape(n, -1), y_rw.reshape(n, -1), x2, w_out.astype(BF16),
                             row(g_ffn), w_router.T, 512)

    comb = _router(lg_t, router_bias.reshape(-1, 1), 512)
    wg = jnp.concatenate([w_gate, ws_gate[None]], axis=0).astype(BF16)
    wu = jnp.concatenate([w_up, ws_up[None]], axis=0).astype(BF16)
    wd = jnp.concatenate([w_down, ws_down[None]], axis=0).astype(BF16)
    h2 = _moe_dense(hn, h1, comb, wg, wu, wd, 512)

    h3 = _ple(h2, p.reshape(n, -1), row(g_ple_in), w_ple_gate.astype(BF16), row(b_ple_gate),
              w_ple_proj.astype(BF16), row(g_ple_out), 512)
    return h3.reshape(b, t, d)


def kernel(x, p, positions, g_mix, w_in, mla_g_qa, mla_w_uq, mla_g_kva, mla_w_ukv, mla_g_qn, mla_g_kn, rw_mu, rw_w0, rw_w_w2, rw_a0, rw_a_w2, rw_g_w2, rw_k_k, rw_k_a, rw_r_k, rw_ln_w, rw_ln_b, w_out, g_ffn, w_router, router_bias, w_gate, w_up, w_down, ws_gate, ws_up, ws_down, g_ple_in, w_ple_gate, b_ple_gate, w_ple_proj, g_ple_out):
    params = (g_mix, w_in, mla_g_qa, mla_w_uq, mla_g_kva, mla_w_ukv, mla_g_qn, mla_g_kn, rw_mu,
              rw_w0, rw_w_w2, rw_a0, rw_a_w2, rw_g_w2, rw_k_k, rw_k_a, rw_r_k, rw_ln_w, rw_ln_b,
              w_out, g_ffn, w_router, router_bias, w_gate, w_up, w_down, ws_gate, ws_up, ws_down,
              g_ple_in, w_ple_gate, b_ple_gate, w_ple_proj, g_ple_out)
    h = x
    for i in range(g_mix.shape[0]):
        h = _layer(h, p[i], positions, *[a[i] for a in params])
    return h
```

```python
import functools

import jax
import jax.numpy as jnp
from jax import lax
from jax.experimental import pallas as pl
from jax.experimental.pallas import tpu as pltpu

F32 = jnp.float32
BF16 = jnp.bfloat16

D_MODEL = 2048
PLE_DIM = 256
RMS_EPS = 1e-6

MLA_HEADS = 8
MLA_NOPE = 128
MLA_ROPE = 64
MLA_QK = MLA_NOPE + MLA_ROPE
MLA_V = 128
MLA_Q_LORA = 512
MLA_KV_LORA = 256
ROPE_THETA = 10000.0
MLA_COLS = MLA_Q_LORA + MLA_KV_LORA + MLA_ROPE
MLA_COLS_PAD = 896

RW_HEADS = 16
RW_HEAD = 64
RW_C = RW_HEADS * RW_HEAD
RW_DECAY_LORA = 64
RW_A_LORA = 64
RW_GATE_LORA = 160
RW_GN_EPS = 64e-5
RW_LORA_PAD = 512
RW_COLS_PAD = 3 * RW_C + RW_LORA_PAD
RW_CHUNK = 64

N_EXPERTS = 64
TOP_K = 8
N_GROUPS = 8
TOPK_GROUPS = 4
EXPERT_DIM = 512
ROUTED_SCALE = 2.5
MOE_TILE = 256

VMEM_LIMIT = 56 * 1024 * 1024


def _cparams(n_axes):
    return pltpu.CompilerParams(dimension_semantics=("arbitrary",) * n_axes,
                                vmem_limit_bytes=VMEM_LIMIT)


def _rms(x, g):
    ms = jnp.mean(x * x, axis=-1, keepdims=True)
    return x * lax.rsqrt(ms + RMS_EPS) * g


def _dot(a, b):
    return jnp.dot(a, b, preferred_element_type=F32)


def _dot_nt(a, b):
    return lax.dot_general(a, b, (((1,), (1,)), ((), ())), preferred_element_type=F32)


def _dot_tn(a, b):
    return lax.dot_general(a, b, (((0,), (0,)), ((), ())), preferred_element_type=F32)


def _split2(x):
    hi = x.astype(BF16)
    lo = (x - hi.astype(F32)).astype(BF16)
    return hi, lo


def _cat3_lhs(x):
    hi = x.astype(BF16).astype(F32)
    return jnp.concatenate([hi, x - hi, hi], axis=1).astype(BF16)


def _cat3_rhs(y):
    hi, lo = _split2(y)
    return jnp.concatenate([hi, hi, lo], axis=0)


def _norm_matmul_kernel(x_ref, g_ref, w_ref, o_ref, xn_ref):
    @pl.when(pl.program_id(1) == 0)
    def _():
        xn_ref[...] = _rms(x_ref[...], g_ref[...]).astype(BF16)

    o_ref[...] = _dot(xn_ref[...], w_ref[...]).astype(o_ref.dtype)


def _norm_matmul(x, g, w, tm, tn, out_dtype, name):
    m, k = x.shape
    n = w.shape[1]
    tm = min(tm, m)
    return pl.pallas_call(
        _norm_matmul_kernel,
        grid=(m // tm, n // tn),
        in_specs=[pl.BlockSpec((tm, k), lambda i, j: (i, 0)),
                  pl.BlockSpec((1, k), lambda i, j: (0, 0)),
                  pl.BlockSpec((k, tn), lambda i, j: (0, j))],
        out_specs=pl.BlockSpec((tm, tn), lambda i, j: (i, j)),
        out_shape=jax.ShapeDtypeStruct((m, n), out_dtype),
        scratch_shapes=[pltpu.VMEM((tm, k), BF16)],
        compiler_params=_cparams(2),
        name=name,
    )(x, g, w)


def _mla_prep_kernel(pm_ref, pos_ref, invf_ref, sgn_ref, gqa_ref, wuq_ref, gkva_ref, wukv_ref,
                     gqn_ref, gkn_ref, q_ref, k_ref, v_ref):
    pm = pm_ref[0]
    ang = pos_ref[0].astype(F32) * invf_ref[...]
    cos = jnp.cos(ang)
    sin = jnp.sin(ang) * sgn_ref[...]

    def rope(xr):
        half = MLA_ROPE // 2
        swapped = jnp.concatenate([xr[:, half:], xr[:, :half]], axis=1)
        return xr * cos + swapped * sin

    scale = MLA_QK ** -0.5
    gqn = gqn_ref[...]
    gkn = gkn_ref[...]

    cq = _rms(pm[:, :MLA_Q_LORA], gqa_ref[...]).astype(BF16)
    q = _dot(cq, wuq_ref[...])
    ckv = _rms(pm[:, MLA_Q_LORA:MLA_Q_LORA + MLA_KV_LORA], gkva_ref[...]).astype(BF16)
    kv = _dot(ckv, wukv_ref[...])
    kpe = pm[:, MLA_Q_LORA + MLA_KV_LORA:MLA_COLS]
    kpe_ssq = jnp.sum(kpe * kpe, axis=-1, keepdims=True)
    kpe_rot = rope(kpe * gkn[:, MLA_NOPE:])

    for h in range(MLA_HEADS):
        qn = q[:, h * MLA_NOPE:(h + 1) * MLA_NOPE]
        qr = q[:, MLA_HEADS * MLA_NOPE + h * MLA_ROPE:MLA_HEADS * MLA_NOPE + (h + 1) * MLA_ROPE]
        ssq = jnp.sum(qn * qn, axis=-1, keepdims=True) + jnp.sum(qr * qr, axis=-1, keepdims=True)
        inv = lax.rsqrt(ssq * (1.0 / MLA_QK) + RMS_EPS) * scale
        q_ref[0, h, :, 0:MLA_NOPE] = (qn * inv * gqn[:, :MLA_NOPE]).astype(BF16)
        q_ref[0, h, :, MLA_NOPE:MLA_QK] = (rope(qr * gqn[:, MLA_NOPE:]) * inv).astype(BF16)

        kn = kv[:, h * 256:h * 256 + MLA_NOPE]
        ssq = jnp.sum(kn * kn, axis=-1, keepdims=True) + kpe_ssq
        inv = lax.rsqrt(ssq * (1.0 / MLA_QK) + RMS_EPS)
        k_ref[0, h, :, 0:MLA_NOPE] = (kn * inv * gkn[:, :MLA_NOPE]).astype(BF16)
        k_ref[0, h, :, MLA_NOPE:MLA_QK] = (kpe_rot * inv).astype(BF16)
        v_ref[0, h] = kv[:, h * 256 + MLA_NOPE:(h + 1) * 256].astype(BF16)


def _mla_prep(pm, pos, invf, sgn, g_qa, w_uq, g_kva, w_ukv, g_qn, g_kn, tm):
    b, t, _ = pm.shape
    tm = min(tm, t)
    full = lambda a: pl.BlockSpec(a.shape, lambda i, j: (0,) * a.ndim)
    qk_shape = jax.ShapeDtypeStruct((b, MLA_HEADS, t, MLA_QK), BF16)
    return pl.pallas_call(
        _mla_prep_kernel,
        grid=(b, t // tm),
        in_specs=[pl.BlockSpec((1, tm, MLA_COLS_PAD), lambda i, j: (i, j, 0)),
                  pl.BlockSpec((1, tm, 1), lambda i, j: (i, j, 0)),
                  full(invf), full(sgn), full(g_qa), full(w_uq), full(g_kva), full(w_ukv),
                  full(g_qn), full(g_kn)],
        out_specs=[pl.BlockSpec((1, MLA_HEADS, tm, MLA_QK), lambda i, j: (i, 0, j, 0)),
                   pl.BlockSpec((1, MLA_HEADS, tm, MLA_QK), lambda i, j: (i, 0, j, 0)),
                   pl.BlockSpec((1, MLA_HEADS, tm, MLA_V), lambda i, j: (i, 0, j, 0))],
        out_shape=[qk_shape, qk_shape, jax.ShapeDtypeStruct((b, MLA_HEADS, t, MLA_V), BF16)],
        compiler_params=_cparams(2),
        name="mla_prep",
    )(pm, pos, invf, sgn, g_qa, w_uq, g_kva, w_ukv, g_qn, g_kn)


def _attn_kernel(q_ref, k_ref, v_ref, o_ref, m_ref, l_ref, acc_ref, *, tq, tk):
    qi = pl.program_id(2)
    q = q_ref[0, 0]
    m_ref[...] = jnp.full(m_ref.shape, -jnp.inf, F32)
    l_ref[...] = jnp.zeros(l_ref.shape, F32)
    acc_ref[...] = jnp.zeros(acc_ref.shape, F32)

    def step(j, masked):
        start = pl.multiple_of(j * tk, tk)
        s = _dot_nt(q, k_ref[0, 0, pl.ds(start, tk), :])
        if masked:
            row = qi * tq + lax.broadcasted_iota(jnp.int32, (tq, tk), 0)
            col = j * tk + lax.broadcasted_iota(jnp.int32, (tq, tk), 1)
            s = jnp.where(col <= row, s, -jnp.inf)
        m_old = m_ref[...]
        m_new = jnp.maximum(m_old, jnp.max(s, axis=-1, keepdims=True))
        alpha = jnp.exp(m_old - m_new)
        p = jnp.exp(s - m_new)
        l_ref[...] = alpha * l_ref[...] + jnp.sum(p, axis=-1, keepdims=True)
        acc_ref[...] = alpha * acc_ref[...] + _dot(p.astype(BF16), v_ref[0, 0, pl.ds(start, tk), :])
        m_ref[...] = m_new

    n_full = (qi * tq) // tk

    def body(j, c):
        step(j, False)
        return c

    lax.fori_loop(0, n_full, body, 0)
    for d in range(tq // tk):
        step(n_full + d, True)
    o_ref[0] = (acc_ref[...] / l_ref[...]).astype(o_ref.dtype)


def _attention(q, k, v, tq, tk):
    b, h, t, _ = q.shape
    tq = min(tq, t)
    tk = min(tk, tq)
    return pl.pallas_call(
        functools.partial(_attn_kernel, tq=tq, tk=tk),
        grid=(b, h, t // tq),
        in_specs=[pl.BlockSpec((1, 1, tq, MLA_QK), lambda i, j, n: (i, j, n, 0)),
                  pl.BlockSpec((1, 1, t, MLA_QK), lambda i, j, n: (i, j, 0, 0)),
                  pl.BlockSpec((1, 1, t, MLA_V), lambda i, j, n: (i, j, 0, 0))],
        out_specs=pl.BlockSpec((1, tq, MLA_V), lambda i, j, n: (i, n, j)),
        out_shape=jax.ShapeDtypeStruct((b, t, h * MLA_V), BF16),
        scratch_shapes=[pltpu.VMEM((tq, 1), F32), pltpu.VMEM((tq, 1), F32),
                        pltpu.VMEM((tq, MLA_V), F32)],
        compiler_params=_cparams(3),
        name="mla_attention",
    )(q, k, v)


def _rw_prep_kernel(z_ref, zp_ref, mu_ref, w0_ref, ww2_ref, a0_ref, aw2_ref, gw2_ref, kk_ref,
                    ka_ref, bd_ref, r_o, w_o, k_o, v_o, kk_o, b_o, g_o):
    z = z_ref[0]
    prev = zp_ref[0][7:8, :]
    prev = jnp.where(pl.program_id(1) == 0, jnp.zeros_like(prev), prev)
    row = lax.broadcasted_iota(jnp.int32, z.shape, 0)
    zs = jnp.where(row == 0, prev, pltpu.roll(z, 1, axis=0))
    z = z + (zs - z) * mu_ref[...]

    c = RW_C
    r = z[:, 0:c]
    k = z[:, c:2 * c]
    v = z[:, 2 * c:3 * c]
    wl = z[:, 3 * c:3 * c + 128]
    al = z[:, 3 * c + 128:3 * c + 256]
    gl = z[:, 3 * c + 256:3 * c + 512]

    f = w0_ref[...] + _dot(jnp.tanh(wl).astype(BF16), ww2_ref[...])
    softplus_neg_f = jnp.maximum(-f, 0.0) + jnp.log(1.0 + jnp.exp(-jnp.abs(f)))
    w = -softplus_neg_f - 0.5
    w_o[0] = -jnp.exp(w)
    a = jax.nn.sigmoid(a0_ref[...] + _dot(al.astype(BF16), aw2_ref[...]))
    g_o[0] = _dot(jax.nn.sigmoid(gl).astype(BF16), gw2_ref[...])

    kk = k * kk_ref[...]
    hi, lo = _split2(kk * kk)
    ssq = _dot(hi, bd_ref[...]) + _dot(lo, bd_ref[...])
    kk = kk / jnp.maximum(jnp.sqrt(ssq), 1e-12)
    r_o[0] = r
    k_o[0] = k * (1.0 + (a - 1.0) * ka_ref[...])
    v_o[0] = v
    kk_o[0] = kk
    b_o[0] = kk * a


def _rw_prep(z, mu, w0, ww2, a0, aw2, gw2, k_k, k_a, bd, tm):
    b, t, cols = z.shape
    tm = min(tm, t)
    full = lambda a: pl.BlockSpec(a.shape, lambda i, j: (0,) * a.ndim)
    out = jax.ShapeDtypeStruct((b, t, RW_C), F32)
    ospec = pl.BlockSpec((1, tm, RW_C), lambda i, j: (i, j, 0))
    return pl.pallas_call(
        _rw_prep_kernel,
        grid=(b, t // tm),
        in_specs=[pl.BlockSpec((1, tm, cols), lambda i, j: (i, j, 0)),
                  pl.BlockSpec((1, 8, cols), lambda i, j: (i, jnp.maximum(j * (tm // 8) - 1, 0), 0)),
                  full(mu), full(w0), full(ww2), full(a0), full(aw2), full(gw2), full(k_k),
                  full(k_a), full(bd)],
        out_specs=[ospec] * 7,
        out_shape=[out] * 7,
        compiler_params=_cparams(2),
        name="rwkv_prep",
    )(z, z, mu, w0, ww2, a0, aw2, gw2, k_k, k_a, bd)


def _rw_chunk_kernel(r_ref, w_ref, k_ref, v_ref, kk_ref, b_ref, g_ref, rk_ref, lnw_ref, lnb_ref,
                     o_ref, s_ref):
    cs = RW_CHUNK
    hd = RW_HEAD

    @pl.when(pl.program_id(1) == 0)
    def _():
        s_ref[...] = jnp.zeros(s_ref.shape, F32)

    r = r_ref[0]
    w = w_ref[0]
    k = k_ref[0]
    v = v_ref[0]
    kk = kk_ref[0]
    b = b_ref[0]
    heads = range(RW_HEADS)

    row = lax.broadcasted_iota(jnp.int32, (cs, cs), 0)
    col = lax.broadcasted_iota(jnp.int32, (cs, cs), 1)
    strict = row > col
    eye = jnp.where(row == col, 1.0, 0.0).astype(F32)
    row3 = lax.broadcasted_iota(jnp.int32, (cs, 3 * cs), 0)
    col3 = lax.broadcasted_iota(jnp.int32, (cs, 3 * cs), 1)
    tri3 = jnp.where(row3 >= jnp.bitwise_and(col3, cs - 1), 1.0, 0.0).astype(BF16)
    row2 = lax.broadcasted_iota(jnp.int32, (cs, 2 * cs), 0)
    col2 = lax.broadcasted_iota(jnp.int32, (cs, 2 * cs), 1)
    mask2 = row2 >= jnp.bitwise_and(col2, cs - 1)

    w1 = w.astype(BF16)
    w2 = (w - w1.astype(F32)).astype(BF16)
    w3 = (w - w1.astype(F32) - w2.astype(F32)).astype(BF16)
    logp = _dot(tri3, jnp.concatenate([w1, w2, w3], axis=0))
    logp_end = logp[cs - 1:cs, :]
    p_tail = jnp.exp(logp_end - logp)
    p_end = jnp.exp(logp_end)
    p_inv = jnp.exp(-logp)

    a_t = -kk * jnp.exp(logp - w)
    r_t = r * jnp.exp(logp)
    b_t = b * p_inv
    k_t = k * p_inv
    b_h = b * p_tail
    k_h = k * p_tail

    sls = [slice(h * hd, (h + 1) * hd) for h in heads]
    lhs = [jnp.concatenate([a_t[:, sl], r_t[:, sl]], axis=0).astype(BF16) for sl in sls]
    rhs = [jnp.concatenate([b_t[:, sl], k_t[:, sl]], axis=0).astype(BF16) for sl in sls]
    gm = [_dot_nt(lhs[h], rhs[h]) for h in heads]
    s0 = [s_ref[h] for h in heads]
    ls = [_dot_nt(lhs[h], s0[h].astype(BF16)) for h in heads]
    a_ak = [jnp.where(strict, gm[h][:cs, cs:], 0.0).astype(BF16) for h in heads]
    vb = [v[:, sl].astype(BF16) for sl in sls]
    rhs_u = [ls[h][:cs] + _dot(a_ak[h], vb[h]) for h in heads]

    x = [jnp.where(strict, gm[h][:cs, :cs], 0.0) for h in heads]
    tm = [eye + x[h] for h in heads]
    for _ in range(5):
        xr = [_cat3_rhs(x[h]) for h in heads]
        x = [_dot(_cat3_lhs(x[h]), xr[h]) for h in heads]
        xr = [_cat3_rhs(x[h]) for h in heads]
        tm = [tm[h] + _dot(_cat3_lhs(tm[h]), xr[h]) for h in heads]

    u = [_dot(_cat3_lhs(tm[h]), _cat3_rhs(rhs_u[h])) for h in heads]
    uv = [jnp.concatenate([u[h].astype(BF16), vb[h]], axis=0) for h in heads]
    a_r = [jnp.where(mask2, gm[h][cs:, :], 0.0).astype(BF16) for h in heads]
    y = [ls[h][cs:] + _dot(a_r[h], uv[h]) for h in heads]
    for h in heads:
        bk = jnp.concatenate([b_h[:, sls[h]], k_h[:, sls[h]]], axis=0).astype(BF16)
        s_ref[h] = s0[h] * p_end[:, sls[h]] + _dot_tn(uv[h], bk)

    rk_all = r * k * rk_ref[...]
    g = g_ref[0]
    for h in heads:
        sl = sls[h]
        mean = jnp.mean(y[h], axis=-1, keepdims=True)
        yc = y[h] - mean
        var = jnp.mean(yc * yc, axis=-1, keepdims=True)
        yn = yc * lax.rsqrt(var + RW_GN_EPS) * lnw_ref[:, sl] + lnb_ref[:, sl]
        bonus = jnp.sum(rk_all[:, sl], axis=-1, keepdims=True) * v[:, sl]
        o_ref[0, :, sl] = ((yn + bonus) * g[:, sl]).astype(o_ref.dtype)


def _rw_chunk(r, w, k, v, kk, bb, g, r_k, ln_w, ln_b):
    b, t, c = r.shape
    cs = RW_CHUNK
    full = lambda a: pl.BlockSpec(a.shape, lambda i, j: (0,) * a.ndim)
    spec = pl.BlockSpec((1, cs, c), lambda i, j: (i, j, 0))
    return pl.pallas_call(
        _rw_chunk_kernel,
        grid=(b, t // cs),
        in_specs=[spec] * 7 + [full(r_k), full(ln_w), full(ln_b)],
        out_specs=spec,
        out_shape=jax.ShapeDtypeStruct((b, t, c), BF16),
        scratch_shapes=[pltpu.VMEM((RW_HEADS, RW_HEAD, RW_HEAD), F32)],
        compiler_params=_cparams(2),
        name="rwkv_chunk",
    )(r, w, k, v, kk, bb, g, r_k, ln_w, ln_b)


def _pack_pair(a, b):
    ua = lax.bitcast_convert_type(a.astype(BF16).astype(F32), jnp.uint32)
    ub = lax.bitcast_convert_type(b.astype(BF16).astype(F32), jnp.uint32)
    return lax.shift_right_logical(ua, jnp.uint32(16)) | (ub & jnp.uint32(0xFFFF0000))


def _unpack_pair(u):
    lo = lax.bitcast_convert_type(lax.shift_left(u, jnp.uint32(16)), F32)
    hi = lax.bitcast_convert_type(u & jnp.uint32(0xFFFF0000), F32)
    return lo, hi


def _out_proj_kernel(ya_ref, yr_ref, x_ref, w_ref, g_ref, wr_ref, h_ref, hn_ref, lg_ref):
    half = ya_ref.shape[1]
    acc = _dot(ya_ref[...], w_ref[0:half, :]) + _dot(yr_ref[...], w_ref[half:, :])
    h = x_ref[...] + acc
    h_ref[...] = h
    hn = _rms(h, g_ref[...])
    d2 = hn.shape[1] // 2
    hn_ref[...] = _pack_pair(hn[:, :d2], hn[:, d2:])
    lg_ref[...] = lax.dot_general(wr_ref[...], hn, (((1,), (1,)), ((), ())),
                                  precision=lax.Precision.HIGHEST, preferred_element_type=F32)


def _out_proj(ya, yr, x, w, g, wr_t, tm):
    m, d = x.shape
    tm = min(tm, m)
    full = lambda a: pl.BlockSpec(a.shape, lambda i: (0,) * a.ndim)
    return pl.pallas_call(
        _out_proj_kernel,
        grid=(m // tm,),
        in_specs=[pl.BlockSpec((tm, ya.shape[1]), lambda i: (i, 0)),
                  pl.BlockSpec((tm, yr.shape[1]), lambda i: (i, 0)),
                  pl.BlockSpec((tm, d), lambda i: (i, 0)),
                  full(w), full(g), full(wr_t)],
        out_specs=[pl.BlockSpec((tm, d), lambda i: (i, 0)),
                   pl.BlockSpec((tm, d // 2), lambda i: (i, 0)),
                   pl.BlockSpec((N_EXPERTS, tm), lambda i: (0, i))],
        out_shape=[jax.ShapeDtypeStruct((m, d), F32), jax.ShapeDtypeStruct((m, d // 2), jnp.uint32),
                   jax.ShapeDtypeStruct((N_EXPERTS, m), F32)],
        compiler_params=_cparams(1),
        name="out_proj_router_logits",
    )(ya, yr, x, w, g, wr_t)


def _router_kernel(lg_ref, bias_ref, e_ref, r_ref, w_ref, cnt_ref, carry_ref):
    tm = lg_ref.shape[1]
    gsz = N_EXPERTS // N_GROUPS

    @pl.when(pl.program_id(0) == 0)
    def _():
        carry_ref[...] = jnp.zeros(carry_ref.shape, F32)

    scores = jax.nn.sigmoid(lg_ref[...]).reshape(N_GROUPS, gsz, tm)
    biased = scores + bias_ref[...].reshape(N_GROUPS, gsz, 1)
    neg = -jnp.inf

    eidx = lax.broadcasted_iota(jnp.int32, biased.shape, 1)
    m1 = jnp.max(biased, axis=1, keepdims=True)
    first = jnp.min(jnp.where(biased == m1, eidx, gsz), axis=1, keepdims=True)
    m2 = jnp.max(jnp.where(eidx == first, neg, biased), axis=1, keepdims=True)
    gscore = m1 + m2

    gidx = lax.broadcasted_iota(jnp.int32, gscore.shape, 0)
    gsel = jnp.zeros(gscore.shape, jnp.bool_)
    work = gscore
    for _ in range(TOPK_GROUPS):
        best = jnp.max(work, axis=0, keepdims=True)
        pick = jnp.min(jnp.where(work == best, gidx, N_GROUPS), axis=0, keepdims=True)
        hit = gidx == pick
        gsel = jnp.logical_or(gsel, hit)
        work = jnp.where(hit, neg, work)

    flat = lax.broadcasted_iota(jnp.int32, biased.shape, 0) * gsz + eidx
    work = jnp.where(gsel, biased, neg)
    hits = []
    picks = []
    for _ in range(TOP_K):
        best = jnp.max(jnp.max(work, axis=1, keepdims=True), axis=0, keepdims=True)
        cand = jnp.where(work == best, flat, N_EXPERTS)
        pick = jnp.min(jnp.min(cand, axis=1, keepdims=True), axis=0, keepdims=True)
        hit = flat == pick
        hits.append(hit)
        picks.append(pick)
        work = jnp.where(hit, neg, work)

    def pick_value(hit, val):
        s = jnp.sum(jnp.sum(jnp.where(hit, val, 0.0), axis=1, keepdims=True), axis=0, keepdims=True)
        return s.reshape(1, tm)

    sel = jnp.zeros(biased.shape, F32)
    for hit in hits:
        sel = jnp.where(hit, 1.0, sel)
    sel2 = sel.reshape(N_EXPERTS, tm)
    r_i = lax.broadcasted_iota(jnp.int32, (tm, tm), 0)
    c_i = lax.broadcasted_iota(jnp.int32, (tm, tm), 1)
    upper = jnp.where(r_i < c_i, 1.0, 0.0).astype(BF16)
    rank = _dot(sel2.astype(BF16), upper) + carry_ref[...]
    carry = carry_ref[...] + jnp.sum(sel2, axis=1, keepdims=True)
    carry_ref[...] = carry
    cnt_ref[...] = carry
    rank3 = rank.reshape(N_GROUPS, gsz, tm)

    raw = [pick_value(hit, scores) for hit in hits]
    tot = raw[0]
    for x in raw[1:]:
        tot = tot + x
    wrow = lax.broadcasted_iota(jnp.int32, (N_EXPERTS, tm), 0)
    wt = jnp.zeros((N_EXPERTS, tm), F32)
    for j in range(TOP_K):
        e_ref[j:j + 1, :] = picks[j].reshape(1, tm)
        r_ref[j:j + 1, :] = pick_value(hits[j], rank3).astype(jnp.int32)
        wt = jnp.where(wrow == j, raw[j] / tot * ROUTED_SCALE, wt)
    w_ref[...] = wt.T


def _router(lg_t, bias, tm):
    e, m = lg_t.shape
    tm = min(tm, m)
    row = pl.BlockSpec((TOP_K, tm), lambda i: (0, i))
    return pl.pallas_call(
        _router_kernel,
        grid=(m // tm,),
        in_specs=[pl.BlockSpec((e, tm), lambda i: (0, i)),
                  pl.BlockSpec((e, 1), lambda i: (0, 0))],
        out_specs=[row, row, pl.BlockSpec((tm, e), lambda i: (i, 0)),
                   pl.BlockSpec((e, 1), lambda i: (0, 0))],
        out_shape=[jax.ShapeDtypeStruct((TOP_K, m), jnp.int32),
                   jax.ShapeDtypeStruct((TOP_K, m), jnp.int32),
                   jax.ShapeDtypeStruct((m, e), F32),
                   jax.ShapeDtypeStruct((e, 1), F32)],
        scratch_shapes=[pltpu.VMEM((e, 1), F32)],
        compiler_params=_cparams(1),
        name="router_topk",
    )(lg_t, bias)


def _dest_kernel(off_ref, e_ref, r_ref, d_ref):
    e = e_ref[...]
    acc = r_ref[...]
    for x in range(N_EXPERTS):
        acc = acc + jnp.where(e == x, off_ref[x], 0)
    d_ref[...] = acc


def _dest(off, eidx, rank, tm):
    k, m = eidx.shape
    tm = min(tm, m)
    spec = pl.BlockSpec((k, tm), lambda i: (0, i))
    return pl.pallas_call(
        _dest_kernel,
        grid=(m // tm,),
        in_specs=[pl.BlockSpec(memory_space=pltpu.SMEM), spec, spec],
        out_specs=spec,
        out_shape=jax.ShapeDtypeStruct((k, m), jnp.int32),
        compiler_params=_cparams(1),
        name="moe_dest_rows",
    )(off, eidx, rank)


def _scatter_kernel(zt_ref, d_ref, x_hbm, z_ref, xs_hbm, sem, *, tm, tile):
    i = pl.program_id(0)

    @pl.when(i == 0)
    def _():
        def zero(e, c):
            start = pl.multiple_of(zt_ref[e] * tile, tile)
            pltpu.make_async_copy(z_ref, xs_hbm.at[pl.ds(start, tile)], sem).start()
            return c

        lax.fori_loop(0, N_EXPERTS, zero, 0)

        def zwait(e, c):
            pltpu.make_async_copy(z_ref, xs_hbm.at[pl.ds(0, tile)], sem).wait()
            return c

        lax.fori_loop(0, N_EXPERTS, zwait, 0)

    base = i * tm

    def body(n, c):
        for j in range(TOP_K):
            pltpu.make_async_copy(x_hbm.at[pl.ds(base + n, 1)], xs_hbm.at[pl.ds(d_ref[j, n], 1)],
                                  sem).start()
        return c

    lax.fori_loop(0, tm, body, 0)

    def wait(n, c):
        for j in range(TOP_K):
            pltpu.make_async_copy(x_hbm.at[pl.ds(0, 1)], xs_hbm.at[pl.ds(0, 1)], sem).wait()
        return c

    lax.fori_loop(0, tm, wait, 0)


def _scatter_rows(zero_tile, dest, xp, rows, tile, tm):
    m, w = xp.shape
    tm = min(tm, m)
    return pl.pallas_call(
        functools.partial(_scatter_kernel, tm=tm, tile=tile),
        grid_spec=pltpu.PrefetchScalarGridSpec(
            num_scalar_prefetch=1,
            grid=(m // tm,),
            in_specs=[pl.BlockSpec((TOP_K, tm), lambda i, zt: (0, i), memory_space=pltpu.SMEM),
                      pl.BlockSpec(memory_space=pl.ANY),
                      pl.BlockSpec((tile, w), lambda i, zt: (0, 0))],
            out_specs=pl.BlockSpec(memory_space=pl.ANY),
            scratch_shapes=[pltpu.SemaphoreType.DMA(())],
        ),
        out_shape=jax.ShapeDtypeStruct((rows, w), jnp.uint32),
        compiler_params=_cparams(1),
        name="moe_scatter_rows",
    )(zero_tile, dest, xp, jnp.zeros((tile, w), jnp.uint32))


def _expert_kernel(te_ref, nu_ref, x_ref, wg_ref, wu_ref, wd_ref, o_ref, wgb, wub, wdb):
    t = pl.program_id(0)
    tc = jnp.minimum(t, nu_ref[0] - 1)
    prev = jnp.maximum(tc - 1, 0)
    new_expert = jnp.logical_or(t == 0, te_ref[tc] != te_ref[prev])

    @pl.when(jnp.logical_and(new_expert, t < nu_ref[0]))
    def _():
        wgb[...] = wg_ref[0].astype(BF16)
        wub[...] = wu_ref[0].astype(BF16)
        wdb[...] = wd_ref[0].astype(BF16)

    @pl.when(t < nu_ref[0])
    def _():
        lo, hi = _unpack_pair(x_ref[...])
        lo = lo.astype(BF16)
        hi = hi.astype(BF16)
        d2 = lo.shape[1]
        gate = _dot(lo, wgb[0:d2, :]) + _dot(hi, wgb[d2:, :])
        up = _dot(lo, wub[0:d2, :]) + _dot(hi, wub[d2:, :])
        mid = (gate * jax.nn.sigmoid(gate) * up).astype(BF16)
        y = _dot(mid, wdb[...])
        o_ref[...] = _pack_pair(y[:, :d2], y[:, d2:])


def _experts(tile_expert, n_used, xs, wg, wu, wd, tile):
    rows, w = xs.shape
    d = 2 * w
    clamp = lambda t, te, nu: jnp.minimum(t, nu[0] - 1)
    return pl.pallas_call(
        _expert_kernel,
        grid_spec=pltpu.PrefetchScalarGridSpec(
            num_scalar_prefetch=2,
            grid=(rows // tile,),
            in_specs=[pl.BlockSpec((tile, w), lambda t, te, nu: (clamp(t, te, nu), 0)),
                      pl.BlockSpec((1, d, EXPERT_DIM), lambda t, te, nu: (te[clamp(t, te, nu)], 0, 0)),
                      pl.BlockSpec((1, d, EXPERT_DIM), lambda t, te, nu: (te[clamp(t, te, nu)], 0, 0)),
                      pl.BlockSpec((1, EXPERT_DIM, d), lambda t, te, nu: (te[clamp(t, te, nu)], 0, 0))],
            out_specs=pl.BlockSpec((tile, w), lambda t, te, nu: (clamp(t, te, nu), 0)),
            scratch_shapes=[pltpu.VMEM((d, EXPERT_DIM), BF16), pltpu.VMEM((d, EXPERT_DIM), BF16),
                            pltpu.VMEM((EXPERT_DIM, d), BF16)],
        ),
        out_shape=jax.ShapeDtypeStruct((rows, w), jnp.uint32),
        compiler_params=_cparams(1),
        name="moe_experts",
    )(tile_expert, n_used, xs, wg, wu, wd)


def _combine_kernel(d_ref, ys_hbm, w_ref, h_ref, xp_ref, sg_ref, su_ref, sd_ref, o_ref, buf, sem,
                    *, tm):
    def body(n, c):
        for j in range(TOP_K):
            pltpu.make_async_copy(ys_hbm.at[pl.ds(d_ref[j, n], 1)], buf.at[j, pl.ds(n, 1)], sem).start()
        return c

    lax.fori_loop(0, tm, body, 0)

    lo, hi = _unpack_pair(xp_ref[...])
    lo = lo.astype(BF16)
    hi = hi.astype(BF16)
    d2 = lo.shape[1]
    gate = _dot(lo, sg_ref[0:d2, :]) + _dot(hi, sg_ref[d2:, :])
    up = _dot(lo, su_ref[0:d2, :]) + _dot(hi, su_ref[d2:, :])
    mid = (gate * jax.nn.sigmoid(gate) * up).astype(BF16)
    shared = _dot(mid, sd_ref[...])
    acc_lo = h_ref[:, 0:d2] + shared[:, :d2]
    acc_hi = h_ref[:, d2:] + shared[:, d2:]

    def wait(n, c):
        for j in range(TOP_K):
            pltpu.make_async_copy(ys_hbm.at[pl.ds(0, 1)], buf.at[0, pl.ds(0, 1)], sem).wait()
        return c

    lax.fori_loop(0, tm, wait, 0)

    r_lo = jnp.zeros(acc_lo.shape, F32)
    r_hi = jnp.zeros(acc_hi.shape, F32)
    wts = w_ref[...]
    for j in range(TOP_K):
        ylo, yhi = _unpack_pair(buf[j])
        wj = wts[:, j:j + 1]
        r_lo = r_lo + wj * ylo
        r_hi = r_hi + wj * yhi
    o_ref[:, 0:d2] = acc_lo + r_lo
    o_ref[:, d2:] = acc_hi + r_hi


def _combine(dest, ys, wcol, h, xp, sg, su, sd, tm):
    m, d = h.shape
    w = d // 2
    tm = min(tm, m)
    full = lambda a: pl.BlockSpec(a.shape, lambda i: (0,) * a.ndim)
    return pl.pallas_call(
        functools.partial(_combine_kernel, tm=tm),
        grid=(m // tm,),
        in_specs=[pl.BlockSpec((TOP_K, tm), lambda i: (0, i), memory_space=pltpu.SMEM),
                  pl.BlockSpec(memory_space=pl.ANY),
                  pl.BlockSpec((tm, N_EXPERTS), lambda i: (i, 0)),
                  pl.BlockSpec((tm, d), lambda i: (i, 0)),
                  pl.BlockSpec((tm, w), lambda i: (i, 0)),
                  full(sg), full(su), full(sd)],
        out_specs=pl.BlockSpec((tm, d), lambda i: (i, 0)),
        out_shape=jax.ShapeDtypeStruct((m, d), F32),
        scratch_shapes=[pltpu.VMEM((TOP_K, tm, w), jnp.uint32), pltpu.SemaphoreType.DMA(())],
        compiler_params=_cparams(1),
        name="moe_combine_shared",
    )(dest, ys, wcol, h, xp, sg, su, sd)


def _moe(h1, hnp, lg_t, router_bias, w_gate, w_up, w_down, ws_gate, ws_up, ws_down):
    n = h1.shape[0]
    tile = MOE_TILE
    eidx, rank, wcol, cnt = _router(lg_t, router_bias.reshape(-1, 1), 512)
    counts = cnt[:, 0].astype(jnp.int32)
    tiles_per = (counts + tile - 1) // tile
    tile_end = jnp.cumsum(tiles_per)
    tile_start = tile_end - tiles_per
    n_tiles = (n * TOP_K) // tile + N_EXPERTS
    rows = n_tiles * tile
    tile_expert = jnp.minimum(jnp.searchsorted(tile_end, jnp.arange(n_tiles), side="right"),
                              N_EXPERTS - 1).astype(jnp.int32)
    n_used = tile_end[-1:].astype(jnp.int32)
    last_tile = jnp.clip(tile_end - 1, 0, n_tiles - 1).astype(jnp.int32)
    dest = _dest((tile_start * tile).astype(jnp.int32), eidx, rank, 2048)
    xs = _scatter_rows(last_tile, dest, hnp, rows, tile, 512)
    ys = _experts(tile_expert, n_used, xs, w_gate, w_up, w_down, tile)
    return _combine(dest, ys, wcol, h1, hnp, ws_gate.astype(BF16), ws_up.astype(BF16),
                    ws_down.astype(BF16), 256)


def _ple_kernel(h_ref, p_ref, gin_ref, wg_ref, bg_ref, wp_ref, gout_ref, o_ref):
    h = h_ref[...]
    gate = jax.nn.sigmoid(_dot(_rms(h, gin_ref[...]).astype(BF16), wg_ref[...]) + bg_ref[...])
    pp = _dot(p_ref[...].astype(BF16), wp_ref[...])
    o_ref[...] = h + _rms(pp * gate, gout_ref[...])


def _ple(h, p, g_in, w_g, b_g, w_p, g_out, tm):
    m, d = h.shape
    tm = min(tm, m)
    full = lambda a: pl.BlockSpec(a.shape, lambda i: (0,) * a.ndim)
    return pl.pallas_call(
        _ple_kernel,
        grid=(m // tm,),
        in_specs=[pl.BlockSpec((tm, d), lambda i: (i, 0)),
                  pl.BlockSpec((tm, p.shape[1]), lambda i: (i, 0)),
                  full(g_in), full(w_g), full(b_g), full(w_p), full(g_out)],
        out_specs=pl.BlockSpec((tm, d), lambda i: (i, 0)),
        out_shape=jax.ShapeDtypeStruct((m, d), F32),
        compiler_params=_cparams(1),
        name="ple",
    )(h, p, g_in, w_g, b_g, w_p, g_out)


def _pad_cols(a, width):
    return jnp.pad(a, ((0, 0), (0, width - a.shape[1])))


def _pad_rows(a, rows):
    return jnp.pad(a, ((0, rows - a.shape[0]), (0, 0)))


def _row(a):
    return a.reshape(1, -1)


def _rw_layout(a):
    c3 = 3 * RW_C
    parts = [a[..., :c3]]
    off = c3
    for width, padded in ((RW_DECAY_LORA, 128), (RW_A_LORA, 128), (RW_GATE_LORA, 256)):
        pad = [(0, 0)] * (a.ndim - 1) + [(0, padded - width)]
        parts.append(jnp.pad(a[..., off:off + width], pad))
        off += width
    return jnp.concatenate(parts, axis=-1)


def _mla_from_pm(pm, positions, g_qa, w_uq, g_kva, w_ukv, g_qn, g_kn):
    b, t, _ = pm.shape
    wq = w_uq.reshape(MLA_Q_LORA, MLA_HEADS, MLA_QK)
    wq = jnp.concatenate([wq[:, :, :MLA_NOPE].reshape(MLA_Q_LORA, -1),
                          wq[:, :, MLA_NOPE:].reshape(MLA_Q_LORA, -1)], axis=1).astype(BF16)
    half = MLA_ROPE // 2
    inv = ROPE_THETA ** (-jnp.arange(half, dtype=F32) / half)
    invf = _row(jnp.concatenate([inv, inv]))
    sgn = _row(jnp.concatenate([-jnp.ones(half, F32), jnp.ones(half, F32)]))
    q, k, v = _mla_prep(pm, positions.reshape(b, t, 1), invf, sgn, _row(g_qa), wq, _row(g_kva),
                        w_ukv.astype(BF16), _row(g_qn), _row(g_kn), 512)
    return _attention(q, k, v, 1024, 512)


def _rwkv_from_z(z, mu, w0, w_w2, a0, a_w2, g_w2, k_k, k_a, r_k, ln_w, ln_b):
    head_of = jnp.arange(RW_C) // RW_HEAD
    bd = (head_of[:, None] == head_of[None, :]).astype(BF16)
    rr, ww, kx, vx, kkx, bx, gx = _rw_prep(
        z, _row(_rw_layout(mu)), _row(w0), _pad_rows(w_w2, 128).astype(BF16), _row(a0),
        _pad_rows(a_w2, 128).astype(BF16), _pad_rows(g_w2, 256).astype(BF16), _row(k_k), _row(k_a),
        bd, 256)
    return _rw_chunk(rr, ww, kx, vx, kkx, bx, gx, _row(r_k), _row(ln_w), _row(ln_b))


def _layer(h, p, positions, g_mix, w_in, mla_g_qa, mla_w_uq, mla_g_kva, mla_w_ukv, mla_g_qn,
           mla_g_kn, rw_mu, rw_w0, rw_w_w2, rw_a0, rw_a_w2, rw_g_w2, rw_k_k, rw_k_a, rw_r_k,
           rw_ln_w, rw_ln_b, w_out, g_ffn, w_router, router_bias, w_gate, w_up, w_down, ws_gate,
           ws_up, ws_down, g_ple_in, w_ple_gate, b_ple_gate, w_ple_proj, g_ple_out):
    b, t, d = h.shape
    n = b * t
    x2 = h.reshape(n, d)

    w_mla = _pad_cols(w_in[:, :MLA_COLS], MLA_COLS_PAD).astype(BF16)
    w_rw = _rw_layout(w_in[:, MLA_COLS:]).astype(BF16)
    pm = _norm_matmul(x2, _row(g_mix), w_mla, 512, MLA_COLS_PAD, F32, "in_proj_mla")
    z = _norm_matmul(x2, _row(g_mix), w_rw, 512, RW_COLS_PAD // 2, F32, "in_proj_rwkv")
    y_mla = _mla_from_pm(pm.reshape(b, t, -1), positions, mla_g_qa, mla_w_uq, mla_g_kva, mla_w_ukv,
                         mla_g_qn, mla_g_kn)
    y_rw = _rwkv_from_z(z.reshape(b, t, -1), rw_mu, rw_w0, rw_w_w2, rw_a0, rw_a_w2, rw_g_w2, rw_k_k,
                        rw_k_a, rw_r_k, rw_ln_w, rw_ln_b)

    h1, hnp, lg_t = _out_proj(y_mla.reshape(n, -1), y_rw.reshape(n, -1), x2, w_out.astype(BF16),
                              _row(g_ffn), w_router.T, 512)
    h2 = _moe(h1, hnp, lg_t, router_bias, w_gate, w_up, w_down, ws_gate, ws_up, ws_down)

    h3 = _ple(h2, p.reshape(n, -1), _row(g_ple_in), w_ple_gate.astype(BF16), _row(b_ple_gate),
              w_ple_proj.astype(BF16), _row(g_ple_out), 512)
    return h3.reshape(b, t, d)


def kernel(x, p, positions, g_mix, w_in, mla_g_qa, mla_w_uq, mla_g_kva, mla_w_ukv, mla_g_qn, mla_g_kn, rw_mu, rw_w0, rw_w_w2, rw_a0, rw_a_w2, rw_g_w2, rw_k_k, rw_k_a, rw_r_k, rw_ln_w, rw_ln_b, w_out, g_ffn, w_router, router_bias, w_gate, w_up, w_down, ws_gate, ws_up, ws_down, g_ple_in, w_ple_gate, b_ple_gate, w_ple_proj, g_ple_out):
    params = (g_mix, w_in, mla_g_qa, mla_w_uq, mla_g_kva, mla_w_ukv, mla_g_qn, mla_g_kn, rw_mu,
              rw_w0, rw_w_w2, rw_a0, rw_a_w2, rw_g_w2, rw_k_k, rw_k_a, rw_r_k, rw_ln_w, rw_ln_b,
              w_out, g_ffn, w_router, router_bias, w_gate, w_up, w_down, ws_gate, ws_up, ws_down,
              g_ple_in, w_ple_gate, b_ple_gate, w_ple_proj, g_ple_out)
    h = x
    for i in range(g_mix.shape[0]):
        h = _layer(h, p[i], positions, *[a[i] for a in params])
    return h
```

```python
import functools

import jax
import jax.numpy as jnp
from jax import lax
from jax.experimental import pallas as pl
from jax.experimental.pallas import tpu as pltpu

F32 = jnp.float32
BF16 = jnp.bfloat16

D_MODEL = 2048
PLE_DIM = 256
RMS_EPS = 1e-6

MLA_HEADS = 8
MLA_NOPE = 128
MLA_ROPE = 64
MLA_QK = MLA_NOPE + MLA_ROPE
MLA_V = 128
MLA_Q_LORA = 512
MLA_KV_LORA = 256
ROPE_THETA = 10000.0
MLA_COLS = MLA_Q_LORA + MLA_KV_LORA + MLA_ROPE
MLA_COLS_PAD = 896

RW_HEADS = 16
RW_HEAD = 64
RW_C = RW_HEADS * RW_HEAD
RW_DECAY_LORA = 64
RW_A_LORA = 64
RW_GATE_LORA = 160
RW_GN_EPS = 64e-5
RW_LORA_PAD = 512
RW_COLS_PAD = 3 * RW_C + RW_LORA_PAD
RW_CHUNK = 64

N_EXPERTS = 64
TOP_K = 8
N_GROUPS = 8
TOPK_GROUPS = 4
EXPERT_DIM = 512
ROUTED_SCALE = 2.5
MOE_TILE = 256

VMEM_LIMIT = 56 * 1024 * 1024


def _cparams(n_axes):
    return pltpu.CompilerParams(dimension_semantics=("arbitrary",) * n_axes,
                                vmem_limit_bytes=VMEM_LIMIT)


def _rms(x, g):
    ms = jnp.mean(x * x, axis=-1, keepdims=True)
    return x * lax.rsqrt(ms + RMS_EPS) * g


def _dot(a, b):
    return jnp.dot(a, b, preferred_element_type=F32)


def _dot_nt(a, b):
    return lax.dot_general(a, b, (((1,), (1,)), ((), ())), preferred_element_type=F32)


def _dot_tn(a, b):
    return lax.dot_general(a, b, (((0,), (0,)), ((), ())), preferred_element_type=F32)


def _split2(x):
    hi = x.astype(BF16)
    lo = (x - hi.astype(F32)).astype(BF16)
    return hi, lo


def _cat3_lhs(x):
    hi = x.astype(BF16).astype(F32)
    return jnp.concatenate([hi, x - hi, hi], axis=1).astype(BF16)


def _cat3_rhs(y):
    hi, lo = _split2(y)
    return jnp.concatenate([hi, hi, lo], axis=0)


def _norm_matmul_kernel(x_ref, g_ref, w_ref, o_ref, xn_ref):
    @pl.when(pl.program_id(1) == 0)
    def _():
        xn_ref[...] = _rms(x_ref[...], g_ref[...]).astype(BF16)

    o_ref[...] = _dot(xn_ref[...], w_ref[...]).astype(o_ref.dtype)


def _norm_matmul(x, g, w, tm, tn, out_dtype, name):
    m, k = x.shape
    n = w.shape[1]
    tm = min(tm, m)
    return pl.pallas_call(
        _norm_matmul_kernel,
        grid=(m // tm, n // tn),
        in_specs=[pl.BlockSpec((tm, k), lambda i, j: (i, 0)),
                  pl.BlockSpec((1, k), lambda i, j: (0, 0)),
                  pl.BlockSpec((k, tn), lambda i, j: (0, j))],
        out_specs=pl.BlockSpec((tm, tn), lambda i, j: (i, j)),
        out_shape=jax.ShapeDtypeStruct((m, n), out_dtype),
        scratch_shapes=[pltpu.VMEM((tm, k), BF16)],
        compiler_params=_cparams(2),
        name=name,
    )(x, g, w)


def _mla_prep_kernel(pm_ref, pos_ref, invf_ref, sgn_ref, gqa_ref, wuq_ref, gkva_ref, wukv_ref,
                     gqn_ref, gkn_ref, q_ref, k_ref, v_ref):
    pm = pm_ref[0]
    ang = pos_ref[0].astype(F32) * invf_ref[...]
    cos = jnp.cos(ang)
    sin = jnp.sin(ang) * sgn_ref[...]

    def rope(xr):
        half = MLA_ROPE // 2
        swapped = jnp.concatenate([xr[:, half:], xr[:, :half]], axis=1)
        return xr * cos + swapped * sin

    scale = MLA_QK ** -0.5 * 1.4426950408889634
    gqn = gqn_ref[...]
    gkn = gkn_ref[...]

    cq = _rms(pm[:, :MLA_Q_LORA], gqa_ref[...]).astype(BF16)
    q = _dot(cq, wuq_ref[...])
    ckv = _rms(pm[:, MLA_Q_LORA:MLA_Q_LORA + MLA_KV_LORA], gkva_ref[...]).astype(BF16)
    kv = _dot(ckv, wukv_ref[...])
    kpe = pm[:, MLA_Q_LORA + MLA_KV_LORA:MLA_COLS]
    kpe_ssq = jnp.sum(kpe * kpe, axis=-1, keepdims=True)
    kpe_rot = rope(kpe * gkn[:, MLA_NOPE:])

    for h in range(MLA_HEADS):
        qn = q[:, h * MLA_NOPE:(h + 1) * MLA_NOPE]
        qr = q[:, MLA_HEADS * MLA_NOPE + h * MLA_ROPE:MLA_HEADS * MLA_NOPE + (h + 1) * MLA_ROPE]
        ssq = jnp.sum(qn * qn, axis=-1, keepdims=True) + jnp.sum(qr * qr, axis=-1, keepdims=True)
        inv = lax.rsqrt(ssq * (1.0 / MLA_QK) + RMS_EPS) * scale
        q_ref[0, h, :, 0:MLA_NOPE] = (qn * inv * gqn[:, :MLA_NOPE]).astype(BF16)
        q_ref[0, h, :, MLA_NOPE:MLA_QK] = (rope(qr * gqn[:, MLA_NOPE:]) * inv).astype(BF16)

        kn = kv[:, h * 256:h * 256 + MLA_NOPE]
        ssq = jnp.sum(kn * kn, axis=-1, keepdims=True) + kpe_ssq
        inv = lax.rsqrt(ssq * (1.0 / MLA_QK) + RMS_EPS)
        k_ref[0, h, :, 0:MLA_NOPE] = (kn * inv * gkn[:, :MLA_NOPE]).astype(BF16)
        k_ref[0, h, :, MLA_NOPE:MLA_QK] = (kpe_rot * inv).astype(BF16)
        v_ref[0, h, :, 0:MLA_V] = kv[:, h * 256 + MLA_NOPE:(h + 1) * 256].astype(BF16)
        v_ref[0, h, :, MLA_V:] = jnp.ones((kv.shape[0], MLA_V), BF16)


def _mla_prep(pm, pos, invf, sgn, g_qa, w_uq, g_kva, w_ukv, g_qn, g_kn, tm):
    b, t, _ = pm.shape
    tm = min(tm, t)
    full = lambda a: pl.BlockSpec(a.shape, lambda i, j: (0,) * a.ndim)
    qk_shape = jax.ShapeDtypeStruct((b, MLA_HEADS, t, MLA_QK), BF16)
    return pl.pallas_call(
        _mla_prep_kernel,
        grid=(b, t // tm),
        in_specs=[pl.BlockSpec((1, tm, MLA_COLS_PAD), lambda i, j: (i, j, 0)),
                  pl.BlockSpec((1, tm, 1), lambda i, j: (i, j, 0)),
                  full(invf), full(sgn), full(g_qa), full(w_uq), full(g_kva), full(w_ukv),
                  full(g_qn), full(g_kn)],
        out_specs=[pl.BlockSpec((1, MLA_HEADS, tm, MLA_QK), lambda i, j: (i, 0, j, 0)),
                   pl.BlockSpec((1, MLA_HEADS, tm, MLA_QK), lambda i, j: (i, 0, j, 0)),
                   pl.BlockSpec((1, MLA_HEADS, tm, 2 * MLA_V), lambda i, j: (i, 0, j, 0))],
        out_shape=[qk_shape, qk_shape, jax.ShapeDtypeStruct((b, MLA_HEADS, t, 2 * MLA_V), BF16)],
        compiler_params=_cparams(2),
        name="mla_prep",
    )(pm, pos, invf, sgn, g_qa, w_uq, g_kva, w_ukv, g_qn, g_kn)


def _attn_kernel(q_ref, k_ref, v_ref, o_ref, m_ref, acc_ref, *, tq, tk, nh):
    qi = pl.program_id(2)
    m_ref[...] = jnp.full(m_ref.shape, -jnp.inf, F32)
    acc_ref[...] = jnp.zeros(acc_ref.shape, F32)
    heads = range(nh)
    qs = [q_ref[0, h] for h in heads]

    def step(j, masked):
        start = pl.multiple_of(j * tk, tk)
        ss = [_dot_nt(qs[h], k_ref[0, h, pl.ds(start, tk), :]) for h in heads]
        for h in heads:
            s = ss[h]
            if masked:
                row = qi * tq + lax.broadcasted_iota(jnp.int32, (tq, tk), 0)
                col = j * tk + lax.broadcasted_iota(jnp.int32, (tq, tk), 1)
                s = jnp.where(col <= row, s, -jnp.inf)
            m_old = m_ref[h]
            m_new = jnp.maximum(m_old, jnp.max(s, axis=-1, keepdims=True))
            alpha = jnp.exp2(m_old - m_new)
            p = jnp.exp2(s - jnp.concatenate([m_new] * (tk // 128), axis=1))
            pv = _dot(p.astype(BF16), v_ref[0, h, pl.ds(start, tk), :])
            acc_ref[h] = jnp.concatenate([alpha, alpha], axis=1) * acc_ref[h] + pv
            m_ref[h] = m_new

    n_full = (qi * tq) // tk

    def body(j, c):
        step(j, False)
        return c

    lax.fori_loop(0, n_full, body, 0)
    for d in range(tq // tk):
        step(n_full + d, True)
    for h in heads:
        acc = acc_ref[h]
        o_ref[0, :, h * MLA_V:(h + 1) * MLA_V] = (acc[:, :MLA_V] / acc[:, MLA_V:]).astype(o_ref.dtype)


def _attention(q, k, v, tq, tk, nh):
    b, h, t, _ = q.shape
    tq = min(tq, t)
    tk = min(tk, tq)
    return pl.pallas_call(
        functools.partial(_attn_kernel, tq=tq, tk=tk, nh=nh),
        grid=(b, h // nh, t // tq),
        in_specs=[pl.BlockSpec((1, nh, tq, MLA_QK), lambda i, j, n: (i, j, n, 0)),
                  pl.BlockSpec((1, nh, t, MLA_QK), lambda i, j, n: (i, j, 0, 0)),
                  pl.BlockSpec((1, nh, t, 2 * MLA_V), lambda i, j, n: (i, j, 0, 0))],
        out_specs=pl.BlockSpec((1, tq, nh * MLA_V), lambda i, j, n: (i, n, j)),
        out_shape=jax.ShapeDtypeStruct((b, t, h * MLA_V), BF16),
        scratch_shapes=[pltpu.VMEM((nh, tq, 128), F32), pltpu.VMEM((nh, tq, 2 * MLA_V), F32)],
        compiler_params=_cparams(3),
        name="mla_attention",
    )(q, k, v)


def _rw_prep_kernel(z_ref, zp_ref, mu_ref, w0_ref, ww2_ref, a0_ref, aw2_ref, gw2_ref, kk_ref,
                    ka_ref, bd_ref, r_o, w_o, k_o, v_o, kk_o, b_o, g_o):
    z = z_ref[0]
    prev = zp_ref[0][7:8, :]
    prev = jnp.where(pl.program_id(1) == 0, jnp.zeros_like(prev), prev)
    row = lax.broadcasted_iota(jnp.int32, z.shape, 0)
    zs = jnp.where(row == 0, prev, pltpu.roll(z, 1, axis=0))
    z = z + (zs - z) * mu_ref[...]

    c = RW_C
    r = z[:, 0:c]
    k = z[:, c:2 * c]
    v = z[:, 2 * c:3 * c]
    wl = z[:, 3 * c:3 * c + 128]
    al = z[:, 3 * c + 128:3 * c + 256]
    gl = z[:, 3 * c + 256:3 * c + 512]

    f = w0_ref[...] + _dot(jnp.tanh(wl).astype(BF16), ww2_ref[...])
    softplus_neg_f = jnp.maximum(-f, 0.0) + jnp.log(1.0 + jnp.exp(-jnp.abs(f)))
    w = -softplus_neg_f - 0.5
    w_o[0] = -jnp.exp(w)
    a = jax.nn.sigmoid(a0_ref[...] + _dot(al.astype(BF16), aw2_ref[...]))
    g_o[0] = _dot(jax.nn.sigmoid(gl).astype(BF16), gw2_ref[...])

    kk = k * kk_ref[...]
    hi, lo = _split2(kk * kk)
    ssq = _dot(hi, bd_ref[...]) + _dot(lo, bd_ref[...])
    kk = kk / jnp.maximum(jnp.sqrt(ssq), 1e-12)
    r_o[0] = r
    k_o[0] = k * (1.0 + (a - 1.0) * ka_ref[...])
    v_o[0] = v
    kk_o[0] = kk
    b_o[0] = kk * a


def _rw_prep(z, mu, w0, ww2, a0, aw2, gw2, k_k, k_a, bd, tm):
    b, t, cols = z.shape
    tm = min(tm, t)
    full = lambda a: pl.BlockSpec(a.shape, lambda i, j: (0,) * a.ndim)
    out = jax.ShapeDtypeStruct((b, t, RW_C), F32)
    ospec = pl.BlockSpec((1, tm, RW_C), lambda i, j: (i, j, 0))
    return pl.pallas_call(
        _rw_prep_kernel,
        grid=(b, t // tm),
        in_specs=[pl.BlockSpec((1, tm, cols), lambda i, j: (i, j, 0)),
                  pl.BlockSpec((1, 8, cols), lambda i, j: (i, jnp.maximum(j * (tm // 8) - 1, 0), 0)),
                  full(mu), full(w0), full(ww2), full(a0), full(aw2), full(gw2), full(k_k),
                  full(k_a), full(bd)],
        out_specs=[ospec] * 7,
        out_shape=[out] * 7,
        compiler_params=_cparams(2),
        name="rwkv_prep",
    )(z, z, mu, w0, ww2, a0, aw2, gw2, k_k, k_a, bd)


def _rw_chunk_kernel(r_ref, w_ref, k_ref, v_ref, kk_ref, b_ref, g_ref, rk_ref, lnw_ref, lnb_ref,
                     o_ref, s_ref):
    cs = RW_CHUNK
    hd = RW_HEAD

    @pl.when(pl.program_id(1) == 0)
    def _():
        s_ref[...] = jnp.zeros(s_ref.shape, F32)

    r = r_ref[0]
    w = w_ref[0]
    k = k_ref[0]
    v = v_ref[0]
    kk = kk_ref[0]
    b = b_ref[0]
    heads = range(RW_HEADS)

    row = lax.broadcasted_iota(jnp.int32, (cs, cs), 0)
    col = lax.broadcasted_iota(jnp.int32, (cs, cs), 1)
    strict = row > col
    eye = jnp.where(row == col, 1.0, 0.0).astype(F32)
    row3 = lax.broadcasted_iota(jnp.int32, (cs, 3 * cs), 0)
    col3 = lax.broadcasted_iota(jnp.int32, (cs, 3 * cs), 1)
    tri3 = jnp.where(row3 >= jnp.bitwise_and(col3, cs - 1), 1.0, 0.0).astype(BF16)
    row2 = lax.broadcasted_iota(jnp.int32, (cs, 2 * cs), 0)
    col2 = lax.broadcasted_iota(jnp.int32, (cs, 2 * cs), 1)
    mask2 = row2 >= jnp.bitwise_and(col2, cs - 1)

    w1 = w.astype(BF16)
    w2 = (w - w1.astype(F32)).astype(BF16)
    w3 = (w - w1.astype(F32) - w2.astype(F32)).astype(BF16)
    logp = _dot(tri3, jnp.concatenate([w1, w2, w3], axis=0))
    logp_end = logp[cs - 1:cs, :]
    p_tail = jnp.exp(logp_end - logp)
    p_end = jnp.exp(logp_end)
    p_inv = jnp.exp(-logp)

    a_t = -kk * jnp.exp(logp - w)
    r_t = r * jnp.exp(logp)
    b_t = b * p_inv
    k_t = k * p_inv
    b_h = b * p_tail
    k_h = k * p_tail

    sls = [slice(h * hd, (h + 1) * hd) for h in heads]
    lhs = [jnp.concatenate([a_t[:, sl], r_t[:, sl]], axis=0).astype(BF16) for sl in sls]
    rhs = [jnp.concatenate([b_t[:, sl], k_t[:, sl]], axis=0).astype(BF16) for sl in sls]
    gm = [_dot_nt(lhs[h], rhs[h]) for h in heads]
    s0 = [s_ref[h] for h in heads]
    ls = [_dot_nt(lhs[h], s0[h].astype(BF16)) for h in heads]
    a_ak = [jnp.where(strict, gm[h][:cs, cs:], 0.0).astype(BF16) for h in heads]
    vb = [v[:, sl].astype(BF16) for sl in sls]
    rhs_u = [ls[h][:cs] + _dot(a_ak[h], vb[h]) for h in heads]

    x = [jnp.where(strict, gm[h][:cs, :cs], 0.0) for h in heads]
    tm = [eye + x[h] for h in heads]
    for _ in range(5):
        xr = [_cat3_rhs(x[h]) for h in heads]
        x = [_dot(_cat3_lhs(x[h]), xr[h]) for h in heads]
        xr = [_cat3_rhs(x[h]) for h in heads]
        tm = [tm[h] + _dot(_cat3_lhs(tm[h]), xr[h]) for h in heads]

    u = [_dot(_cat3_lhs(tm[h]), _cat3_rhs(rhs_u[h])) for h in heads]
    uv = [jnp.concatenate([u[h].astype(BF16), vb[h]], axis=0) for h in heads]
    a_r = [jnp.where(mask2, gm[h][cs:, :], 0.0).astype(BF16) for h in heads]
    y = [ls[h][cs:] + _dot(a_r[h], uv[h]) for h in heads]
    for h in heads:
        bk = jnp.concatenate([b_h[:, sls[h]], k_h[:, sls[h]]], axis=0).astype(BF16)
        s_ref[h] = s0[h] * p_end[:, sls[h]] + _dot_tn(uv[h], bk)

    rk_all = r * k * rk_ref[...]
    g = g_ref[0]
    for h in heads:
        sl = sls[h]
        mean = jnp.mean(y[h], axis=-1, keepdims=True)
        yc = y[h] - mean
        var = jnp.mean(yc * yc, axis=-1, keepdims=True)
        yn = yc * lax.rsqrt(var + RW_GN_EPS) * lnw_ref[:, sl] + lnb_ref[:, sl]
        bonus = jnp.sum(rk_all[:, sl], axis=-1, keepdims=True) * v[:, sl]
        o_ref[0, :, sl] = ((yn + bonus) * g[:, sl]).astype(o_ref.dtype)


def _rw_chunk(r, w, k, v, kk, bb, g, r_k, ln_w, ln_b):
    b, t, c = r.shape
    cs = RW_CHUNK
    full = lambda a: pl.BlockSpec(a.shape, lambda i, j: (0,) * a.ndim)
    spec = pl.BlockSpec((1, cs, c), lambda i, j: (i, j, 0))
    return pl.pallas_call(
        _rw_chunk_kernel,
        grid=(b, t // cs),
        in_specs=[spec] * 7 + [full(r_k), full(ln_w), full(ln_b)],
        out_specs=spec,
        out_shape=jax.ShapeDtypeStruct((b, t, c), BF16),
        scratch_shapes=[pltpu.VMEM((RW_HEADS, RW_HEAD, RW_HEAD), F32)],
        compiler_params=_cparams(2),
        name="rwkv_chunk",
    )(r, w, k, v, kk, bb, g, r_k, ln_w, ln_b)


def _pack_pair(a, b):
    ua = lax.bitcast_convert_type(a.astype(BF16).astype(F32), jnp.uint32)
    ub = lax.bitcast_convert_type(b.astype(BF16).astype(F32), jnp.uint32)
    return lax.shift_right_logical(ua, jnp.uint32(16)) | (ub & jnp.uint32(0xFFFF0000))


def _unpack_pair(u):
    lo = lax.bitcast_convert_type(lax.shift_left(u, jnp.uint32(16)), F32)
    hi = lax.bitcast_convert_type(u & jnp.uint32(0xFFFF0000), F32)
    return lo, hi


def _out_proj_kernel(ya_ref, yr_ref, x_ref, w_ref, g_ref, wr_ref, h_ref, hn_ref, lg_ref):
    half = ya_ref.shape[1]
    acc = _dot(ya_ref[...], w_ref[0:half, :]) + _dot(yr_ref[...], w_ref[half:, :])
    h = x_ref[...] + acc
    h_ref[...] = h
    hn = _rms(h, g_ref[...])
    d2 = hn.shape[1] // 2
    hn_ref[...] = _pack_pair(hn[:, :d2], hn[:, d2:])
    lg_ref[...] = lax.dot_general(wr_ref[...], hn, (((1,), (1,)), ((), ())),
                                  precision=lax.Precision.HIGHEST, preferred_element_type=F32)


def _out_proj(ya, yr, x, w, g, wr_t, tm):
    m, d = x.shape
    tm = min(tm, m)
    full = lambda a: pl.BlockSpec(a.shape, lambda i: (0,) * a.ndim)
    return pl.pallas_call(
        _out_proj_kernel,
        grid=(m // tm,),
        in_specs=[pl.BlockSpec((tm, ya.shape[1]), lambda i: (i, 0)),
                  pl.BlockSpec((tm, yr.shape[1]), lambda i: (i, 0)),
                  pl.BlockSpec((tm, d), lambda i: (i, 0)),
                  full(w), full(g), full(wr_t)],
        out_specs=[pl.BlockSpec((tm, d), lambda i: (i, 0)),
                   pl.BlockSpec((tm, d // 2), lambda i: (i, 0)),
                   pl.BlockSpec((N_EXPERTS, tm), lambda i: (0, i))],
        out_shape=[jax.ShapeDtypeStruct((m, d), F32), jax.ShapeDtypeStruct((m, d // 2), jnp.uint32),
                   jax.ShapeDtypeStruct((N_EXPERTS, m), F32)],
        compiler_params=_cparams(1),
        name="out_proj_router_logits",
    )(ya, yr, x, w, g, wr_t)


def _router_kernel(lg_ref, bias_ref, e_ref, r_ref, w_ref, cnt_ref, carry_ref):
    tm = lg_ref.shape[1]
    gsz = N_EXPERTS // N_GROUPS

    @pl.when(pl.program_id(0) == 0)
    def _():
        carry_ref[...] = jnp.zeros(carry_ref.shape, F32)

    scores = jax.nn.sigmoid(lg_ref[...]).reshape(N_GROUPS, gsz, tm)
    biased = scores + bias_ref[...].reshape(N_GROUPS, gsz, 1)
    neg = -jnp.inf

    eidx = lax.broadcasted_iota(jnp.int32, biased.shape, 1)
    m1 = jnp.max(biased, axis=1, keepdims=True)
    first = jnp.min(jnp.where(biased == m1, eidx, gsz), axis=1, keepdims=True)
    m2 = jnp.max(jnp.where(eidx == first, neg, biased), axis=1, keepdims=True)
    gscore = m1 + m2

    gidx = lax.broadcasted_iota(jnp.int32, gscore.shape, 0)
    gsel = jnp.zeros(gscore.shape, jnp.bool_)
    work = gscore
    for _ in range(TOPK_GROUPS):
        best = jnp.max(work, axis=0, keepdims=True)
        pick = jnp.min(jnp.where(work == best, gidx, N_GROUPS), axis=0, keepdims=True)
        hit = gidx == pick
        gsel = jnp.logical_or(gsel, hit)
        work = jnp.where(hit, neg, work)

    flat = lax.broadcasted_iota(jnp.int32, biased.shape, 0) * gsz + eidx
    work = jnp.where(gsel, biased, neg)
    hits = []
    picks = []
    for _ in range(TOP_K):
        best = jnp.max(jnp.max(work, axis=1, keepdims=True), axis=0, keepdims=True)
        cand = jnp.where(work == best, flat, N_EXPERTS)
        pick = jnp.min(jnp.min(cand, axis=1, keepdims=True), axis=0, keepdims=True)
        hit = flat == pick
        hits.append(hit)
        picks.append(pick)
        work = jnp.where(hit, neg, work)

    def pick_value(hit, val):
        s = jnp.sum(jnp.sum(jnp.where(hit, val, 0.0), axis=1, keepdims=True), axis=0, keepdims=True)
        return s.reshape(1, tm)

    sel = jnp.zeros(biased.shape, F32)
    for hit in hits:
        sel = jnp.where(hit, 1.0, sel)
    sel2 = sel.reshape(N_EXPERTS, tm)
    r_i = lax.broadcasted_iota(jnp.int32, (tm, tm), 0)
    c_i = lax.broadcasted_iota(jnp.int32, (tm, tm), 1)
    upper = jnp.where(r_i < c_i, 1.0, 0.0).astype(BF16)
    rank = _dot(sel2.astype(BF16), upper) + carry_ref[...]
    carry = carry_ref[...] + jnp.sum(sel2, axis=1, keepdims=True)
    carry_ref[...] = carry
    cnt_ref[...] = carry
    rank3 = rank.reshape(N_GROUPS, gsz, tm)

    raw = [pick_value(hit, scores) for hit in hits]
    tot = raw[0]
    for x in raw[1:]:
        tot = tot + x
    wrow = lax.broadcasted_iota(jnp.int32, (N_EXPERTS, tm), 0)
    wt = jnp.zeros((N_EXPERTS, tm), F32)
    for j in range(TOP_K):
        e_ref[j:j + 1, :] = picks[j].reshape(1, tm)
        r_ref[j:j + 1, :] = pick_value(hits[j], rank3).astype(jnp.int32)
        wt = jnp.where(wrow == j, raw[j] / tot * ROUTED_SCALE, wt)
    w_ref[...] = wt.T


def _router(lg_t, bias, tm):
    e, m = lg_t.shape
    tm = min(tm, m)
    row = pl.BlockSpec((TOP_K, tm), lambda i: (0, i))
    return pl.pallas_call(
        _router_kernel,
        grid=(m // tm,),
        in_specs=[pl.BlockSpec((e, tm), lambda i: (0, i)),
                  pl.BlockSpec((e, 1), lambda i: (0, 0))],
        out_specs=[row, row, pl.BlockSpec((tm, e), lambda i: (i, 0)),
                   pl.BlockSpec((e, 1), lambda i: (0, 0))],
        out_shape=[jax.ShapeDtypeStruct((TOP_K, m), jnp.int32),
                   jax.ShapeDtypeStruct((TOP_K, m), jnp.int32),
                   jax.ShapeDtypeStruct((m, e), F32),
                   jax.ShapeDtypeStruct((e, 1), F32)],
        scratch_shapes=[pltpu.VMEM((e, 1), F32)],
        compiler_params=_cparams(1),
        name="router_topk",
    )(lg_t, bias)


def _dest_kernel(off_ref, e_ref, r_ref, d_ref):
    e = e_ref[...]
    acc = r_ref[...]
    for x in range(N_EXPERTS):
        acc = acc + jnp.where(e == x, off_ref[x], 0)
    d_ref[...] = acc


def _dest(off, eidx, rank, tm):
    k, m = eidx.shape
    tm = min(tm, m)
    spec = pl.BlockSpec((k, tm), lambda i: (0, i))
    return pl.pallas_call(
        _dest_kernel,
        grid=(m // tm,),
        in_specs=[pl.BlockSpec(memory_space=pltpu.SMEM), spec, spec],
        out_specs=spec,
        out_shape=jax.ShapeDtypeStruct((k, m), jnp.int32),
        compiler_params=_cparams(1),
        name="moe_dest_rows",
    )(off, eidx, rank)


def _scatter_kernel(zt_ref, d_ref, x_ref, z_ref, xs_hbm, sem, *, tm, tile):
    i = pl.program_id(0)

    @pl.when(i == 0)
    def _():
        def zero(e, c):
            start = pl.multiple_of(zt_ref[e] * tile, tile)
            pltpu.make_async_copy(z_ref, xs_hbm.at[pl.ds(start, tile)], sem).start()
            return c

        lax.fori_loop(0, N_EXPERTS, zero, 0)

        def zwait(e, c):
            pltpu.make_async_copy(z_ref, xs_hbm.at[pl.ds(0, tile)], sem).wait()
            return c

        lax.fori_loop(0, N_EXPERTS, zwait, 0)

    def body(n, c):
        for j in range(TOP_K):
            pltpu.make_async_copy(x_ref.at[pl.ds(n, 1)], xs_hbm.at[pl.ds(d_ref[j, n], 1)],
                                  sem).start(priority=j % 2)
        return c

    lax.fori_loop(0, tm, body, 0)

    def wait(n, c):
        for j in range(TOP_K):
            pltpu.make_async_copy(x_ref.at[pl.ds(0, 1)], xs_hbm.at[pl.ds(0, 1)], sem).wait()
        return c

    lax.fori_loop(0, tm, wait, 0)


def _scatter_rows(zero_tile, dest, xp, rows, tile, tm):
    m, w = xp.shape
    tm = min(tm, m)
    return pl.pallas_call(
        functools.partial(_scatter_kernel, tm=tm, tile=tile),
        grid_spec=pltpu.PrefetchScalarGridSpec(
            num_scalar_prefetch=1,
            grid=(m // tm,),
            in_specs=[pl.BlockSpec((TOP_K, tm), lambda i, zt: (0, i), memory_space=pltpu.SMEM),
                      pl.BlockSpec((tm, w), lambda i, zt: (i, 0)),
                      pl.BlockSpec((tile, w), lambda i, zt: (0, 0))],
            out_specs=pl.BlockSpec(memory_space=pl.ANY),
            scratch_shapes=[pltpu.SemaphoreType.DMA(())],
        ),
        out_shape=jax.ShapeDtypeStruct((rows, w), jnp.uint32),
        compiler_params=_cparams(1),
        name="moe_scatter_rows",
    )(zero_tile, dest, xp, jnp.zeros((tile, w), jnp.uint32))


def _expert_kernel(te_ref, nu_ref, x_ref, wg_ref, wu_ref, wd_ref, o_ref, wgb, wub, wdb):
    t = pl.program_id(0)
    tc = jnp.minimum(t, nu_ref[0] - 1)
    prev = jnp.maximum(tc - 1, 0)
    new_expert = jnp.logical_or(t == 0, te_ref[tc] != te_ref[prev])

    @pl.when(jnp.logical_and(new_expert, t < nu_ref[0]))
    def _():
        wgb[...] = wg_ref[0].astype(BF16)
        wub[...] = wu_ref[0].astype(BF16)
        wdb[...] = wd_ref[0].astype(BF16)

    @pl.when(t < nu_ref[0])
    def _():
        lo, hi = _unpack_pair(x_ref[...])
        lo = lo.astype(BF16)
        hi = hi.astype(BF16)
        d2 = lo.shape[1]
        gate = _dot(lo, wgb[0:d2, :]) + _dot(hi, wgb[d2:, :])
        up = _dot(lo, wub[0:d2, :]) + _dot(hi, wub[d2:, :])
        mid = (gate * jax.nn.sigmoid(gate) * up).astype(BF16)
        y = _dot(mid, wdb[...])
        o_ref[...] = _pack_pair(y[:, :d2], y[:, d2:])


def _experts(tile_expert, n_used, xs, wg, wu, wd, tile):
    rows, w = xs.shape
    d = 2 * w
    clamp = lambda t, te, nu: jnp.minimum(t, nu[0] - 1)
    return pl.pallas_call(
        _expert_kernel,
        grid_spec=pltpu.PrefetchScalarGridSpec(
            num_scalar_prefetch=2,
            grid=(rows // tile,),
            in_specs=[pl.BlockSpec((tile, w), lambda t, te, nu: (clamp(t, te, nu), 0)),
                      pl.BlockSpec((1, d, EXPERT_DIM), lambda t, te, nu: (te[clamp(t, te, nu)], 0, 0)),
                      pl.BlockSpec((1, d, EXPERT_DIM), lambda t, te, nu: (te[clamp(t, te, nu)], 0, 0)),
                      pl.BlockSpec((1, EXPERT_DIM, d), lambda t, te, nu: (te[clamp(t, te, nu)], 0, 0))],
            out_specs=pl.BlockSpec((tile, w), lambda t, te, nu: (clamp(t, te, nu), 0)),
            scratch_shapes=[pltpu.VMEM((d, EXPERT_DIM), BF16), pltpu.VMEM((d, EXPERT_DIM), BF16),
                            pltpu.VMEM((EXPERT_DIM, d), BF16)],
        ),
        out_shape=jax.ShapeDtypeStruct((rows, w), jnp.uint32),
        compiler_params=_cparams(1),
        name="moe_experts",
    )(tile_expert, n_used, xs, wg, wu, wd)


def _combine_kernel(d_ref, ys_hbm, w_ref, h_ref, xp_ref, sg_ref, su_ref, sd_ref, o_ref, buf, sem,
                    *, tm):
    def body(n, c):
        for j in range(TOP_K):
            pltpu.make_async_copy(ys_hbm.at[pl.ds(d_ref[j, n], 1)], buf.at[j, pl.ds(n, 1)],
                                  sem).start(priority=j % 2)
        return c

    lax.fori_loop(0, tm, body, 0)

    lo, hi = _unpack_pair(xp_ref[...])
    lo = lo.astype(BF16)
    hi = hi.astype(BF16)
    d2 = lo.shape[1]
    gate = _dot(lo, sg_ref[0:d2, :]) + _dot(hi, sg_ref[d2:, :])
    up = _dot(lo, su_ref[0:d2, :]) + _dot(hi, su_ref[d2:, :])
    mid = (gate * jax.nn.sigmoid(gate) * up).astype(BF16)
    shared = _dot(mid, sd_ref[...])
    acc_lo = h_ref[:, 0:d2] + shared[:, :d2]
    acc_hi = h_ref[:, d2:] + shared[:, d2:]

    def wait(n, c):
        for j in range(TOP_K):
            pltpu.make_async_copy(ys_hbm.at[pl.ds(0, 1)], buf.at[0, pl.ds(0, 1)], sem).wait()
        return c

    lax.fori_loop(0, tm, wait, 0)

    r_lo = jnp.zeros(acc_lo.shape, F32)
    r_hi = jnp.zeros(acc_hi.shape, F32)
    wts = w_ref[...]
    for j in range(TOP_K):
        ylo, yhi = _unpack_pair(buf[j])
        wj = wts[:, j:j + 1]
        r_lo = r_lo + wj * ylo
        r_hi = r_hi + wj * yhi
    o_ref[:, 0:d2] = acc_lo + r_lo
    o_ref[:, d2:] = acc_hi + r_hi


def _combine(dest, ys, wcol, h, xp, sg, su, sd, tm):
    m, d = h.shape
    w = d // 2
    tm = min(tm, m)
    full = lambda a: pl.BlockSpec(a.shape, lambda i: (0,) * a.ndim)
    return pl.pallas_call(
        functools.partial(_combine_kernel, tm=tm),
        grid=(m // tm,),
        in_specs=[pl.BlockSpec((TOP_K, tm), lambda i: (0, i), memory_space=pltpu.SMEM),
                  pl.BlockSpec(memory_space=pl.ANY),
                  pl.BlockSpec((tm, N_EXPERTS), lambda i: (i, 0)),
                  pl.BlockSpec((tm, d), lambda i: (i, 0)),
                  pl.BlockSpec((tm, w), lambda i: (i, 0)),
                  full(sg), full(su), full(sd)],
        out_specs=pl.BlockSpec((tm, d), lambda i: (i, 0)),
        out_shape=jax.ShapeDtypeStruct((m, d), F32),
        scratch_shapes=[pltpu.VMEM((TOP_K, tm, w), jnp.uint32), pltpu.SemaphoreType.DMA(())],
        compiler_params=_cparams(1),
        name="moe_combine_shared",
    )(dest, ys, wcol, h, xp, sg, su, sd)


def _moe(h1, hnp, lg_t, router_bias, w_gate, w_up, w_down, ws_gate, ws_up, ws_down):
    n = h1.shape[0]
    tile = MOE_TILE
    eidx, rank, wcol, cnt = _router(lg_t, router_bias.reshape(-1, 1), 512)
    counts = cnt[:, 0].astype(jnp.int32)
    tiles_per = (counts + tile - 1) // tile
    tile_end = jnp.cumsum(tiles_per)
    tile_start = tile_end - tiles_per
    n_tiles = (n * TOP_K) // tile + N_EXPERTS
    rows = n_tiles * tile
    tile_expert = jnp.minimum(jnp.sum(tile_end[None, :] <= jnp.arange(n_tiles)[:, None], axis=1),
                              N_EXPERTS - 1).astype(jnp.int32)
    n_used = tile_end[-1:].astype(jnp.int32)
    last_tile = jnp.clip(tile_end - 1, 0, n_tiles - 1).astype(jnp.int32)
    dest = _dest((tile_start * tile).astype(jnp.int32), eidx, rank, 2048)
    xs = _scatter_rows(last_tile, dest, hnp, rows, tile, 512)
    ys = _experts(tile_expert, n_used, xs, w_gate, w_up, w_down, tile)
    return _combine(dest, ys, wcol, h1, hnp, ws_gate.astype(BF16), ws_up.astype(BF16),
                    ws_down.astype(BF16), 256)


def _ple_kernel(h_ref, p_ref, gin_ref, wg_ref, bg_ref, wp_ref, gout_ref, o_ref):
    h = h_ref[...]
    gate = jax.nn.sigmoid(_dot(_rms(h, gin_ref[...]).astype(BF16), wg_ref[...]) + bg_ref[...])
    pp = _dot(p_ref[...].astype(BF16), wp_ref[...])
    o_ref[...] = h + _rms(pp * gate, gout_ref[...])


def _ple(h, p, g_in, w_g, b_g, w_p, g_out, tm):
    m, d = h.shape
    tm = min(tm, m)
    full = lambda a: pl.BlockSpec(a.shape, lambda i: (0,) * a.ndim)
    return pl.pallas_call(
        _ple_kernel,
        grid=(m // tm,),
        in_specs=[pl.BlockSpec((tm, d), lambda i: (i, 0)),
                  pl.BlockSpec((tm, p.shape[1]), lambda i: (i, 0)),
                  full(g_in), full(w_g), full(b_g), full(w_p), full(g_out)],
        out_specs=pl.BlockSpec((tm, d), lambda i: (i, 0)),
        out_shape=jax.ShapeDtypeStruct((m, d), F32),
        compiler_params=_cparams(1),
        name="ple",
    )(h, p, g_in, w_g, b_g, w_p, g_out)


def _pad_cols(a, width):
    return jnp.pad(a, ((0, 0), (0, width - a.shape[1])))


def _pad_rows(a, rows):
    return jnp.pad(a, ((0, rows - a.shape[0]), (0, 0)))


def _row(a):
    return a.reshape(1, -1)


def _rw_layout(a):
    c3 = 3 * RW_C
    parts = [a[..., :c3]]
    off = c3
    for width, padded in ((RW_DECAY_LORA, 128), (RW_A_LORA, 128), (RW_GATE_LORA, 256)):
        pad = [(0, 0)] * (a.ndim - 1) + [(0, padded - width)]
        parts.append(jnp.pad(a[..., off:off + width], pad))
        off += width
    return jnp.concatenate(parts, axis=-1)


def _mla_from_pm(pm, positions, g_qa, w_uq, g_kva, w_ukv, g_qn, g_kn):
    b, t, _ = pm.shape
    wq = w_uq.reshape(MLA_Q_LORA, MLA_HEADS, MLA_QK)
    wq = jnp.concatenate([wq[:, :, :MLA_NOPE].reshape(MLA_Q_LORA, -1),
                          wq[:, :, MLA_NOPE:].reshape(MLA_Q_LORA, -1)], axis=1).astype(BF16)
    half = MLA_ROPE // 2
    inv = ROPE_THETA ** (-jnp.arange(half, dtype=F32) / half)
    invf = _row(jnp.concatenate([inv, inv]))
    sgn = _row(jnp.concatenate([-jnp.ones(half, F32), jnp.ones(half, F32)]))
    q, k, v = _mla_prep(pm, positions.reshape(b, t, 1), invf, sgn, _row(g_qa), wq, _row(g_kva),
                        w_ukv.astype(BF16), _row(g_qn), _row(g_kn), 512)
    return _attention(q, k, v, 1024, 512, 2)


def _rwkv_from_z(z, mu, w0, w_w2, a0, a_w2, g_w2, k_k, k_a, r_k, ln_w, ln_b):
    head_of = jnp.arange(RW_C) // RW_HEAD
    bd = (head_of[:, None] == head_of[None, :]).astype(BF16)
    rr, ww, kx, vx, kkx, bx, gx = _rw_prep(
        z, _row(_rw_layout(mu)), _row(w0), _pad_rows(w_w2, 128).astype(BF16), _row(a0),
        _pad_rows(a_w2, 128).astype(BF16), _pad_rows(g_w2, 256).astype(BF16), _row(k_k), _row(k_a),
        bd, 256)
    return _rw_chunk(rr, ww, kx, vx, kkx, bx, gx, _row(r_k), _row(ln_w), _row(ln_b))


def _layer(h, p, positions, g_mix, w_in, mla_g_qa, mla_w_uq, mla_g_kva, mla_w_ukv, mla_g_qn,
           mla_g_kn, rw_mu, rw_w0, rw_w_w2, rw_a0, rw_a_w2, rw_g_w2, rw_k_k, rw_k_a, rw_r_k,
           rw_ln_w, rw_ln_b, w_out, g_ffn, w_router, router_bias, w_gate, w_up, w_down, ws_gate,
           ws_up, ws_down, g_ple_in, w_ple_gate, b_ple_gate, w_ple_proj, g_ple_out):
    b, t, d = h.shape
    n = b * t
    x2 = h.reshape(n, d)

    w_mla = _pad_cols(w_in[:, :MLA_COLS], MLA_COLS_PAD).astype(BF16)
    w_rw = _rw_layout(w_in[:, MLA_COLS:]).astype(BF16)
    pm = _norm_matmul(x2, _row(g_mix), w_mla, 512, MLA_COLS_PAD, F32, "in_proj_mla")
    z = _norm_matmul(x2, _row(g_mix), w_rw, 512, RW_COLS_PAD // 2, F32, "in_proj_rwkv")
    y_mla = _mla_from_pm(pm.reshape(b, t, -1), positions, mla_g_qa, mla_w_uq, mla_g_kva, mla_w_ukv,
                         mla_g_qn, mla_g_kn)
    y_rw = _rwkv_from_z(z.reshape(b, t, -1), rw_mu, rw_w0, rw_w_w2, rw_a0, rw_a_w2, rw_g_w2, rw_k_k,
                        rw_k_a, rw_r_k, rw_ln_w, rw_ln_b)

    h1, hnp, lg_t = _out_proj(y_mla.reshape(n, -1), y_rw.reshape(n, -1), x2, w_out.astype(BF16),
                              _row(g_ffn), w_router.T, 512)
    h2 = _moe(h1, hnp, lg_t, router_bias, w_gate, w_up, w_down, ws_gate, ws_up, ws_down)

    h3 = _ple(h2, p.reshape(n, -1), _row(g_ple_in), w_ple_gate.astype(BF16), _row(b_ple_gate),
              w_ple_proj.astype(BF16), _row(g_ple_out), 512)
    return h3.reshape(b, t, d)


def kernel(x, p, positions, g_mix, w_in, mla_g_qa, mla_w_uq, mla_g_kva, mla_w_ukv, mla_g_qn, mla_g_kn, rw_mu, rw_w0, rw_w_w2, rw_a0, rw_a_w2, rw_g_w2, rw_k_k, rw_k_a, rw_r_k, rw_ln_w, rw_ln_b, w_out, g_ffn, w_router, router_bias, w_gate, w_up, w_down, ws_gate, ws_up, ws_down, g_ple_in, w_ple_gate, b_ple_gate, w_ple_proj, g_ple_out):
    params = (g_mix, w_in, mla_g_qa, mla_w_uq, mla_g_kva, mla_w_ukv, mla_g_qn, mla_g_kn, rw_mu,
              rw_w0, rw_w_w2, rw_a0, rw_a_w2, rw_g_w2, rw_k_k, rw_k_a, rw_r_k, rw_ln_w, rw_ln_b,
              w_out, g_ffn, w_router, router_bias, w_gate, w_up, w_down, ws_gate, ws_up, ws_down,
              g_ple_in, w_ple_gate, b_ple_gate, w_ple_proj, g_ple_out)
    h = x
    for i in range(g_mix.shape[0]):
        h = _layer(h, p[i], positions, *[a[i] for a in params])
    return h
```

```python
import functools

import jax
import jax.numpy as jnp
from jax import lax
from jax.experimental import pallas as pl
from jax.experimental.pallas import tpu as pltpu

F32 = jnp.float32
BF16 = jnp.bfloat16

D_MODEL = 2048
PLE_DIM = 256
RMS_EPS = 1e-6

MLA_HEADS = 8
MLA_NOPE = 128
MLA_ROPE = 64
MLA_QK = MLA_NOPE + MLA_ROPE
MLA_V = 128
MLA_Q_LORA = 512
MLA_KV_LORA = 256
ROPE_THETA = 10000.0
MLA_COLS = MLA_Q_LORA + MLA_KV_LORA + MLA_ROPE
MLA_COLS_PAD = 896

RW_HEADS = 16
RW_HEAD = 64
RW_C = RW_HEADS * RW_HEAD
RW_DECAY_LORA = 64
RW_A_LORA = 64
RW_GATE_LORA = 160
RW_GN_EPS = 64e-5
RW_LORA_PAD = 512
RW_COLS_PAD = 3 * RW_C + RW_LORA_PAD
RW_CHUNK = 64

N_EXPERTS = 64
TOP_K = 8
N_GROUPS = 8
TOPK_GROUPS = 4
EXPERT_DIM = 512
ROUTED_SCALE = 2.5
MOE_TILE = 256
ROW_SUB = D_MODEL // 2 // 128

VMEM_LIMIT = 56 * 1024 * 1024


def _cparams(n_axes):
    return pltpu.CompilerParams(dimension_semantics=("arbitrary",) * n_axes,
                                vmem_limit_bytes=VMEM_LIMIT)


def _rms(x, g):
    ms = jnp.mean(x * x, axis=-1, keepdims=True)
    return x * lax.rsqrt(ms + RMS_EPS) * g


def _dot(a, b):
    return jnp.dot(a, b, preferred_element_type=F32)


def _dot_nt(a, b):
    return lax.dot_general(a, b, (((1,), (1,)), ((), ())), preferred_element_type=F32)


def _dot_tn(a, b):
    return lax.dot_general(a, b, (((0,), (0,)), ((), ())), preferred_element_type=F32)


def _split2(x):
    hi = x.astype(BF16)
    lo = (x - hi.astype(F32)).astype(BF16)
    return hi, lo


def _norm_matmul_kernel(x_ref, g_ref, w_ref, o_ref, xn_ref):
    @pl.when(pl.program_id(1) == 0)
    def _():
        xn_ref[...] = _rms(x_ref[...], g_ref[...]).astype(BF16)

    o_ref[...] = _dot(xn_ref[...], w_ref[...]).astype(o_ref.dtype)


def _norm_matmul(x, g, w, tm, tn, out_dtype, name):
    m, k = x.shape
    n = w.shape[1]
    tm = min(tm, m)
    return pl.pallas_call(
        _norm_matmul_kernel,
        grid=(m // tm, n // tn),
        in_specs=[pl.BlockSpec((tm, k), lambda i, j: (i, 0)),
                  pl.BlockSpec((1, k), lambda i, j: (0, 0)),
                  pl.BlockSpec((k, tn), lambda i, j: (0, j))],
        out_specs=pl.BlockSpec((tm, tn), lambda i, j: (i, j)),
        out_shape=jax.ShapeDtypeStruct((m, n), out_dtype),
        scratch_shapes=[pltpu.VMEM((tm, k), BF16)],
        compiler_params=_cparams(2),
        name=name,
    )(x, g, w)


def _mla_prep_kernel(pm_ref, pos_ref, invf_ref, sgn_ref, gqa_ref, wuq_ref, gkva_ref, wukv_ref,
                     gqn_ref, gkn_ref, q_ref, k_ref, v_ref):
    pm = pm_ref[0]
    ang = pos_ref[0].astype(F32) * invf_ref[...]
    cos = jnp.cos(ang)
    sin = jnp.sin(ang) * sgn_ref[...]

    def rope(xr):
        half = MLA_ROPE // 2
        swapped = jnp.concatenate([xr[:, half:], xr[:, :half]], axis=1)
        return xr * cos + swapped * sin

    scale = MLA_QK ** -0.5 * 1.4426950408889634
    gqn = gqn_ref[...]
    gkn = gkn_ref[...]

    cq = _rms(pm[:, :MLA_Q_LORA], gqa_ref[...]).astype(BF16)
    q = _dot(cq, wuq_ref[...])
    ckv = _rms(pm[:, MLA_Q_LORA:MLA_Q_LORA + MLA_KV_LORA], gkva_ref[...]).astype(BF16)
    kv = _dot(ckv, wukv_ref[...])
    kpe = pm[:, MLA_Q_LORA + MLA_KV_LORA:MLA_COLS]
    kpe_ssq = jnp.sum(kpe * kpe, axis=-1, keepdims=True)
    kpe_rot = rope(kpe * gkn[:, MLA_NOPE:])

    for h in range(MLA_HEADS):
        qn = q[:, h * MLA_NOPE:(h + 1) * MLA_NOPE]
        qr = q[:, MLA_HEADS * MLA_NOPE + h * MLA_ROPE:MLA_HEADS * MLA_NOPE + (h + 1) * MLA_ROPE]
        ssq = jnp.sum(qn * qn, axis=-1, keepdims=True) + jnp.sum(qr * qr, axis=-1, keepdims=True)
        inv = lax.rsqrt(ssq * (1.0 / MLA_QK) + RMS_EPS) * scale
        q_ref[0, h, :, 0:MLA_NOPE] = (qn * inv * gqn[:, :MLA_NOPE]).astype(BF16)
        q_ref[0, h, :, MLA_NOPE:MLA_QK] = (rope(qr * gqn[:, MLA_NOPE:]) * inv).astype(BF16)

        kn = kv[:, h * 256:h * 256 + MLA_NOPE]
        ssq = jnp.sum(kn * kn, axis=-1, keepdims=True) + kpe_ssq
        inv = lax.rsqrt(ssq * (1.0 / MLA_QK) + RMS_EPS)
        k_ref[0, h, :, 0:MLA_NOPE] = (kn * inv * gkn[:, :MLA_NOPE]).astype(BF16)
        k_ref[0, h, :, MLA_NOPE:MLA_QK] = (kpe_rot * inv).astype(BF16)
        v_ref[0, h, :, 0:MLA_V] = kv[:, h * 256 + MLA_NOPE:(h + 1) * 256].astype(BF16)
        v_ref[0, h, :, MLA_V:] = jnp.ones((kv.shape[0], MLA_V), BF16)


def _mla_prep(pm, pos, invf, sgn, g_qa, w_uq, g_kva, w_ukv, g_qn, g_kn, tm):
    b, t, _ = pm.shape
    tm = min(tm, t)
    full = lambda a: pl.BlockSpec(a.shape, lambda i, j: (0,) * a.ndim)
    qk_shape = jax.ShapeDtypeStruct((b, MLA_HEADS, t, MLA_QK), BF16)
    return pl.pallas_call(
        _mla_prep_kernel,
        grid=(b, t // tm),
        in_specs=[pl.BlockSpec((1, tm, MLA_COLS_PAD), lambda i, j: (i, j, 0)),
                  pl.BlockSpec((1, tm, 1), lambda i, j: (i, j, 0)),
                  full(invf), full(sgn), full(g_qa), full(w_uq), full(g_kva), full(w_ukv),
                  full(g_qn), full(g_kn)],
        out_specs=[pl.BlockSpec((1, MLA_HEADS, tm, MLA_QK), lambda i, j: (i, 0, j, 0)),
                   pl.BlockSpec((1, MLA_HEADS, tm, MLA_QK), lambda i, j: (i, 0, j, 0)),
                   pl.BlockSpec((1, MLA_HEADS, tm, 2 * MLA_V), lambda i, j: (i, 0, j, 0))],
        out_shape=[qk_shape, qk_shape, jax.ShapeDtypeStruct((b, MLA_HEADS, t, 2 * MLA_V), BF16)],
        compiler_params=_cparams(2),
        name="mla_prep",
    )(pm, pos, invf, sgn, g_qa, w_uq, g_kva, w_ukv, g_qn, g_kn)


def _attn_kernel(q_ref, k_ref, v_ref, o_ref, m_ref, acc_ref, *, tq, tk, nh):
    qi = pl.program_id(2)
    m_ref[...] = jnp.full(m_ref.shape, -jnp.inf, F32)
    acc_ref[...] = jnp.zeros(acc_ref.shape, F32)
    heads = range(nh)
    qs = [q_ref[0, h] for h in heads]

    def step(j, masked):
        start = pl.multiple_of(j * tk, tk)
        ss = [_dot_nt(qs[h], k_ref[0, h, pl.ds(start, tk), :]) for h in heads]
        for h in heads:
            s = ss[h]
            if masked:
                row = qi * tq + lax.broadcasted_iota(jnp.int32, (tq, tk), 0)
                col = j * tk + lax.broadcasted_iota(jnp.int32, (tq, tk), 1)
                s = jnp.where(col <= row, s, -jnp.inf)
            m_old = m_ref[h]
            m_new = jnp.maximum(m_old, jnp.max(s, axis=-1, keepdims=True))
            alpha = jnp.exp2(m_old - m_new)
            p = jnp.exp2(s - jnp.concatenate([m_new] * (tk // 128), axis=1))
            pv = _dot(p.astype(BF16), v_ref[0, h, pl.ds(start, tk), :])
            acc_ref[h] = jnp.concatenate([alpha, alpha], axis=1) * acc_ref[h] + pv
            m_ref[h] = m_new

    n_full = (qi * tq) // tk

    def body(j, c):
        step(j, False)
        return c

    lax.fori_loop(0, n_full, body, 0)
    for d in range(tq // tk):
        step(n_full + d, True)
    for h in heads:
        acc = acc_ref[h]
        o_ref[0, :, h * MLA_V:(h + 1) * MLA_V] = (acc[:, :MLA_V] / acc[:, MLA_V:]).astype(o_ref.dtype)


def _attention(q, k, v, tq, tk, nh):
    b, h, t, _ = q.shape
    tq = min(tq, t)
    tk = min(tk, tq)
    return pl.pallas_call(
        functools.partial(_attn_kernel, tq=tq, tk=tk, nh=nh),
        grid=(b, h // nh, t // tq),
        in_specs=[pl.BlockSpec((1, nh, tq, MLA_QK), lambda i, j, n: (i, j, n, 0)),
                  pl.BlockSpec((1, nh, t, MLA_QK), lambda i, j, n: (i, j, 0, 0)),
                  pl.BlockSpec((1, nh, t, 2 * MLA_V), lambda i, j, n: (i, j, 0, 0))],
        out_specs=pl.BlockSpec((1, tq, nh * MLA_V), lambda i, j, n: (i, n, j)),
        out_shape=jax.ShapeDtypeStruct((b, t, h * MLA_V), BF16),
        scratch_shapes=[pltpu.VMEM((nh, tq, 128), F32), pltpu.VMEM((nh, tq, 2 * MLA_V), F32)],
        compiler_params=_cparams(3),
        name="mla_attention",
    )(q, k, v)


def _rw_prep_kernel(z_ref, zp_ref, mu_ref, w0_ref, ww2_ref, a0_ref, aw2_ref, gw2_ref, kk_ref,
                    ka_ref, bd_ref, r_o, w_o, k_o, v_o, kk_o, b_o, g_o):
    z = z_ref[0]
    prev = zp_ref[0][7:8, :]
    prev = jnp.where(pl.program_id(1) == 0, jnp.zeros_like(prev), prev)
    row = lax.broadcasted_iota(jnp.int32, z.shape, 0)
    zs = jnp.where(row == 0, prev, pltpu.roll(z, 1, axis=0))
    z = z + (zs - z) * mu_ref[...]

    c = RW_C
    r = z[:, 0:c]
    k = z[:, c:2 * c]
    v = z[:, 2 * c:3 * c]
    wl = z[:, 3 * c:3 * c + 128]
    al = z[:, 3 * c + 128:3 * c + 256]
    gl = z[:, 3 * c + 256:3 * c + 512]

    f = w0_ref[...] + _dot(jnp.tanh(wl).astype(BF16), ww2_ref[...])
    softplus_neg_f = jnp.maximum(-f, 0.0) + jnp.log(1.0 + jnp.exp(-jnp.abs(f)))
    w = -softplus_neg_f - 0.5
    w_o[0] = -jnp.exp(w)
    a = jax.nn.sigmoid(a0_ref[...] + _dot(al.astype(BF16), aw2_ref[...]))
    g_o[0] = _dot(jax.nn.sigmoid(gl).astype(BF16), gw2_ref[...]).astype(g_o.dtype)

    kk = k * kk_ref[...]
    hi, lo = _split2(kk * kk)
    ssq = _dot(hi, bd_ref[...]) + _dot(lo, bd_ref[...])
    kk = kk / jnp.maximum(jnp.sqrt(ssq), 1e-12)
    r_o[0] = r.astype(r_o.dtype)
    k_o[0] = (k * (1.0 + (a - 1.0) * ka_ref[...])).astype(k_o.dtype)
    v_o[0] = v.astype(v_o.dtype)
    kk_o[0] = kk.astype(kk_o.dtype)
    b_o[0] = (kk * a).astype(b_o.dtype)


def _rw_prep(z, mu, w0, ww2, a0, aw2, gw2, k_k, k_a, bd, tm):
    b, t, cols = z.shape
    tm = min(tm, t)
    full = lambda a: pl.BlockSpec(a.shape, lambda i, j: (0,) * a.ndim)
    out = [jax.ShapeDtypeStruct((b, t, RW_C), F32 if i == 1 else BF16) for i in range(7)]
    ospec = pl.BlockSpec((1, tm, RW_C), lambda i, j: (i, j, 0))
    return pl.pallas_call(
        _rw_prep_kernel,
        grid=(b, t // tm),
        in_specs=[pl.BlockSpec((1, tm, cols), lambda i, j: (i, j, 0)),
                  pl.BlockSpec((1, 8, cols), lambda i, j: (i, jnp.maximum(j * (tm // 8) - 1, 0), 0)),
                  full(mu), full(w0), full(ww2), full(a0), full(aw2), full(gw2), full(k_k),
                  full(k_a), full(bd)],
        out_specs=[ospec] * 7,
        out_shape=out,
        compiler_params=_cparams(2),
        name="rwkv_prep",
    )(z, z, mu, w0, ww2, a0, aw2, gw2, k_k, k_a, bd)


def _rw_chunk_kernel(r_ref, w_ref, k_ref, v_ref, kk_ref, b_ref, g_ref, rk_ref, lnw_ref, lnb_ref,
                     o_ref, s_ref):
    cs = RW_CHUNK
    hd = RW_HEAD

    @pl.when(pl.program_id(1) == 0)
    def _():
        s_ref[...] = jnp.zeros(s_ref.shape, F32)

    r = r_ref[0].astype(F32)
    w = w_ref[0]
    k = k_ref[0].astype(F32)
    v = v_ref[0].astype(F32)
    kk = kk_ref[0].astype(F32)
    b = b_ref[0].astype(F32)
    pairs = range(RW_HEADS // 2)
    pw = 2 * hd

    row = lax.broadcasted_iota(jnp.int32, (cs, pw), 0)
    col = lax.broadcasted_iota(jnp.int32, (cs, pw), 1)
    colh = jnp.bitwise_and(col, hd - 1)
    lo = col < hd
    strict = row > colh
    eye = jnp.where(row == colh, 1.0, 0.0).astype(F32)
    row2 = lax.broadcasted_iota(jnp.int32, (cs, 2 * pw), 0)
    col2 = lax.broadcasted_iota(jnp.int32, (cs, 2 * pw), 1)
    incl2 = row2 >= jnp.bitwise_and(col2, hd - 1)
    row3 = lax.broadcasted_iota(jnp.int32, (cs, 3 * cs), 0)
    col3 = lax.broadcasted_iota(jnp.int32, (cs, 3 * cs), 1)
    tri3 = jnp.where(row3 >= jnp.bitwise_and(col3, cs - 1), 1.0, 0.0).astype(BF16)

    def bdiag(x):
        zero = jnp.zeros_like(x)
        return jnp.concatenate([jnp.where(lo, x, zero), jnp.where(lo, zero, x)], axis=0)

    def cat3_lhs(x):
        hi = x.astype(BF16)
        return jnp.concatenate([hi, (x - hi.astype(F32)).astype(BF16), hi], axis=1)

    def cat3_rhs(x):
        hi, lw = _split2(x)
        bh = bdiag(hi)
        return jnp.concatenate([bh, bh, bdiag(lw)], axis=0)

    w1 = w.astype(BF16)
    w2 = (w - w1.astype(F32)).astype(BF16)
    w3 = (w - w1.astype(F32) - w2.astype(F32)).astype(BF16)
    logp = _dot(tri3, jnp.concatenate([w1, w2, w3], axis=0))
    logp_end = logp[cs - 1:cs, :]
    p_tail = jnp.exp(logp_end - logp)
    p_end = jnp.exp(logp_end)
    p_inv = jnp.exp(-logp)

    a_t = -kk * jnp.exp(logp - w)
    r_t = r * jnp.exp(logp)
    b_t = b * p_inv
    k_t = k * p_inv
    b_h = b * p_tail
    k_h = k * p_tail

    sls = [slice(p * pw, (p + 1) * pw) for p in pairs]
    lhs = [jnp.concatenate([a_t[:, sl], r_t[:, sl]], axis=0).astype(BF16) for sl in sls]
    rhs = [jnp.concatenate([bdiag(b_t[:, sl].astype(BF16)), bdiag(k_t[:, sl].astype(BF16))], axis=0)
           for sl in sls]
    gm = [_dot_nt(lhs[p], rhs[p]) for p in pairs]
    s0 = [s_ref[p] for p in pairs]
    ls = [_dot_nt(lhs[p], bdiag(s0[p].astype(BF16))) for p in pairs]
    vb = [v[:, sl].astype(BF16) for sl in sls]
    bv = [bdiag(vb[p]) for p in pairs]
    a_ak = [jnp.where(strict, gm[p][:cs, pw:], 0.0).astype(BF16) for p in pairs]
    rhs_u = [ls[p][:cs] + _dot(a_ak[p], bv[p]) for p in pairs]

    x = [jnp.where(strict, gm[p][:cs, :pw], 0.0) for p in pairs]
    tm = [eye + x[p] for p in pairs]
    x = [_dot(cat3_lhs(x[p]), cat3_rhs(x[p])) for p in pairs]
    for _ in range(4):
        both = [_dot(cat3_lhs(jnp.concatenate([x[p], tm[p]], axis=0)), cat3_rhs(x[p])) for p in pairs]
        x = [both[p][:cs] for p in pairs]
        tm = [tm[p] + both[p][cs:] for p in pairs]
    tm = [tm[p] + _dot(cat3_lhs(tm[p]), cat3_rhs(x[p])) for p in pairs]

    u = [_dot(cat3_lhs(tm[p]), cat3_rhs(rhs_u[p])) for p in pairs]
    ub = [u[p].astype(BF16) for p in pairs]
    a_r = [jnp.where(incl2, gm[p][cs:, :], 0.0).astype(BF16) for p in pairs]
    y = [ls[p][cs:] + _dot(a_r[p], jnp.concatenate([bdiag(ub[p]), bv[p]], axis=0)) for p in pairs]
    for p in pairs:
        uv = jnp.concatenate([ub[p], vb[p]], axis=0)
        bk = jnp.concatenate([b_h[:, sls[p]], k_h[:, sls[p]]], axis=0).astype(BF16)
        full = _dot_tn(uv, bk)
        s_ref[p] = s0[p] * p_end[:, sls[p]] + jnp.where(lo, full[:hd], full[hd:])

    def head_sum(t):
        s_lo = jnp.sum(jnp.where(lo, t, 0.0), axis=-1, keepdims=True)
        s_all = jnp.sum(t, axis=-1, keepdims=True)
        return jnp.where(lo, s_lo, s_all - s_lo)

    rk_all = r * k * rk_ref[...]
    g = g_ref[0].astype(F32)
    for p in pairs:
        sl = sls[p]
        yc = y[p] - head_sum(y[p]) * (1.0 / hd)
        var = head_sum(yc * yc) * (1.0 / hd)
        yn = yc * lax.rsqrt(var + RW_GN_EPS) * lnw_ref[:, sl] + lnb_ref[:, sl]
        bonus = head_sum(rk_all[:, sl]) * v[:, sl]
        o_ref[0, :, sl] = ((yn + bonus) * g[:, sl]).astype(o_ref.dtype)


def _rw_chunk(r, w, k, v, kk, bb, g, r_k, ln_w, ln_b):
    b, t, c = r.shape
    cs = RW_CHUNK
    full = lambda a: pl.BlockSpec(a.shape, lambda i, j: (0,) * a.ndim)
    spec = pl.BlockSpec((1, cs, c), lambda i, j: (i, j, 0))
    return pl.pallas_call(
        _rw_chunk_kernel,
        grid=(b, t // cs),
        in_specs=[spec] * 7 + [full(r_k), full(ln_w), full(ln_b)],
        out_specs=spec,
        out_shape=jax.ShapeDtypeStruct((b, t, c), BF16),
        scratch_shapes=[pltpu.VMEM((RW_HEADS // 2, RW_HEAD, 2 * RW_HEAD), F32)],
        compiler_params=_cparams(2),
        name="rwkv_chunk",
    )(r, w, k, v, kk, bb, g, r_k, ln_w, ln_b)


def _pack_pair(a, b):
    ua = lax.bitcast_convert_type(a.astype(BF16).astype(F32), jnp.uint32)
    ub = lax.bitcast_convert_type(b.astype(BF16).astype(F32), jnp.uint32)
    return lax.shift_right_logical(ua, jnp.uint32(16)) | (ub & jnp.uint32(0xFFFF0000))


def _unpack_pair(u):
    lo = lax.bitcast_convert_type(lax.shift_left(u, jnp.uint32(16)), F32)
    hi = lax.bitcast_convert_type(u & jnp.uint32(0xFFFF0000), F32)
    return lo, hi


def _store_rows(ref, packed):
    m = packed.shape[0]
    for s in range(ROW_SUB):
        ref[pl.ds(s, m, stride=ROW_SUB), :] = packed[:, s * 128:(s + 1) * 128]


def _load_rows(ref):
    m = ref.shape[-2] // ROW_SUB
    los, his = [], []
    for s in range(ROW_SUB):
        lo, hi = _unpack_pair(ref[pl.ds(s, m, stride=ROW_SUB), :])
        los.append(lo.astype(BF16))
        his.append(hi.astype(BF16))
    return jnp.concatenate(los, axis=1), jnp.concatenate(his, axis=1)


def _out_proj_kernel(ya_ref, yr_ref, x_ref, w_ref, g_ref, wr_ref, h_ref, hn_ref, lg_ref):
    half = ya_ref.shape[1]
    acc = _dot(ya_ref[...], w_ref[0:half, :]) + _dot(yr_ref[...], w_ref[half:, :])
    h = x_ref[...] + acc
    h_ref[...] = h
    hn = _rms(h, g_ref[...])
    d2 = hn.shape[1] // 2
    _store_rows(hn_ref, _pack_pair(hn[:, :d2], hn[:, d2:]))
    lg_ref[...] = lax.dot_general(wr_ref[...], hn, (((1,), (1,)), ((), ())),
                                  precision=lax.Precision.HIGHEST, preferred_element_type=F32)


def _out_proj(ya, yr, x, w, g, wr_t, tm):
    m, d = x.shape
    tm = min(tm, m)
    full = lambda a: pl.BlockSpec(a.shape, lambda i: (0,) * a.ndim)
    return pl.pallas_call(
        _out_proj_kernel,
        grid=(m // tm,),
        in_specs=[pl.BlockSpec((tm, ya.shape[1]), lambda i: (i, 0)),
                  pl.BlockSpec((tm, yr.shape[1]), lambda i: (i, 0)),
                  pl.BlockSpec((tm, d), lambda i: (i, 0)),
                  full(w), full(g), full(wr_t)],
        out_specs=[pl.BlockSpec((tm, d), lambda i: (i, 0)),
                   pl.BlockSpec((tm * ROW_SUB, 128), lambda i: (i, 0)),
                   pl.BlockSpec((N_EXPERTS, tm), lambda i: (0, i))],
        out_shape=[jax.ShapeDtypeStruct((m, d), F32), jax.ShapeDtypeStruct((m * ROW_SUB, 128), jnp.uint32),
                   jax.ShapeDtypeStruct((N_EXPERTS, m), F32)],
        compiler_params=_cparams(1),
        name="out_proj_router_logits",
    )(ya, yr, x, w, g, wr_t)


def _router_kernel(lg_ref, bias_ref, e_ref, r_ref, w_ref, cnt_ref, carry_ref):
    tm = lg_ref.shape[1]
    gsz = N_EXPERTS // N_GROUPS

    @pl.when(pl.program_id(0) == 0)
    def _():
        carry_ref[...] = jnp.zeros(carry_ref.shape, F32)

    scores = jax.nn.sigmoid(lg_ref[...]).reshape(N_GROUPS, gsz, tm)
    biased = scores + bias_ref[...].reshape(N_GROUPS, gsz, 1)
    neg = -jnp.inf

    eidx = lax.broadcasted_iota(jnp.int32, biased.shape, 1)
    m1 = jnp.max(biased, axis=1, keepdims=True)
    first = jnp.min(jnp.where(biased == m1, eidx, gsz), axis=1, keepdims=True)
    m2 = jnp.max(jnp.where(eidx == first, neg, biased), axis=1, keepdims=True)
    gscore = m1 + m2

    gidx = lax.broadcasted_iota(jnp.int32, gscore.shape, 0)
    gsel = jnp.zeros(gscore.shape, jnp.bool_)
    work = gscore
    for _ in range(TOPK_GROUPS):
        best = jnp.max(work, axis=0, keepdims=True)
        pick = jnp.min(jnp.where(work == best, gidx, N_GROUPS), axis=0, keepdims=True)
        hit = gidx == pick
        gsel = jnp.logical_or(gsel, hit)
        work = jnp.where(hit, neg, work)

    flat = lax.broadcasted_iota(jnp.int32, biased.shape, 0) * gsz + eidx
    work = jnp.where(gsel, biased, neg)
    hits = []
    picks = []
    for _ in range(TOP_K):
        best = jnp.max(jnp.max(work, axis=1, keepdims=True), axis=0, keepdims=True)
        cand = jnp.where(work == best, flat, N_EXPERTS)
        pick = jnp.min(jnp.min(cand, axis=1, keepdims=True), axis=0, keepdims=True)
        hit = flat == pick
        hits.append(hit)
        picks.append(pick)
        work = jnp.where(hit, neg, work)

    def pick_value(hit, val):
        s = jnp.sum(jnp.sum(jnp.where(hit, val, 0.0), axis=1, keepdims=True), axis=0, keepdims=True)
        return s.reshape(1, tm)

    sel = jnp.zeros(biased.shape, F32)
    for hit in hits:
        sel = jnp.where(hit, 1.0, sel)
    sel2 = sel.reshape(N_EXPERTS, tm)
    r_i = lax.broadcasted_iota(jnp.int32, (tm, tm), 0)
    c_i = lax.broadcasted_iota(jnp.int32, (tm, tm), 1)
    upper = jnp.where(r_i < c_i, 1.0, 0.0).astype(BF16)
    rank = _dot(sel2.astype(BF16), upper) + carry_ref[...]
    carry = carry_ref[...] + jnp.sum(sel2, axis=1, keepdims=True)
    carry_ref[...] = carry
    cnt_ref[...] = carry
    rank3 = rank.reshape(N_GROUPS, gsz, tm)

    raw = [pick_value(hit, scores) for hit in hits]
    tot = raw[0]
    for x in raw[1:]:
        tot = tot + x
    wrow = lax.broadcasted_iota(jnp.int32, (N_EXPERTS, tm), 0)
    wt = jnp.zeros((N_EXPERTS, tm), F32)
    for j in range(TOP_K):
        e_ref[j:j + 1, :] = picks[j].reshape(1, tm)
        r_ref[j:j + 1, :] = pick_value(hits[j], rank3).astype(jnp.int32)
        wt = jnp.where(wrow == j, raw[j] / tot * ROUTED_SCALE, wt)
    w_ref[...] = wt.T


def _router(lg_t, bias, tm):
    e, m = lg_t.shape
    tm = min(tm, m)
    row = pl.BlockSpec((TOP_K, tm), lambda i: (0, i))
    return pl.pallas_call(
        _router_kernel,
        grid=(m // tm,),
        in_specs=[pl.BlockSpec((e, tm), lambda i: (0, i)),
                  pl.BlockSpec((e, 1), lambda i: (0, 0))],
        out_specs=[row, row, pl.BlockSpec((tm, e), lambda i: (i, 0)),
                   pl.BlockSpec((e, 1), lambda i: (0, 0))],
        out_shape=[jax.ShapeDtypeStruct((TOP_K, m), jnp.int32),
                   jax.ShapeDtypeStruct((TOP_K, m), jnp.int32),
                   jax.ShapeDtypeStruct((m, e), F32),
                   jax.ShapeDtypeStruct((e, 1), F32)],
        scratch_shapes=[pltpu.VMEM((e, 1), F32)],
        compiler_params=_cparams(1),
        name="router_topk",
    )(lg_t, bias)


def _dest_kernel(off_ref, e_ref, r_ref, d_ref):
    e = e_ref[...]
    acc = r_ref[...]
    for x in range(N_EXPERTS):
        acc = acc + jnp.where(e == x, off_ref[x], 0)
    d_ref[...] = acc * ROW_SUB


def _dest(off, eidx, rank, tm):
    k, m = eidx.shape
    tm = min(tm, m)
    spec = pl.BlockSpec((k, tm), lambda i: (0, i))
    return pl.pallas_call(
        _dest_kernel,
        grid=(m // tm,),
        in_specs=[pl.BlockSpec(memory_space=pltpu.SMEM), spec, spec],
        out_specs=spec,
        out_shape=jax.ShapeDtypeStruct((k, m), jnp.int32),
        compiler_params=_cparams(1),
        name="moe_dest_rows",
    )(off, eidx, rank)


def _scatter_kernel(zt_ref, d_ref, x_ref, z_ref, xs_hbm, sem, *, tm, tile):
    i = pl.program_id(0)
    rs = ROW_SUB

    @pl.when(i == 0)
    def _():
        def zero(e, c):
            start = pl.multiple_of(zt_ref[e] * (tile * rs), tile * rs)
            pltpu.make_async_copy(z_ref, xs_hbm.at[pl.ds(start, tile * rs)], sem).start()
            return c

        lax.fori_loop(0, N_EXPERTS, zero, 0)

        def zwait(e, c):
            pltpu.make_async_copy(z_ref, xs_hbm.at[pl.ds(0, tile * rs)], sem).wait()
            return c

        lax.fori_loop(0, N_EXPERTS, zwait, 0)

    def body(n, c):
        src = x_ref.at[pl.ds(pl.multiple_of(n * rs, rs), rs)]
        for j in range(TOP_K):
            dst = xs_hbm.at[pl.ds(pl.multiple_of(d_ref[n * TOP_K + j], rs), rs)]
            pltpu.make_async_copy(src, dst, sem).start(priority=j % 2)
        return c

    lax.fori_loop(0, tm, body, 0)

    def wait(n, c):
        for j in range(TOP_K):
            pltpu.make_async_copy(x_ref.at[pl.ds(0, rs)], xs_hbm.at[pl.ds(0, rs)], sem).wait()
        return c

    lax.fori_loop(0, tm, wait, 0)


def _scatter_rows(zero_tile, dest, xp, rows, tile, tm):
    m = xp.shape[0] // ROW_SUB
    tm = min(tm, m)
    return pl.pallas_call(
        functools.partial(_scatter_kernel, tm=tm, tile=tile),
        grid_spec=pltpu.PrefetchScalarGridSpec(
            num_scalar_prefetch=1,
            grid=(m // tm,),
            in_specs=[pl.BlockSpec((tm * TOP_K,), lambda i, zt: (i,), memory_space=pltpu.SMEM),
                      pl.BlockSpec((tm * ROW_SUB, 128), lambda i, zt: (i, 0)),
                      pl.BlockSpec((tile * ROW_SUB, 128), lambda i, zt: (0, 0))],
            out_specs=pl.BlockSpec(memory_space=pl.ANY),
            scratch_shapes=[pltpu.SemaphoreType.DMA(())],
        ),
        out_shape=jax.ShapeDtypeStruct((rows * ROW_SUB, 128), jnp.uint32),
        compiler_params=_cparams(1),
        name="moe_scatter_rows",
    )(zero_tile, dest, xp, jnp.zeros((tile * ROW_SUB, 128), jnp.uint32))


def _expert_kernel(te_ref, nu_ref, x_ref, wg_ref, wu_ref, wd_ref, o_ref, wgb, wub, wdb):
    t = pl.program_id(0)
    tc = jnp.minimum(t, nu_ref[0] - 1)
    prev = jnp.maximum(tc - 1, 0)
    new_expert = jnp.logical_or(t == 0, te_ref[tc] != te_ref[prev])

    @pl.when(jnp.logical_and(new_expert, t < nu_ref[0]))
    def _():
        wgb[...] = wg_ref[0].astype(BF16)
        wub[...] = wu_ref[0].astype(BF16)
        wdb[...] = wd_ref[0].astype(BF16)

    @pl.when(t < nu_ref[0])
    def _():
        lo, hi = _load_rows(x_ref)
        d2 = lo.shape[1]
        gate = _dot(lo, wgb[0:d2, :]) + _dot(hi, wgb[d2:, :])
        up = _dot(lo, wub[0:d2, :]) + _dot(hi, wub[d2:, :])
        mid = (gate * jax.nn.sigmoid(gate) * up).astype(BF16)
        y = _dot(mid, wdb[...])
        _store_rows(o_ref, _pack_pair(y[:, :d2], y[:, d2:]))


def _experts(tile_expert, n_used, xs, wg, wu, wd, tile):
    rows = xs.shape[0] // ROW_SUB
    d = wg.shape[1]
    blk = (tile * ROW_SUB, 128)
    clamp = lambda t, te, nu: jnp.minimum(t, nu[0] - 1)
    return pl.pallas_call(
        _expert_kernel,
        grid_spec=pltpu.PrefetchScalarGridSpec(
            num_scalar_prefetch=2,
            grid=(rows // tile,),
            in_specs=[pl.BlockSpec(blk, lambda t, te, nu: (clamp(t, te, nu), 0)),
                      pl.BlockSpec((1, d, EXPERT_DIM), lambda t, te, nu: (te[clamp(t, te, nu)], 0, 0)),
                      pl.BlockSpec((1, d, EXPERT_DIM), lambda t, te, nu: (te[clamp(t, te, nu)], 0, 0)),
                      pl.BlockSpec((1, EXPERT_DIM, d), lambda t, te, nu: (te[clamp(t, te, nu)], 0, 0))],
            out_specs=pl.BlockSpec(blk, lambda t, te, nu: (clamp(t, te, nu), 0)),
            scratch_shapes=[pltpu.VMEM((d, EXPERT_DIM), BF16), pltpu.VMEM((d, EXPERT_DIM), BF16),
                            pltpu.VMEM((EXPERT_DIM, d), BF16)],
        ),
        out_shape=jax.ShapeDtypeStruct(xs.shape, jnp.uint32),
        compiler_params=_cparams(1),
        name="moe_experts",
    )(tile_expert, n_used, xs, wg, wu, wd)


def _combine_kernel(d_ref, ys_hbm, w_ref, h_ref, xp_ref, sg_ref, su_ref, sd_ref, o_ref, buf, sem,
                    *, tm):
    rs = ROW_SUB

    def body(n, c):
        row = pl.multiple_of(n * rs, rs)
        for j in range(TOP_K):
            src = ys_hbm.at[pl.ds(pl.multiple_of(d_ref[n * TOP_K + j], rs), rs)]
            pltpu.make_async_copy(src, buf.at[j, pl.ds(row, rs)], sem).start(priority=j % 2)
        return c

    lax.fori_loop(0, tm, body, 0)

    lo, hi = _load_rows(xp_ref)
    d2 = lo.shape[1]
    gate = _dot(lo, sg_ref[0:d2, :]) + _dot(hi, sg_ref[d2:, :])
    up = _dot(lo, su_ref[0:d2, :]) + _dot(hi, su_ref[d2:, :])
    mid = (gate * jax.nn.sigmoid(gate) * up).astype(BF16)
    o_ref[...] = h_ref[...] + _dot(mid, sd_ref[...])

    def wait(n, c):
        for j in range(TOP_K):
            pltpu.make_async_copy(ys_hbm.at[pl.ds(0, rs)], buf.at[0, pl.ds(0, rs)], sem).wait()
        return c

    lax.fori_loop(0, tm, wait, 0)

    wts = w_ref[...]
    wj = [wts[:, j:j + 1] for j in range(TOP_K)]
    for s in range(rs):
        r_lo = jnp.zeros((tm, 128), F32)
        r_hi = jnp.zeros((tm, 128), F32)
        for j in range(TOP_K):
            ylo, yhi = _unpack_pair(buf[j, pl.ds(s, tm, stride=rs), :])
            r_lo = r_lo + wj[j] * ylo
            r_hi = r_hi + wj[j] * yhi
        o_ref[:, s * 128:(s + 1) * 128] += r_lo
        o_ref[:, d2 + s * 128:d2 + (s + 1) * 128] += r_hi


def _combine(dest, ys, wcol, h, xp, sg, su, sd, tm):
    m, d = h.shape
    tm = min(tm, m)
    full = lambda a: pl.BlockSpec(a.shape, lambda i: (0,) * a.ndim)
    return pl.pallas_call(
        functools.partial(_combine_kernel, tm=tm),
        grid=(m // tm,),
        in_specs=[pl.BlockSpec((tm * TOP_K,), lambda i: (i,), memory_space=pltpu.SMEM),
                  pl.BlockSpec(memory_space=pl.ANY),
                  pl.BlockSpec((tm, N_EXPERTS), lambda i: (i, 0)),
                  pl.BlockSpec((tm, d), lambda i: (i, 0)),
                  pl.BlockSpec((tm * ROW_SUB, 128), lambda i: (i, 0)),
                  full(sg), full(su), full(sd)],
        out_specs=pl.BlockSpec((tm, d), lambda i: (i, 0)),
        out_shape=jax.ShapeDtypeStruct((m, d), F32),
        scratch_shapes=[pltpu.VMEM((TOP_K, tm * ROW_SUB, 128), jnp.uint32), pltpu.SemaphoreType.DMA(())],
        compiler_params=_cparams(1),
        name="moe_combine_shared",
    )(dest, ys, wcol, h, xp, sg, su, sd)


def _moe(h1, hnp, lg_t, router_bias, w_gate, w_up, w_down, ws_gate, ws_up, ws_down):
    n = h1.shape[0]
    tile = MOE_TILE
    eidx, rank, wcol, cnt = _router(lg_t, router_bias.reshape(-1, 1), 512)
    counts = cnt[:, 0].astype(jnp.int32)
    tiles_per = (counts + tile - 1) // tile
    tile_end = jnp.cumsum(tiles_per)
    tile_start = tile_end - tiles_per
    n_tiles = (n * TOP_K) // tile + N_EXPERTS
    rows = n_tiles * tile
    tile_expert = jnp.minimum(jnp.sum(tile_end[None, :] <= jnp.arange(n_tiles)[:, None], axis=1),
                              N_EXPERTS - 1).astype(jnp.int32)
    n_used = tile_end[-1:].astype(jnp.int32)
    last_tile = jnp.clip(tile_end - 1, 0, n_tiles - 1).astype(jnp.int32)
    dest = _dest((tile_start * tile).astype(jnp.int32), eidx, rank, 2048)
    dest = dest.T.reshape(-1)
    xs = _scatter_rows(last_tile, dest, hnp, rows, tile, 512)
    ys = _experts(tile_expert, n_used, xs, w_gate, w_up, w_down, tile)
    return _combine(dest, ys, wcol, h1, hnp, ws_gate.astype(BF16), ws_up.astype(BF16),
                    ws_down.astype(BF16), 256)


def _ple_kernel(h_ref, p_ref, gin_ref, wg_ref, bg_ref, wp_ref, gout_ref, o_ref):
    h = h_ref[...]
    gate = jax.nn.sigmoid(_dot(_rms(h, gin_ref[...]).astype(BF16), wg_ref[...]) + bg_ref[...])
    pp = _dot(p_ref[...].astype(BF16), wp_ref[...])
    o_ref[...] = h + _rms(pp * gate, gout_ref[...])


def _ple(h, p, g_in, w_g, b_g, w_p, g_out, tm):
    m, d = h.shape
    tm = min(tm, m)
    full = lambda a: pl.BlockSpec(a.shape, lambda i: (0,) * a.ndim)
    return pl.pallas_call(
        _ple_kernel,
        grid=(m // tm,),
        in_specs=[pl.BlockSpec((tm, d), lambda i: (i, 0)),
                  pl.BlockSpec((tm, p.shape[1]), lambda i: (i, 0)),
                  full(g_in), full(w_g), full(b_g), full(w_p), full(g_out)],
        out_specs=pl.BlockSpec((tm, d), lambda i: (i, 0)),
        out_shape=jax.ShapeDtypeStruct((m, d), F32),
        compiler_params=_cparams(1),
        name="ple",
    )(h, p, g_in, w_g, b_g, w_p, g_out)


def _pad_cols(a, width):
    return jnp.pad(a, ((0, 0), (0, width - a.shape[1])))


def _pad_rows(a, rows):
    return jnp.pad(a, ((0, rows - a.shape[0]), (0, 0)))


def _row(a):
    return a.reshape(1, -1)


def _rw_layout(a):
    c3 = 3 * RW_C
    parts = [a[..., :c3]]
    off = c3
    for width, padded in ((RW_DECAY_LORA, 128), (RW_A_LORA, 128), (RW_GATE_LORA, 256)):
        pad = [(0, 0)] * (a.ndim - 1) + [(0, padded - width)]
        parts.append(jnp.pad(a[..., off:off + width], pad))
        off += width
    return jnp.concatenate(parts, axis=-1)


def _mla_from_pm(pm, positions, g_qa, w_uq, g_kva, w_ukv, g_qn, g_kn):
    b, t, _ = pm.shape
    wq = w_uq.reshape(MLA_Q_LORA, MLA_HEADS, MLA_QK)
    wq = jnp.concatenate([wq[:, :, :MLA_NOPE].reshape(MLA_Q_LORA, -1),
                          wq[:, :, MLA_NOPE:].reshape(MLA_Q_LORA, -1)], axis=1).astype(BF16)
    half = MLA_ROPE // 2
    inv = ROPE_THETA ** (-jnp.arange(half, dtype=F32) / half)
    invf = _row(jnp.concatenate([inv, inv]))
    sgn = _row(jnp.concatenate([-jnp.ones(half, F32), jnp.ones(half, F32)]))
    q, k, v = _mla_prep(pm, positions.reshape(b, t, 1), invf, sgn, _row(g_qa), wq, _row(g_kva),
                        w_ukv.astype(BF16), _row(g_qn), _row(g_kn), 512)
    return _attention(q, k, v, 1024, 512, 2)


def _rwkv_from_z(z, mu, w0, w_w2, a0, a_w2, g_w2, k_k, k_a, r_k, ln_w, ln_b):
    head_of = jnp.arange(RW_C) // RW_HEAD
    bd = (head_of[:, None] == head_of[None, :]).astype(BF16)
    rr, ww, kx, vx, kkx, bx, gx = _rw_prep(
        z, _row(_rw_layout(mu)), _row(w0), _pad_rows(w_w2, 128).astype(BF16), _row(a0),
        _pad_rows(a_w2, 128).astype(BF16), _pad_rows(g_w2, 256).astype(BF16), _row(k_k), _row(k_a),
        bd, 256)
    return _rw_chunk(rr, ww, kx, vx, kkx, bx, gx, _row(r_k), _row(ln_w), _row(ln_b))


def _layer(h, p, positions, g_mix, w_in, mla_g_qa, mla_w_uq, mla_g_kva, mla_w_ukv, mla_g_qn,
           mla_g_kn, rw_mu, rw_w0, rw_w_w2, rw_a0, rw_a_w2, rw_g_w2, rw_k_k, rw_k_a, rw_r_k,
           rw_ln_w, rw_ln_b, w_out, g_ffn, w_router, router_bias, w_gate, w_up, w_down, ws_gate,
           ws_up, ws_down, g_ple_in, w_ple_gate, b_ple_gate, w_ple_proj, g_ple_out):
    b, t, d = h.shape
    n = b * t
    x2 = h.reshape(n, d)

    w_mla = _pad_cols(w_in[:, :MLA_COLS], MLA_COLS_PAD).astype(BF16)
    w_rw = _rw_layout(w_in[:, MLA_COLS:]).astype(BF16)
    pm = _norm_matmul(x2, _row(g_mix), w_mla, 512, MLA_COLS_PAD, F32, "in_proj_mla")
    z = _norm_matmul(x2, _row(g_mix), w_rw, 512, RW_COLS_PAD // 2, F32, "in_proj_rwkv")
    y_mla = _mla_from_pm(pm.reshape(b, t, -1), positions, mla_g_qa, mla_w_uq, mla_g_kva, mla_w_ukv,
                         mla_g_qn, mla_g_kn)
    y_rw = _rwkv_from_z(z.reshape(b, t, -1), rw_mu, rw_w0, rw_w_w2, rw_a0, rw_a_w2, rw_g_w2, rw_k_k,
                        rw_k_a, rw_r_k, rw_ln_w, rw_ln_b)

    h1, hnp, lg_t = _out_proj(y_mla.reshape(n, -1), y_rw.reshape(n, -1), x2, w_out.astype(BF16),
                              _row(g_ffn), w_router.T, 512)
    h2 = _moe(h1, hnp, lg_t, router_bias, w_gate, w_up, w_down, ws_gate, ws_up, ws_down)

    h3 = _ple(h2, p.reshape(n, -1), _row(g_ple_in), w_ple_gate.astype(BF16), _row(b_ple_gate),
              w_ple_proj.astype(BF16), _row(g_ple_out), 512)
    return h3.reshape(b, t, d)


def kernel(x, p, positions, g_mix, w_in, mla_g_qa, mla_w_uq, mla_g_kva, mla_w_ukv, mla_g_qn, mla_g_kn, rw_mu, rw_w0, rw_w_w2, rw_a0, rw_a_w2, rw_g_w2, rw_k_k, rw_k_a, rw_r_k, rw_ln_w, rw_ln_b, w_out, g_ffn, w_router, router_bias, w_gate, w_up, w_down, ws_gate, ws_up, ws_down, g_ple_in, w_ple_gate, b_ple_gate, w_ple_proj, g_ple_out):
    params = (g_mix, w_in, mla_g_qa, mla_w_uq, mla_g_kva, mla_w_ukv, mla_g_qn, mla_g_kn, rw_mu,
              rw_w0, rw_w_w2, rw_a0, rw_a_w2, rw_g_w2, rw_k_k, rw_k_a, rw_r_k, rw_ln_w, rw_ln_b,
              w_out, g_ffn, w_router, router_bias, w_gate, w_up, w_down, ws_gate, ws_up, ws_down,
              g_ple_in, w_ple_gate, b_ple_gate, w_ple_proj, g_ple_out)
    h = x
    for i in range(g_mix.shape[0]):
        h = _layer(h, p[i], positions, *[a[i] for a in params])
    return h
```

```python
import functools

import jax
import jax.numpy as jnp
from jax import lax
from jax.experimental import pallas as pl
from jax.experimental.pallas import tpu as pltpu

F32 = jnp.float32
BF16 = jnp.bfloat16

D_MODEL = 2048
PLE_DIM = 256
RMS_EPS = 1e-6

MLA_HEADS = 8
MLA_NOPE = 128
MLA_ROPE = 64
MLA_QK = MLA_NOPE + MLA_ROPE
MLA_V = 128
MLA_Q_LORA = 512
MLA_KV_LORA = 256
ROPE_THETA = 10000.0
MLA_COLS = MLA_Q_LORA + MLA_KV_LORA + MLA_ROPE
MLA_COLS_PAD = 896

RW_HEADS = 16
RW_HEAD = 64
RW_C = RW_HEADS * RW_HEAD
RW_DECAY_LORA = 64
RW_A_LORA = 64
RW_GATE_LORA = 160
RW_GN_EPS = 64e-5
RW_LORA_PAD = 512
RW_COLS_PAD = 3 * RW_C + RW_LORA_PAD
RW_CHUNK = 64

N_EXPERTS = 64
TOP_K = 8
N_GROUPS = 8
TOPK_GROUPS = 4
EXPERT_DIM = 512
ROUTED_SCALE = 2.5
MOE_TILE = 256
ROW_SUB = D_MODEL // 2 // 128

VMEM_LIMIT = 56 * 1024 * 1024


def _cparams(n_axes):
    return pltpu.CompilerParams(dimension_semantics=("arbitrary",) * n_axes,
                                vmem_limit_bytes=VMEM_LIMIT)


def _rms(x, g):
    ms = jnp.mean(x * x, axis=-1, keepdims=True)
    return x * lax.rsqrt(ms + RMS_EPS) * g


def _dot(a, b):
    return jnp.dot(a, b, preferred_element_type=F32)


def _dot_nt(a, b):
    return lax.dot_general(a, b, (((1,), (1,)), ((), ())), preferred_element_type=F32)


def _dot_tn(a, b):
    return lax.dot_general(a, b, (((0,), (0,)), ((), ())), preferred_element_type=F32)


def _split2(x):
    hi = x.astype(BF16)
    lo = (x - hi.astype(F32)).astype(BF16)
    return hi, lo


def _in_proj_kernel(x_ref, g_ref, w_ref, pm_ref, z_ref, xn_ref):
    j = pl.program_id(1)

    @pl.when(j == 0)
    def _():
        xn_ref[...] = _rms(x_ref[...], g_ref[...]).astype(BF16)

    acc = _dot(xn_ref[...], w_ref[...])

    @pl.when(j == 0)
    def _():
        pm_ref[...] = acc.astype(pm_ref.dtype)

    @pl.when(j > 0)
    def _():
        z_ref[...] = acc.astype(z_ref.dtype)


def _in_proj(x, g, w, tm):
    m, k = x.shape
    tn = MLA_COLS_PAD
    nt = w.shape[1] // tn
    tm = min(tm, m)
    return pl.pallas_call(
        _in_proj_kernel,
        grid=(m // tm, nt),
        in_specs=[pl.BlockSpec((tm, k), lambda i, j: (i, 0)),
                  pl.BlockSpec((1, k), lambda i, j: (0, 0)),
                  pl.BlockSpec((k, tn), lambda i, j: (0, j))],
        out_specs=[pl.BlockSpec((tm, tn), lambda i, j: (i, 0)),
                   pl.BlockSpec((tm, tn), lambda i, j: (i, jnp.maximum(j - 1, 0)))],
        out_shape=[jax.ShapeDtypeStruct((m, tn), BF16),
                   jax.ShapeDtypeStruct((m, (nt - 1) * tn), BF16)],
        scratch_shapes=[pltpu.VMEM((tm, k), BF16)],
        compiler_params=_cparams(2),
        name="in_proj",
    )(x, g, w)


def _mla_prep_kernel(pm_ref, pos_ref, invf_ref, sgn_ref, gqa_ref, wuq_ref, gkva_ref, wukv_ref,
                     gqn_ref, gkn_ref, q_ref, k_ref, v_ref):
    pm = pm_ref[0].astype(F32)
    ang = pos_ref[0].astype(F32) * invf_ref[...]
    cos = jnp.cos(ang)
    sin = jnp.sin(ang) * sgn_ref[...]

    def rope(xr):
        half = MLA_ROPE // 2
        swapped = jnp.concatenate([xr[:, half:], xr[:, :half]], axis=1)
        return xr * cos + swapped * sin

    scale = MLA_QK ** -0.5 * 1.4426950408889634
    gqn = gqn_ref[...]
    gkn = gkn_ref[...]

    cq = _rms(pm[:, :MLA_Q_LORA], gqa_ref[...]).astype(BF16)
    q = _dot(cq, wuq_ref[...])
    ckv = _rms(pm[:, MLA_Q_LORA:MLA_Q_LORA + MLA_KV_LORA], gkva_ref[...]).astype(BF16)
    kv = _dot(ckv, wukv_ref[...])
    kpe = pm[:, MLA_Q_LORA + MLA_KV_LORA:MLA_COLS]
    kpe_ssq = jnp.sum(kpe * kpe, axis=-1, keepdims=True)
    kpe_rot = rope(kpe * gkn[:, MLA_NOPE:])

    for h in range(MLA_HEADS):
        qn = q[:, h * MLA_NOPE:(h + 1) * MLA_NOPE]
        qr = q[:, MLA_HEADS * MLA_NOPE + h * MLA_ROPE:MLA_HEADS * MLA_NOPE + (h + 1) * MLA_ROPE]
        ssq = jnp.sum(qn * qn, axis=-1, keepdims=True) + jnp.sum(qr * qr, axis=-1, keepdims=True)
        inv = lax.rsqrt(ssq * (1.0 / MLA_QK) + RMS_EPS) * scale
        q_ref[0, h, :, 0:MLA_NOPE] = (qn * inv * gqn[:, :MLA_NOPE]).astype(BF16)
        q_ref[0, h, :, MLA_NOPE:MLA_QK] = (rope(qr * gqn[:, MLA_NOPE:]) * inv).astype(BF16)

        kn = kv[:, h * 256:h * 256 + MLA_NOPE]
        ssq = jnp.sum(kn * kn, axis=-1, keepdims=True) + kpe_ssq
        inv = lax.rsqrt(ssq * (1.0 / MLA_QK) + RMS_EPS)
        k_ref[0, h, :, 0:MLA_NOPE] = (kn * inv * gkn[:, :MLA_NOPE]).astype(BF16)
        k_ref[0, h, :, MLA_NOPE:MLA_QK] = (kpe_rot * inv).astype(BF16)
        v_ref[0, h, :, 0:MLA_V] = kv[:, h * 256 + MLA_NOPE:(h + 1) * 256].astype(BF16)
        v_ref[0, h, :, MLA_V:] = jnp.ones((kv.shape[0], MLA_V), BF16)


def _mla_prep(pm, pos, invf, sgn, g_qa, w_uq, g_kva, w_ukv, g_qn, g_kn, tm):
    b, t, _ = pm.shape
    tm = min(tm, t)
    full = lambda a: pl.BlockSpec(a.shape, lambda i, j: (0,) * a.ndim)
    qk_shape = jax.ShapeDtypeStruct((b, MLA_HEADS, t, MLA_QK), BF16)
    return pl.pallas_call(
        _mla_prep_kernel,
        grid=(b, t // tm),
        in_specs=[pl.BlockSpec((1, tm, MLA_COLS_PAD), lambda i, j: (i, j, 0)),
                  pl.BlockSpec((1, tm, 1), lambda i, j: (i, j, 0)),
                  full(invf), full(sgn), full(g_qa), full(w_uq), full(g_kva), full(w_ukv),
                  full(g_qn), full(g_kn)],
        out_specs=[pl.BlockSpec((1, MLA_HEADS, tm, MLA_QK), lambda i, j: (i, 0, j, 0)),
                   pl.BlockSpec((1, MLA_HEADS, tm, MLA_QK), lambda i, j: (i, 0, j, 0)),
                   pl.BlockSpec((1, MLA_HEADS, tm, 2 * MLA_V), lambda i, j: (i, 0, j, 0))],
        out_shape=[qk_shape, qk_shape, jax.ShapeDtypeStruct((b, MLA_HEADS, t, 2 * MLA_V), BF16)],
        compiler_params=_cparams(2),
        name="mla_prep",
    )(pm, pos, invf, sgn, g_qa, w_uq, g_kva, w_ukv, g_qn, g_kn)


def _attn_kernel(q_ref, k_ref, v_ref, o_ref, m_ref, acc_ref, *, tq, tk, nh):
    qi = pl.program_id(2)
    m_ref[...] = jnp.full(m_ref.shape, -jnp.inf, F32)
    acc_ref[...] = jnp.zeros(acc_ref.shape, F32)
    heads = range(nh)
    qs = [q_ref[0, h] for h in heads]

    def step(j, masked):
        start = pl.multiple_of(j * tk, tk)
        ss = [_dot_nt(qs[h], k_ref[0, h, pl.ds(start, tk), :]) for h in heads]
        for h in heads:
            s = ss[h]
            if masked:
                row = qi * tq + lax.broadcasted_iota(jnp.int32, (tq, tk), 0)
                col = j * tk + lax.broadcasted_iota(jnp.int32, (tq, tk), 1)
                s = jnp.where(col <= row, s, -jnp.inf)
            m_old = m_ref[h]
            m_new = jnp.maximum(m_old, jnp.max(s, axis=-1, keepdims=True))
            alpha = jnp.exp2(m_old - m_new)
            p = jnp.exp2(s - jnp.concatenate([m_new] * (tk // 128), axis=1))
            pv = _dot(p.astype(BF16), v_ref[0, h, pl.ds(start, tk), :])
            acc_ref[h] = jnp.concatenate([alpha, alpha], axis=1) * acc_ref[h] + pv
            m_ref[h] = m_new

    per = tq // tk
    n_full = qi * per

    def body(jj, c):
        for d in range(per):
            step(jj * per + d, False)
        return c

    lax.fori_loop(0, qi, body, 0)
    for d in range(tq // tk):
        step(n_full + d, True)
    for h in heads:
        acc = acc_ref[h]
        o_ref[0, :, h * MLA_V:(h + 1) * MLA_V] = (acc[:, :MLA_V] / acc[:, MLA_V:]).astype(o_ref.dtype)


def _attention(q, k, v, tq, tk, nh):
    b, h, t, _ = q.shape
    tq = min(tq, t)
    tk = min(tk, tq)
    return pl.pallas_call(
        functools.partial(_attn_kernel, tq=tq, tk=tk, nh=nh),
        grid=(b, h // nh, t // tq),
        in_specs=[pl.BlockSpec((1, nh, tq, MLA_QK), lambda i, j, n: (i, j, n, 0)),
                  pl.BlockSpec((1, nh, t, MLA_QK), lambda i, j, n: (i, j, 0, 0)),
                  pl.BlockSpec((1, nh, t, 2 * MLA_V), lambda i, j, n: (i, j, 0, 0))],
        out_specs=pl.BlockSpec((1, tq, nh * MLA_V), lambda i, j, n: (i, n, j)),
        out_shape=jax.ShapeDtypeStruct((b, t, h * MLA_V), BF16),
        scratch_shapes=[pltpu.VMEM((nh, tq, 128), F32), pltpu.VMEM((nh, tq, 2 * MLA_V), F32)],
        compiler_params=_cparams(3),
        name="mla_attention",
    )(q, k, v)


def _rw_prep_kernel(z_ref, zp_ref, mu_ref, w0_ref, ww2_ref, a0_ref, aw2_ref, gw2_ref, kk_ref,
                    ka_ref, bd_ref, r_o, w_o, k_o, v_o, kk_o, b_o, g_o):
    z = z_ref[0].astype(F32)
    prev = zp_ref[0][15:16, :].astype(F32)
    prev = jnp.where(pl.program_id(1) == 0, jnp.zeros_like(prev), prev)
    row = lax.broadcasted_iota(jnp.int32, z.shape, 0)
    zs = jnp.where(row == 0, prev, pltpu.roll(z, 1, axis=0))
    z = z + (zs - z) * mu_ref[...]

    c = RW_C
    r = z[:, 0:c]
    k = z[:, c:2 * c]
    v = z[:, 2 * c:3 * c]
    wl = z[:, 3 * c:3 * c + 128]
    al = z[:, 3 * c + 128:3 * c + 256]
    gl = z[:, 3 * c + 256:3 * c + 512]

    f = w0_ref[...] + _dot(jnp.tanh(wl).astype(BF16), ww2_ref[...])
    softplus_neg_f = jnp.maximum(-f, 0.0) + jnp.log(1.0 + jnp.exp(-jnp.abs(f)))
    w = -softplus_neg_f - 0.5
    w_o[0] = -jnp.exp(w)
    a = jax.nn.sigmoid(a0_ref[...] + _dot(al.astype(BF16), aw2_ref[...]))
    g_o[0] = _dot(jax.nn.sigmoid(gl).astype(BF16), gw2_ref[...]).astype(g_o.dtype)

    kk = k * kk_ref[...]
    hi, lo = _split2(kk * kk)
    ssq = _dot(hi, bd_ref[...]) + _dot(lo, bd_ref[...])
    kk = kk / jnp.maximum(jnp.sqrt(ssq), 1e-12)
    r_o[0] = r.astype(r_o.dtype)
    k_o[0] = (k * (1.0 + (a - 1.0) * ka_ref[...])).astype(k_o.dtype)
    v_o[0] = v.astype(v_o.dtype)
    kk_o[0] = kk.astype(kk_o.dtype)
    b_o[0] = (kk * a).astype(b_o.dtype)


def _rw_prep(z, mu, w0, ww2, a0, aw2, gw2, k_k, k_a, bd, tm):
    b, t, cols = z.shape
    tm = min(tm, t)
    full = lambda a: pl.BlockSpec(a.shape, lambda i, j: (0,) * a.ndim)
    out = [jax.ShapeDtypeStruct((b, t, RW_C), F32 if i == 1 else BF16) for i in range(7)]
    ospec = pl.BlockSpec((1, tm, RW_C), lambda i, j: (i, j, 0))
    return pl.pallas_call(
        _rw_prep_kernel,
        grid=(b, t // tm),
        in_specs=[pl.BlockSpec((1, tm, cols), lambda i, j: (i, j, 0)),
                  pl.BlockSpec((1, 16, cols), lambda i, j: (i, jnp.maximum(j * (tm // 16) - 1, 0), 0)),
                  full(mu), full(w0), full(ww2), full(a0), full(aw2), full(gw2), full(k_k),
                  full(k_a), full(bd)],
        out_specs=[ospec] * 7,
        out_shape=out,
        compiler_params=_cparams(2),
        name="rwkv_prep",
    )(z, z, mu, w0, ww2, a0, aw2, gw2, k_k, k_a, bd)


def _rw_chunk_kernel(r_ref, w_ref, k_ref, v_ref, kk_ref, b_ref, g_ref, rk_ref, lnw_ref, lnb_ref,
                     o_ref, s_ref):
    cs = RW_CHUNK
    hd = RW_HEAD

    @pl.when(pl.program_id(1) == 0)
    def _():
        s_ref[...] = jnp.zeros(s_ref.shape, F32)

    r = r_ref[0].astype(F32)
    w = w_ref[0]
    k = k_ref[0].astype(F32)
    v = v_ref[0].astype(F32)
    kk = kk_ref[0].astype(F32)
    b = b_ref[0].astype(F32)
    pairs = range(RW_HEADS // 2)
    pw = 2 * hd

    row = lax.broadcasted_iota(jnp.int32, (cs, pw), 0)
    col = lax.broadcasted_iota(jnp.int32, (cs, pw), 1)
    colh = jnp.bitwise_and(col, hd - 1)
    lo = col < hd
    strict = row > colh
    eye = jnp.where(row == colh, 1.0, 0.0).astype(F32)
    row2 = lax.broadcasted_iota(jnp.int32, (cs, 2 * pw), 0)
    col2 = lax.broadcasted_iota(jnp.int32, (cs, 2 * pw), 1)
    incl2 = row2 >= jnp.bitwise_and(col2, hd - 1)
    row3 = lax.broadcasted_iota(jnp.int32, (cs, 3 * cs), 0)
    col3 = lax.broadcasted_iota(jnp.int32, (cs, 3 * cs), 1)
    tri3 = jnp.where(row3 >= jnp.bitwise_and(col3, cs - 1), 1.0, 0.0).astype(BF16)

    def bdiag(x):
        zero = jnp.zeros_like(x)
        return jnp.concatenate([jnp.where(lo, x, zero), jnp.where(lo, zero, x)], axis=0)

    def cat3_lhs(x):
        hi = x.astype(BF16)
        return jnp.concatenate([hi, (x - hi.astype(F32)).astype(BF16), hi], axis=1)

    def cat3_rhs(x):
        hi, lw = _split2(x)
        bh = bdiag(hi)
        return jnp.concatenate([bh, bh, bdiag(lw)], axis=0)

    w1 = w.astype(BF16)
    w2 = (w - w1.astype(F32)).astype(BF16)
    w3 = (w - w1.astype(F32) - w2.astype(F32)).astype(BF16)
    logp = _dot(tri3, jnp.concatenate([w1, w2, w3], axis=0))
    logp_end = logp[cs - 1:cs, :]
    p_tail = jnp.exp(logp_end - logp)
    p_end = jnp.exp(logp_end)
    p_inv = jnp.exp(-logp)

    a_t = -kk * jnp.exp(logp - w)
    r_t = r * jnp.exp(logp)
    b_t = b * p_inv
    k_t = k * p_inv
    b_h = b * p_tail
    k_h = k * p_tail

    sls = [slice(p * pw, (p + 1) * pw) for p in pairs]
    lhs = [jnp.concatenate([a_t[:, sl], r_t[:, sl]], axis=0).astype(BF16) for sl in sls]
    rhs = [jnp.concatenate([bdiag(b_t[:, sl].astype(BF16)), bdiag(k_t[:, sl].astype(BF16))], axis=0)
           for sl in sls]
    gm = [_dot_nt(lhs[p], rhs[p]) for p in pairs]
    s0 = [s_ref[p] for p in pairs]
    ls = [_dot_nt(lhs[p], bdiag(s0[p].astype(BF16))) for p in pairs]
    vb = [v[:, sl].astype(BF16) for sl in sls]
    bv = [bdiag(vb[p]) for p in pairs]
    a_ak = [jnp.where(strict, gm[p][:cs, pw:], 0.0).astype(BF16) for p in pairs]
    rhs_u = [ls[p][:cs] + _dot(a_ak[p], bv[p]) for p in pairs]

    x = [jnp.where(strict, gm[p][:cs, :pw], 0.0) for p in pairs]
    tm = [eye + x[p] for p in pairs]
    x = [_dot(cat3_lhs(x[p]), cat3_rhs(x[p])) for p in pairs]
    for _ in range(4):
        both = [_dot(cat3_lhs(jnp.concatenate([x[p], tm[p]], axis=0)), cat3_rhs(x[p])) for p in pairs]
        x = [both[p][:cs] for p in pairs]
        tm = [tm[p] + both[p][cs:] for p in pairs]
    tm = [tm[p] + _dot(cat3_lhs(tm[p]), cat3_rhs(x[p])) for p in pairs]

    u = [_dot(cat3_lhs(tm[p]), cat3_rhs(rhs_u[p])) for p in pairs]
    ub = [u[p].astype(BF16) for p in pairs]
    a_r = [jnp.where(incl2, gm[p][cs:, :], 0.0).astype(BF16) for p in pairs]
    y = [ls[p][cs:] + _dot(a_r[p], jnp.concatenate([bdiag(ub[p]), bv[p]], axis=0)) for p in pairs]
    for p in pairs:
        uv = jnp.concatenate([ub[p], vb[p]], axis=0)
        bk = jnp.concatenate([b_h[:, sls[p]], k_h[:, sls[p]]], axis=0).astype(BF16)
        full = _dot_tn(uv, bk)
        s_ref[p] = s0[p] * p_end[:, sls[p]] + jnp.where(lo, full[:hd], full[hd:])

    def head_sum(t):
        s_lo = jnp.sum(jnp.where(lo, t, 0.0), axis=-1, keepdims=True)
        s_all = jnp.sum(t, axis=-1, keepdims=True)
        return jnp.where(lo, s_lo, s_all - s_lo)

    rk_all = r * k * rk_ref[...]
    g = g_ref[0].astype(F32)
    for p in pairs:
        sl = sls[p]
        yc = y[p] - head_sum(y[p]) * (1.0 / hd)
        var = head_sum(yc * yc) * (1.0 / hd)
        yn = yc * lax.rsqrt(var + RW_GN_EPS) * lnw_ref[:, sl] + lnb_ref[:, sl]
        bonus = head_sum(rk_all[:, sl]) * v[:, sl]
        o_ref[0, :, sl] = ((yn + bonus) * g[:, sl]).astype(o_ref.dtype)


def _rw_chunk(r, w, k, v, kk, bb, g, r_k, ln_w, ln_b):
    b, t, c = r.shape
    cs = RW_CHUNK
    full = lambda a: pl.BlockSpec(a.shape, lambda i, j: (0,) * a.ndim)
    spec = pl.BlockSpec((1, cs, c), lambda i, j: (i, j, 0))
    return pl.pallas_call(
        _rw_chunk_kernel,
        grid=(b, t // cs),
        in_specs=[spec] * 7 + [full(r_k), full(ln_w), full(ln_b)],
        out_specs=spec,
        out_shape=jax.ShapeDtypeStruct((b, t, c), BF16),
        scratch_shapes=[pltpu.VMEM((RW_HEADS // 2, RW_HEAD, 2 * RW_HEAD), F32)],
        compiler_params=_cparams(2),
        name="rwkv_chunk",
    )(r, w, k, v, kk, bb, g, r_k, ln_w, ln_b)


def _pack_pair(a, b):
    ua = lax.bitcast_convert_type(a.astype(BF16).astype(F32), jnp.uint32)
    ub = lax.bitcast_convert_type(b.astype(BF16).astype(F32), jnp.uint32)
    return lax.shift_right_logical(ua, jnp.uint32(16)) | (ub & jnp.uint32(0xFFFF0000))


def _unpack_pair(u):
    lo = lax.bitcast_convert_type(lax.shift_left(u, jnp.uint32(16)), F32)
    hi = lax.bitcast_convert_type(u & jnp.uint32(0xFFFF0000), F32)
    return lo, hi


def _store_rows(ref, packed):
    m = packed.shape[0]
    for s in range(ROW_SUB):
        ref[pl.ds(s, m, stride=ROW_SUB), :] = packed[:, s * 128:(s + 1) * 128]


def _load_rows(ref):
    m = ref.shape[-2] // ROW_SUB
    los, his = [], []
    for s in range(ROW_SUB):
        lo, hi = _unpack_pair(ref[pl.ds(s, m, stride=ROW_SUB), :])
        los.append(lo.astype(BF16))
        his.append(hi.astype(BF16))
    return jnp.concatenate(los, axis=1), jnp.concatenate(his, axis=1)


def _out_proj_kernel(ya_ref, yr_ref, x_ref, w_ref, g_ref, wrh_ref, wrl_ref, h_ref, hn_ref, lg_ref):
    half = ya_ref.shape[1]
    acc = _dot(ya_ref[...], w_ref[0:half, :]) + _dot(yr_ref[...], w_ref[half:, :])
    h = x_ref[...] + acc
    h_ref[...] = h
    hn = _rms(h, g_ref[...])
    d2 = hn.shape[1] // 2
    _store_rows(hn_ref, _pack_pair(hn[:, :d2], hn[:, d2:]))
    hh, hl = _split2(hn)
    lg_ref[...] = (_dot_nt(wrh_ref[...], hh) + _dot_nt(wrh_ref[...], hl)
                   + _dot_nt(wrl_ref[...], hh))


def _out_proj(ya, yr, x, w, g, wr, tm):
    wr_hi, wr_lo = _split2(wr.T)
    m, d = x.shape
    tm = min(tm, m)
    full = lambda a: pl.BlockSpec(a.shape, lambda i: (0,) * a.ndim)
    return pl.pallas_call(
        _out_proj_kernel,
        grid=(m // tm,),
        in_specs=[pl.BlockSpec((tm, ya.shape[1]), lambda i: (i, 0)),
                  pl.BlockSpec((tm, yr.shape[1]), lambda i: (i, 0)),
                  pl.BlockSpec((tm, d), lambda i: (i, 0)),
                  full(w), full(g), full(wr_hi), full(wr_lo)],
        out_specs=[pl.BlockSpec((tm, d), lambda i: (i, 0)),
                   pl.BlockSpec((tm * ROW_SUB, 128), lambda i: (i, 0)),
                   pl.BlockSpec((N_EXPERTS, tm), lambda i: (0, i))],
        out_shape=[jax.ShapeDtypeStruct((m, d), F32), jax.ShapeDtypeStruct((m * ROW_SUB, 128), jnp.uint32),
                   jax.ShapeDtypeStruct((N_EXPERTS, m), F32)],
        compiler_params=_cparams(1),
        name="out_proj_router_logits",
    )(ya, yr, x, w, g, wr_hi, wr_lo)


def _router_kernel(lg_ref, bias_ref, e_ref, r_ref, w_ref, cnt_ref, carry_ref):
    tm = lg_ref.shape[1]
    gsz = N_EXPERTS // N_GROUPS

    @pl.when(pl.program_id(0) == 0)
    def _():
        carry_ref[...] = jnp.zeros(carry_ref.shape, F32)

    scores = jax.nn.sigmoid(lg_ref[...]).reshape(N_GROUPS, gsz, tm)
    biased = scores + bias_ref[...].reshape(N_GROUPS, gsz, 1)
    neg = -jnp.inf

    eidx = lax.broadcasted_iota(jnp.int32, biased.shape, 1)
    m1 = jnp.max(biased, axis=1, keepdims=True)
    first = jnp.min(jnp.where(biased == m1, eidx, gsz), axis=1, keepdims=True)
    m2 = jnp.max(jnp.where(eidx == first, neg, biased), axis=1, keepdims=True)
    gscore = m1 + m2

    gidx = lax.broadcasted_iota(jnp.int32, gscore.shape, 0)
    gsel = jnp.zeros(gscore.shape, jnp.bool_)
    work = gscore
    for _ in range(TOPK_GROUPS):
        best = jnp.max(work, axis=0, keepdims=True)
        pick = jnp.min(jnp.where(work == best, gidx, N_GROUPS), axis=0, keepdims=True)
        hit = gidx == pick
        gsel = jnp.logical_or(gsel, hit)
        work = jnp.where(hit, neg, work)

    flat = lax.broadcasted_iota(jnp.int32, biased.shape, 0) * gsz + eidx
    work = jnp.where(gsel, biased, neg)
    hits = []
    picks = []
    for _ in range(TOP_K):
        best = jnp.max(jnp.max(work, axis=1, keepdims=True), axis=0, keepdims=True)
        cand = jnp.where(work == best, flat, N_EXPERTS)
        pick = jnp.min(jnp.min(cand, axis=1, keepdims=True), axis=0, keepdims=True)
        hit = flat == pick
        hits.append(hit)
        picks.append(pick)
        work = jnp.where(hit, neg, work)

    def pick_value(hit, val):
        s = jnp.sum(jnp.sum(jnp.where(hit, val, 0.0), axis=1, keepdims=True), axis=0, keepdims=True)
        return s.reshape(1, tm)

    sel = jnp.zeros(biased.shape, F32)
    for hit in hits:
        sel = jnp.where(hit, 1.0, sel)
    sel2 = sel.reshape(N_EXPERTS, tm)
    r_i = lax.broadcasted_iota(jnp.int32, (tm, tm), 0)
    c_i = lax.broadcasted_iota(jnp.int32, (tm, tm), 1)
    upper = jnp.where(r_i < c_i, 1.0, 0.0).astype(BF16)
    rank = _dot(sel2.astype(BF16), upper) + carry_ref[...]
    carry = carry_ref[...] + jnp.sum(sel2, axis=1, keepdims=True)
    carry_ref[...] = carry
    cnt_ref[...] = carry
    rank3 = rank.reshape(N_GROUPS, gsz, tm)

    raw = [pick_value(hit, scores) for hit in hits]
    tot = raw[0]
    for x in raw[1:]:
        tot = tot + x
    wrow = lax.broadcasted_iota(jnp.int32, (N_EXPERTS, tm), 0)
    wt = jnp.zeros((N_EXPERTS, tm), F32)
    for j in range(TOP_K):
        e_ref[j:j + 1, :] = picks[j].reshape(1, tm)
        r_ref[j:j + 1, :] = pick_value(hits[j], rank3).astype(jnp.int32)
        wt = jnp.where(wrow == j, raw[j] / tot * ROUTED_SCALE, wt)
    w_ref[...] = wt.T


def _router(lg_t, bias, tm):
    e, m = lg_t.shape
    tm = min(tm, m)
    row = pl.BlockSpec((TOP_K, tm), lambda i: (0, i))
    return pl.pallas_call(
        _router_kernel,
        grid=(m // tm,),
        in_specs=[pl.BlockSpec((e, tm), lambda i: (0, i)),
                  pl.BlockSpec((e, 1), lambda i: (0, 0))],
        out_specs=[row, row, pl.BlockSpec((tm, e), lambda i: (i, 0)),
                   pl.BlockSpec((e, 1), lambda i: (0, 0))],
        out_shape=[jax.ShapeDtypeStruct((TOP_K, m), jnp.int32),
                   jax.ShapeDtypeStruct((TOP_K, m), jnp.int32),
                   jax.ShapeDtypeStruct((m, e), F32),
                   jax.ShapeDtypeStruct((e, 1), F32)],
        scratch_shapes=[pltpu.VMEM((e, 1), F32)],
        compiler_params=_cparams(1),
        name="router_topk",
    )(lg_t, bias)


def _dest_kernel(off_ref, e_ref, r_ref, d_ref):
    e = e_ref[...]
    acc = r_ref[...]
    for x in range(N_EXPERTS):
        acc = acc + jnp.where(e == x, off_ref[x], 0)
    d_ref[...] = acc * ROW_SUB


def _dest(off, eidx, rank, tm):
    k, m = eidx.shape
    tm = min(tm, m)
    spec = pl.BlockSpec((k, tm), lambda i: (0, i))
    return pl.pallas_call(
        _dest_kernel,
        grid=(m // tm,),
        in_specs=[pl.BlockSpec(memory_space=pltpu.SMEM), spec, spec],
        out_specs=spec,
        out_shape=jax.ShapeDtypeStruct((k, m), jnp.int32),
        compiler_params=_cparams(1),
        name="moe_dest_rows",
    )(off, eidx, rank)


def _scatter_kernel(zt_ref, d_ref, x_ref, z_ref, xs_hbm, sem, *, tm, tile):
    i = pl.program_id(0)
    rs = ROW_SUB

    @pl.when(i == 0)
    def _():
        def zero(e, c):
            start = pl.multiple_of(zt_ref[e] * (tile * rs), tile * rs)
            pltpu.make_async_copy(z_ref, xs_hbm.at[pl.ds(start, tile * rs)], sem).start()
            return c

        lax.fori_loop(0, N_EXPERTS, zero, 0)

        def zwait(e, c):
            pltpu.make_async_copy(z_ref, xs_hbm.at[pl.ds(0, tile * rs)], sem).wait()
            return c

        lax.fori_loop(0, N_EXPERTS, zwait, 0)

    def body(n, c):
        src = x_ref.at[pl.ds(pl.multiple_of(n * rs, rs), rs)]
        for j in range(TOP_K):
            dst = xs_hbm.at[pl.ds(pl.multiple_of(d_ref[n * TOP_K + j], rs), rs)]
            pltpu.make_async_copy(src, dst, sem).start(priority=j % 2)
        return c

    lax.fori_loop(0, tm, body, 0)

    def wait(n, c):
        for j in range(TOP_K):
            pltpu.make_async_copy(x_ref.at[pl.ds(0, rs)], xs_hbm.at[pl.ds(0, rs)], sem).wait()
        return c

    lax.fori_loop(0, tm, wait, 0)


def _scatter_rows(zero_tile, dest, xp, rows, tile, tm):
    m = xp.shape[0] // ROW_SUB
    tm = min(tm, m)
    return pl.pallas_call(
        functools.partial(_scatter_kernel, tm=tm, tile=tile),
        grid_spec=pltpu.PrefetchScalarGridSpec(
            num_scalar_prefetch=1,
            grid=(m // tm,),
            in_specs=[pl.BlockSpec((tm * TOP_K,), lambda i, zt: (i,), memory_space=pltpu.SMEM),
                      pl.BlockSpec((tm * ROW_SUB, 128), lambda i, zt: (i, 0)),
                      pl.BlockSpec((tile * ROW_SUB, 128), lambda i, zt: (0, 0))],
            out_specs=pl.BlockSpec(memory_space=pl.ANY),
            scratch_shapes=[pltpu.SemaphoreType.DMA(())],
        ),
        out_shape=jax.ShapeDtypeStruct((rows * ROW_SUB, 128), jnp.uint32),
        compiler_params=_cparams(1),
        name="moe_scatter_rows",
    )(zero_tile, dest, xp, jnp.zeros((tile * ROW_SUB, 128), jnp.uint32))


def _expert_kernel(te_ref, nu_ref, x_ref, wg_ref, wu_ref, wd_ref, o_ref, wgb, wub, wdb):
    t = pl.program_id(0)
    tc = jnp.minimum(t, nu_ref[0] - 1)
    prev = jnp.maximum(tc - 1, 0)
    new_expert = jnp.logical_or(t == 0, te_ref[tc] != te_ref[prev])

    @pl.when(jnp.logical_and(new_expert, t < nu_ref[0]))
    def _():
        wgb[...] = wg_ref[0].astype(BF16)
        wub[...] = wu_ref[0].astype(BF16)
        wdb[...] = wd_ref[0].astype(BF16)

    @pl.when(t < nu_ref[0])
    def _():
        lo, hi = _load_rows(x_ref)
        d2 = lo.shape[1]
        gate = _dot(lo, wgb[0:d2, :]) + _dot(hi, wgb[d2:, :])
        up = _dot(lo, wub[0:d2, :]) + _dot(hi, wub[d2:, :])
        mid = (gate * jax.nn.sigmoid(gate) * up).astype(BF16)
        y = _dot(mid, wdb[...])
        _store_rows(o_ref, _pack_pair(y[:, :d2], y[:, d2:]))


def _experts(tile_expert, n_used, xs, wg, wu, wd, tile):
    rows = xs.shape[0] // ROW_SUB
    d = wg.shape[1]
    blk = (tile * ROW_SUB, 128)
    clamp = lambda t, te, nu: jnp.minimum(t, nu[0] - 1)
    return pl.pallas_call(
        _expert_kernel,
        grid_spec=pltpu.PrefetchScalarGridSpec(
            num_scalar_prefetch=2,
            grid=(rows // tile,),
            in_specs=[pl.BlockSpec(blk, lambda t, te, nu: (clamp(t, te, nu), 0)),
                      pl.BlockSpec((1, d, EXPERT_DIM), lambda t, te, nu: (te[clamp(t, te, nu)], 0, 0)),
                      pl.BlockSpec((1, d, EXPERT_DIM), lambda t, te, nu: (te[clamp(t, te, nu)], 0, 0)),
                      pl.BlockSpec((1, EXPERT_DIM, d), lambda t, te, nu: (te[clamp(t, te, nu)], 0, 0))],
            out_specs=pl.BlockSpec(blk, lambda t, te, nu: (clamp(t, te, nu), 0)),
            scratch_shapes=[pltpu.VMEM((d, EXPERT_DIM), BF16), pltpu.VMEM((d, EXPERT_DIM), BF16),
                            pltpu.VMEM((EXPERT_DIM, d), BF16)],
        ),
        out_shape=jax.ShapeDtypeStruct(xs.shape, jnp.uint32),
        compiler_params=_cparams(1),
        name="moe_experts",
    )(tile_expert, n_used, xs, wg, wu, wd)


def _combine_kernel(d_ref, dn_ref, ys_hbm, w_ref, h_ref, xp_ref, sg_ref, su_ref, sd_ref, o_ref, buf, sem,
                    *, tm, nt):
    i = pl.program_id(0)
    slot = jnp.bitwise_and(i, 1)
    nxt = 1 - slot
    rs = ROW_SUB

    def start(idx_ref, n, s):
        for j in range(TOP_K):
            src = ys_hbm.at[pl.ds(pl.multiple_of(idx_ref[n * TOP_K + j], rs), rs)]
            pltpu.make_async_copy(src, buf.at[s, j, pl.ds(n * rs, rs)], sem.at[s]).start(priority=j % 2)

    def wait_all(s):
        for j in range(TOP_K):
            pltpu.make_async_copy(ys_hbm.at[pl.ds(0, tm * rs)], buf.at[s, j], sem.at[s]).wait()

    @pl.when(i == 0)
    def _():
        def body(n, c):
            for j in range(TOP_K):
                src = ys_hbm.at[pl.ds(pl.multiple_of(d_ref[n * TOP_K + j], rs), rs)]
                dst = buf.at[0, j, pl.ds(pl.multiple_of(n * rs, rs), rs)]
                pltpu.make_async_copy(src, dst, sem.at[0]).start(priority=j % 2)
            return c

        lax.fori_loop(0, tm, body, 0)

    per = tm // (rs * TOP_K)
    batches = iter(range(0, tm, per))

    def start_batch():
        n0 = next(batches)
        for n in range(n0, n0 + per):
            start(dn_ref, n, nxt)

    lo, hi = _load_rows(xp_ref)
    d2 = lo.shape[1]
    gate = _dot(lo, sg_ref[0:d2, :]) + _dot(hi, sg_ref[d2:, :])
    up = _dot(lo, su_ref[0:d2, :]) + _dot(hi, su_ref[d2:, :])
    mid = (gate * jax.nn.sigmoid(gate) * up).astype(BF16)
    o_ref[...] = h_ref[...] + _dot(mid, sd_ref[...])

    wait_all(slot)
    wts = w_ref[...]
    wj = [wts[:, j:j + 1] for j in range(TOP_K)]
    for s in range(rs):
        r_lo = jnp.zeros((tm, 128), F32)
        r_hi = jnp.zeros((tm, 128), F32)
        for j in range(TOP_K):
            start_batch()
            ylo, yhi = _unpack_pair(buf[slot, j, pl.ds(s, tm, stride=rs), :])
            r_lo = r_lo + wj[j] * ylo
            r_hi = r_hi + wj[j] * yhi
        o_ref[:, s * 128:(s + 1) * 128] += r_lo
        o_ref[:, d2 + s * 128:d2 + (s + 1) * 128] += r_hi

    @pl.when(i == nt - 1)
    def _():
        wait_all(nxt)


def _combine(dest, ys, wcol, h, xp, sg, su, sd, tm):
    m, d = h.shape
    tm = min(tm, m)
    nt = m // tm
    full = lambda a: pl.BlockSpec(a.shape, lambda i: (0,) * a.ndim)
    idx = lambda f: pl.BlockSpec((tm * TOP_K,), f, memory_space=pltpu.SMEM)
    return pl.pallas_call(
        functools.partial(_combine_kernel, tm=tm, nt=nt),
        grid=(nt,),
        in_specs=[idx(lambda i: (i,)), idx(lambda i: (jnp.minimum(i + 1, nt - 1),)),
                  pl.BlockSpec(memory_space=pl.ANY),
                  pl.BlockSpec((tm, N_EXPERTS), lambda i: (i, 0)),
                  pl.BlockSpec((tm, d), lambda i: (i, 0)),
                  pl.BlockSpec((tm * ROW_SUB, 128), lambda i: (i, 0)),
                  full(sg), full(su), full(sd)],
        out_specs=pl.BlockSpec((tm, d), lambda i: (i, 0)),
        out_shape=jax.ShapeDtypeStruct((m, d), F32),
        scratch_shapes=[pltpu.VMEM((2, TOP_K, tm * ROW_SUB, 128), jnp.uint32),
                        pltpu.SemaphoreType.DMA((2,))],
        compiler_params=_cparams(1),
        name="moe_combine_shared",
    )(dest, dest, ys, wcol, h, xp, sg, su, sd)


def _moe(h1, hnp, lg_t, router_bias, w_gate, w_up, w_down, ws_gate, ws_up, ws_down):
    n = h1.shape[0]
    tile = MOE_TILE
    eidx, rank, wcol, cnt = _router(lg_t, router_bias.reshape(-1, 1), 512)
    counts = cnt[:, 0].astype(jnp.int32)
    tiles_per = (counts + tile - 1) // tile
    tile_end = jnp.cumsum(tiles_per)
    tile_start = tile_end - tiles_per
    n_tiles = (n * TOP_K) // tile + N_EXPERTS
    rows = n_tiles * tile
    tile_expert = jnp.minimum(jnp.sum(tile_end[None, :] <= jnp.arange(n_tiles)[:, None], axis=1),
                              N_EXPERTS - 1).astype(jnp.int32)
    n_used = tile_end[-1:].astype(jnp.int32)
    last_tile = jnp.clip(tile_end - 1, 0, n_tiles - 1).astype(jnp.int32)
    dest = _dest((tile_start * tile).astype(jnp.int32), eidx, rank, 2048)
    dest = dest.T.reshape(-1)
    xs = _scatter_rows(last_tile, dest, hnp, rows, tile, 512)
    ys = _experts(tile_expert, n_used, xs, w_gate, w_up, w_down, tile)
    return _combine(dest, ys, wcol, h1, hnp, ws_gate.astype(BF16), ws_up.astype(BF16),
                    ws_down.astype(BF16), 256)


def _ple_kernel(h_ref, p_ref, gin_ref, wg_ref, bg_ref, wp_ref, gout_ref, o_ref):
    h = h_ref[...]
    gate = jax.nn.sigmoid(_dot(_rms(h, gin_ref[...]).astype(BF16), wg_ref[...]) + bg_ref[...])
    pp = _dot(p_ref[...].astype(BF16), wp_ref[...])
    o_ref[...] = h + _rms(pp * gate, gout_ref[...])


def _ple(h, p, g_in, w_g, b_g, w_p, g_out, tm):
    m, d = h.shape
    tm = min(tm, m)
    full = lambda a: pl.BlockSpec(a.shape, lambda i: (0,) * a.ndim)
    return pl.pallas_call(
        _ple_kernel,
        grid=(m // tm,),
        in_specs=[pl.BlockSpec((tm, d), lambda i: (i, 0)),
                  pl.BlockSpec((tm, p.shape[1]), lambda i: (i, 0)),
                  full(g_in), full(w_g), full(b_g), full(w_p), full(g_out)],
        out_specs=pl.BlockSpec((tm, d), lambda i: (i, 0)),
        out_shape=jax.ShapeDtypeStruct((m, d), F32),
        compiler_params=_cparams(1),
        name="ple",
    )(h, p, g_in, w_g, b_g, w_p, g_out)


def _pad_cols(a, width):
    return jnp.pad(a, ((0, 0), (0, width - a.shape[1])))


def _pad_rows(a, rows):
    return jnp.pad(a, ((0, rows - a.shape[0]), (0, 0)))


def _row(a):
    return a.reshape(1, -1)


def _rw_layout(a):
    c3 = 3 * RW_C
    parts = [a[..., :c3]]
    off = c3
    for width, padded in ((RW_DECAY_LORA, 128), (RW_A_LORA, 128), (RW_GATE_LORA, 256)):
        pad = [(0, 0)] * (a.ndim - 1) + [(0, padded - width)]
        parts.append(jnp.pad(a[..., off:off + width], pad))
        off += width
    return jnp.concatenate(parts, axis=-1)


def _mla_from_pm(pm, positions, g_qa, w_uq, g_kva, w_ukv, g_qn, g_kn):
    b, t, _ = pm.shape
    wq = w_uq.reshape(MLA_Q_LORA, MLA_HEADS, MLA_QK)
    wq = jnp.concatenate([wq[:, :, :MLA_NOPE].reshape(MLA_Q_LORA, -1),
                          wq[:, :, MLA_NOPE:].reshape(MLA_Q_LORA, -1)], axis=1).astype(BF16)
    half = MLA_ROPE // 2
    inv = ROPE_THETA ** (-jnp.arange(half, dtype=F32) / half)
    invf = _row(jnp.concatenate([inv, inv]))
    sgn = _row(jnp.concatenate([-jnp.ones(half, F32), jnp.ones(half, F32)]))
    q, k, v = _mla_prep(pm, positions.reshape(b, t, 1), invf, sgn, _row(g_qa), wq, _row(g_kva),
                        w_ukv.astype(BF16), _row(g_qn), _row(g_kn), 512)
    return _attention(q, k, v, 1024, 512, 2)


def _rwkv_from_z(z, mu, w0, w_w2, a0, a_w2, g_w2, k_k, k_a, r_k, ln_w, ln_b):
    head_of = jnp.arange(RW_C) // RW_HEAD
    bd = (head_of[:, None] == head_of[None, :]).astype(BF16)
    rr, ww, kx, vx, kkx, bx, gx = _rw_prep(
        z, _row(_rw_layout(mu)), _row(w0), _pad_rows(w_w2, 128).astype(BF16), _row(a0),
        _pad_rows(a_w2, 128).astype(BF16), _pad_rows(g_w2, 256).astype(BF16), _row(k_k), _row(k_a),
        bd, 256)
    return _rw_chunk(rr, ww, kx, vx, kkx, bx, gx, _row(r_k), _row(ln_w), _row(ln_b))


def _layer(h, p, positions, g_mix, w_in, mla_g_qa, mla_w_uq, mla_g_kva, mla_w_ukv, mla_g_qn,
           mla_g_kn, rw_mu, rw_w0, rw_w_w2, rw_a0, rw_a_w2, rw_g_w2, rw_k_k, rw_k_a, rw_r_k,
           rw_ln_w, rw_ln_b, w_out, g_ffn, w_router, router_bias, w_gate, w_up, w_down, ws_gate,
           ws_up, ws_down, g_ple_in, w_ple_gate, b_ple_gate, w_ple_proj, g_ple_out):
    b, t, d = h.shape
    n = b * t
    x2 = h.reshape(n, d)

    w_mla = _pad_cols(w_in[:, :MLA_COLS], MLA_COLS_PAD).astype(BF16)
    w_rw = _rw_layout(w_in[:, MLA_COLS:]).astype(BF16)
    pm, z = _in_proj(x2, _row(g_mix), jnp.concatenate([w_mla, w_rw], axis=1), 1024)
    y_mla = _mla_from_pm(pm.reshape(b, t, -1), positions, mla_g_qa, mla_w_uq, mla_g_kva, mla_w_ukv,
                         mla_g_qn, mla_g_kn)
    y_rw = _rwkv_from_z(z.reshape(b, t, -1), rw_mu, rw_w0, rw_w_w2, rw_a0, rw_a_w2, rw_g_w2, rw_k_k,
                        rw_k_a, rw_r_k, rw_ln_w, rw_ln_b)

    h1, hnp, lg_t = _out_proj(y_mla.reshape(n, -1), y_rw.reshape(n, -1), x2, w_out.astype(BF16),
                              _row(g_ffn), w_router, 512)
    h2 = _moe(h1, hnp, lg_t, router_bias, w_gate, w_up, w_down, ws_gate, ws_up, ws_down)

    h3 = _ple(h2, p.reshape(n, -1), _row(g_ple_in), w_ple_gate.astype(BF16), _row(b_ple_gate),
              w_ple_proj.astype(BF16), _row(g_ple_out), 512)
    return h3.reshape(b, t, d)


def kernel(x, p, positions, g_mix, w_in, mla_g_qa, mla_w_uq, mla_g_kva, mla_w_ukv, mla_g_qn, mla_g_kn, rw_mu, rw_w0, rw_w_w2, rw_a0, rw_a_w2, rw_g_w2, rw_k_k, rw_k_a, rw_r_k, rw_ln_w, rw_ln_b, w_out, g_ffn, w_router, router_bias, w_gate, w_up, w_down, ws_gate, ws_up, ws_down, g_ple_in, w_ple_gate, b_ple_gate, w_ple_proj, g_ple_out):
    params = (g_mix, w_in, mla_g_qa, mla_w_uq, mla_g_kva, mla_w_ukv, mla_g_qn, mla_g_kn, rw_mu,
              rw_w0, rw_w_w2, rw_a0, rw_a_w2, rw_g_w2, rw_k_k, rw_k_a, rw_r_k, rw_ln_w, rw_ln_b,
              w_out, g_ffn, w_router, router_bias, w_gate, w_up, w_down, ws_gate, ws_up, ws_down,
              g_ple_in, w_ple_gate, b_ple_gate, w_ple_proj, g_ple_out)
    h = x
    for i in range(g_mix.shape[0]):
        h = _layer(h, p[i], positions, *[a[i] for a in params])
    return h
```

```python
import functools

import jax
import jax.numpy as jnp
from jax import lax
from jax.experimental import pallas as pl
from jax.experimental.pallas import tpu as pltpu

F32 = jnp.float32
BF16 = jnp.bfloat16

D_MODEL = 2048
PLE_DIM = 256
RMS_EPS = 1e-6

MLA_HEADS = 8
MLA_NOPE = 128
MLA_ROPE = 64
MLA_QK = MLA_NOPE + MLA_ROPE
MLA_V = 128
MLA_Q_LORA = 512
MLA_KV_LORA = 256
ROPE_THETA = 10000.0
MLA_COLS = MLA_Q_LORA + MLA_KV_LORA + MLA_ROPE
MLA_COLS_PAD = 896

RW_HEADS = 16
RW_HEAD = 64
RW_C = RW_HEADS * RW_HEAD
RW_DECAY_LORA = 64
RW_A_LORA = 64
RW_GATE_LORA = 160
RW_GN_EPS = 64e-5
RW_LORA_PAD = 512
RW_COLS_PAD = 3 * RW_C + RW_LORA_PAD
RW_CHUNK = 64

N_EXPERTS = 64
TOP_K = 8
N_GROUPS = 8
TOPK_GROUPS = 4
EXPERT_DIM = 512
ROUTED_SCALE = 2.5
MOE_TILE = 512
ROW_SUB = D_MODEL // 2 // 128

VMEM_LIMIT = 56 * 1024 * 1024


def _cparams(n_axes):
    return pltpu.CompilerParams(dimension_semantics=("arbitrary",) * n_axes,
                                vmem_limit_bytes=VMEM_LIMIT)


def _rms(x, g):
    ms = jnp.mean(x * x, axis=-1, keepdims=True)
    return x * lax.rsqrt(ms + RMS_EPS) * g


def _dot(a, b):
    return jnp.dot(a, b, preferred_element_type=F32)


def _dot_nt(a, b):
    return lax.dot_general(a, b, (((1,), (1,)), ((), ())), preferred_element_type=F32)


def _dot_tn(a, b):
    return lax.dot_general(a, b, (((0,), (0,)), ((), ())), preferred_element_type=F32)


def _split2(x):
    hi = x.astype(BF16)
    lo = (x - hi.astype(F32)).astype(BF16)
    return hi, lo


def _in_proj_kernel(x_ref, g_ref, w_ref, pm_ref, z_ref, xn_ref):
    j = pl.program_id(1)

    @pl.when(j == 0)
    def _():
        xn_ref[...] = _rms(x_ref[...], g_ref[...]).astype(BF16)

    acc = _dot(xn_ref[...], w_ref[...])

    @pl.when(j == 0)
    def _():
        pm_ref[...] = acc.astype(pm_ref.dtype)

    @pl.when(j > 0)
    def _():
        z_ref[...] = acc.astype(z_ref.dtype)


def _in_proj(x, g, w, tm):
    m, k = x.shape
    tn = MLA_COLS_PAD
    nt = w.shape[1] // tn
    tm = min(tm, m)
    return pl.pallas_call(
        _in_proj_kernel,
        grid=(m // tm, nt),
        in_specs=[pl.BlockSpec((tm, k), lambda i, j: (i, 0)),
                  pl.BlockSpec((1, k), lambda i, j: (0, 0)),
                  pl.BlockSpec((k, tn), lambda i, j: (0, j))],
        out_specs=[pl.BlockSpec((tm, tn), lambda i, j: (i, 0)),
                   pl.BlockSpec((tm, tn), lambda i, j: (i, jnp.maximum(j - 1, 0)))],
        out_shape=[jax.ShapeDtypeStruct((m, tn), BF16),
                   jax.ShapeDtypeStruct((m, (nt - 1) * tn), BF16)],
        scratch_shapes=[pltpu.VMEM((tm, k), BF16)],
        compiler_params=_cparams(2),
        name="in_proj",
    )(x, g, w)


def _mla_prep_kernel(pm_ref, pos_ref, invf_ref, sgn_ref, gqa_ref, wuq_ref, gkva_ref, wukv_ref,
                     gqn_ref, gkn_ref, q_ref, k_ref, v_ref):
    pm = pm_ref[0].astype(F32)
    ang = pos_ref[0].astype(F32) * invf_ref[...]
    cos = jnp.cos(ang)
    sin = jnp.sin(ang) * sgn_ref[...]

    def rope(xr):
        half = MLA_ROPE // 2
        swapped = jnp.concatenate([xr[:, half:], xr[:, :half]], axis=1)
        return xr * cos + swapped * sin

    scale = MLA_QK ** -0.5 * 1.4426950408889634
    gqn = gqn_ref[...]
    gkn = gkn_ref[...]

    cq = _rms(pm[:, :MLA_Q_LORA], gqa_ref[...]).astype(BF16)
    q = _dot(cq, wuq_ref[...])
    ckv = _rms(pm[:, MLA_Q_LORA:MLA_Q_LORA + MLA_KV_LORA], gkva_ref[...]).astype(BF16)
    kv = _dot(ckv, wukv_ref[...])
    kpe = pm[:, MLA_Q_LORA + MLA_KV_LORA:MLA_COLS]
    kpe_ssq = jnp.sum(kpe * kpe, axis=-1, keepdims=True)
    kpe_rot = rope(kpe * gkn[:, MLA_NOPE:])

    for h in range(MLA_HEADS):
        qn = q[:, h * MLA_NOPE:(h + 1) * MLA_NOPE]
        qr = q[:, MLA_HEADS * MLA_NOPE + h * MLA_ROPE:MLA_HEADS * MLA_NOPE + (h + 1) * MLA_ROPE]
        ssq = jnp.sum(qn * qn, axis=-1, keepdims=True) + jnp.sum(qr * qr, axis=-1, keepdims=True)
        inv = lax.rsqrt(ssq * (1.0 / MLA_QK) + RMS_EPS) * scale
        q_ref[0, h, :, 0:MLA_NOPE] = (qn * inv * gqn[:, :MLA_NOPE]).astype(BF16)
        q_ref[0, h, :, MLA_NOPE:MLA_QK] = (rope(qr * gqn[:, MLA_NOPE:]) * inv).astype(BF16)

        kn = kv[:, h * 256:h * 256 + MLA_NOPE]
        ssq = jnp.sum(kn * kn, axis=-1, keepdims=True) + kpe_ssq
        inv = lax.rsqrt(ssq * (1.0 / MLA_QK) + RMS_EPS)
        k_ref[0, h, :, 0:MLA_NOPE] = (kn * inv * gkn[:, :MLA_NOPE]).astype(BF16)
        k_ref[0, h, :, MLA_NOPE:MLA_QK] = (kpe_rot * inv).astype(BF16)
        v_ref[0, h, :, 0:MLA_V] = kv[:, h * 256 + MLA_NOPE:(h + 1) * 256].astype(BF16)
        v_ref[0, h, :, MLA_V:] = jnp.ones((kv.shape[0], MLA_V), BF16)


def _mla_prep(pm, pos, invf, sgn, g_qa, w_uq, g_kva, w_ukv, g_qn, g_kn, tm):
    b, t, _ = pm.shape
    tm = min(tm, t)
    full = lambda a: pl.BlockSpec(a.shape, lambda i, j: (0,) * a.ndim)
    qk_shape = jax.ShapeDtypeStruct((b, MLA_HEADS, t, MLA_QK), BF16)
    return pl.pallas_call(
        _mla_prep_kernel,
        grid=(b, t // tm),
        in_specs=[pl.BlockSpec((1, tm, MLA_COLS_PAD), lambda i, j: (i, j, 0)),
                  pl.BlockSpec((1, tm, 1), lambda i, j: (i, j, 0)),
                  full(invf), full(sgn), full(g_qa), full(w_uq), full(g_kva), full(w_ukv),
                  full(g_qn), full(g_kn)],
        out_specs=[pl.BlockSpec((1, MLA_HEADS, tm, MLA_QK), lambda i, j: (i, 0, j, 0)),
                   pl.BlockSpec((1, MLA_HEADS, tm, MLA_QK), lambda i, j: (i, 0, j, 0)),
                   pl.BlockSpec((1, MLA_HEADS, tm, 2 * MLA_V), lambda i, j: (i, 0, j, 0))],
        out_shape=[qk_shape, qk_shape, jax.ShapeDtypeStruct((b, MLA_HEADS, t, 2 * MLA_V), BF16)],
        compiler_params=_cparams(2),
        name="mla_prep",
    )(pm, pos, invf, sgn, g_qa, w_uq, g_kva, w_ukv, g_qn, g_kn)


def _attn_kernel(q_ref, k_ref, v_ref, o_ref, m_ref, acc_ref, *, tq, tk, nh):
    qi = pl.program_id(2)
    m_ref[...] = jnp.full(m_ref.shape, -jnp.inf, F32)
    acc_ref[...] = jnp.zeros(acc_ref.shape, F32)
    heads = range(nh)
    qs = [q_ref[0, h] for h in heads]

    def step(j, masked):
        start = pl.multiple_of(j * tk, tk)
        ss = [_dot_nt(qs[h], k_ref[0, h, pl.ds(start, tk), :]) for h in heads]
        for h in heads:
            s = ss[h]
            if masked:
                row = qi * tq + lax.broadcasted_iota(jnp.int32, (tq, tk), 0)
                col = j * tk + lax.broadcasted_iota(jnp.int32, (tq, tk), 1)
                s = jnp.where(col <= row, s, -jnp.inf)
            m_old = m_ref[h]
            m_new = jnp.maximum(m_old, jnp.max(s, axis=-1, keepdims=True))
            alpha = jnp.exp2(m_old - m_new)
            p = jnp.exp2(s - jnp.concatenate([m_new] * (tk // 128), axis=1))
            pv = _dot(p.astype(BF16), v_ref[0, h, pl.ds(start, tk), :])
            acc_ref[h] = jnp.concatenate([alpha, alpha], axis=1) * acc_ref[h] + pv
            m_ref[h] = m_new

    per = tq // tk
    n_full = qi * per

    def body(jj, c):
        for d in range(per):
            step(jj * per + d, False)
        return c

    lax.fori_loop(0, qi, body, 0)
    for d in range(tq // tk):
        step(n_full + d, True)
    for h in heads:
        acc = acc_ref[h]
        o_ref[0, :, h * MLA_V:(h + 1) * MLA_V] = (acc[:, :MLA_V] / acc[:, MLA_V:]).astype(o_ref.dtype)


def _attention(q, k, v, tq, tk, nh):
    b, h, t, _ = q.shape
    tq = min(tq, t)
    tk = min(tk, tq)
    return pl.pallas_call(
        functools.partial(_attn_kernel, tq=tq, tk=tk, nh=nh),
        grid=(b, h // nh, t // tq),
        in_specs=[pl.BlockSpec((1, nh, tq, MLA_QK), lambda i, j, n: (i, j, n, 0)),
                  pl.BlockSpec((1, nh, t, MLA_QK), lambda i, j, n: (i, j, 0, 0)),
                  pl.BlockSpec((1, nh, t, 2 * MLA_V), lambda i, j, n: (i, j, 0, 0))],
        out_specs=pl.BlockSpec((1, tq, nh * MLA_V), lambda i, j, n: (i, n, j)),
        out_shape=jax.ShapeDtypeStruct((b, t, h * MLA_V), BF16),
        scratch_shapes=[pltpu.VMEM((nh, tq, 128), F32), pltpu.VMEM((nh, tq, 2 * MLA_V), F32)],
        compiler_params=_cparams(3),
        name="mla_attention",
    )(q, k, v)


def _rw_prep_kernel(z_ref, zp_ref, mu_ref, w0_ref, ww2_ref, a0_ref, aw2_ref, gw2_ref, kk_ref,
                    ka_ref, bd_ref, r_o, w_o, k_o, v_o, kk_o, b_o, g_o):
    z = z_ref[0].astype(F32)
    prev = zp_ref[0][15:16, :].astype(F32)
    prev = jnp.where(pl.program_id(1) == 0, jnp.zeros_like(prev), prev)
    row = lax.broadcasted_iota(jnp.int32, z.shape, 0)
    zs = jnp.where(row == 0, prev, pltpu.roll(z, 1, axis=0))
    z = z + (zs - z) * mu_ref[...]

    c = RW_C
    r = z[:, 0:c]
    k = z[:, c:2 * c]
    v = z[:, 2 * c:3 * c]
    wl = z[:, 3 * c:3 * c + 128]
    al = z[:, 3 * c + 128:3 * c + 256]
    gl = z[:, 3 * c + 256:3 * c + 512]

    f = w0_ref[...] + _dot(jnp.tanh(wl).astype(BF16), ww2_ref[...])
    softplus_neg_f = jnp.maximum(-f, 0.0) + jnp.log(1.0 + jnp.exp(-jnp.abs(f)))
    w = -softplus_neg_f - 0.5
    w_o[0] = -jnp.exp(w)
    a = jax.nn.sigmoid(a0_ref[...] + _dot(al.astype(BF16), aw2_ref[...]))
    g_o[0] = _dot(jax.nn.sigmoid(gl).astype(BF16), gw2_ref[...]).astype(g_o.dtype)

    kk = k * kk_ref[...]
    hi, lo = _split2(kk * kk)
    ssq = _dot(hi, bd_ref[...]) + _dot(lo, bd_ref[...])
    kk = kk / jnp.maximum(jnp.sqrt(ssq), 1e-12)
    r_o[0] = r.astype(r_o.dtype)
    k_o[0] = (k * (1.0 + (a - 1.0) * ka_ref[...])).astype(k_o.dtype)
    v_o[0] = v.astype(v_o.dtype)
    kk_o[0] = kk.astype(kk_o.dtype)
    b_o[0] = (kk * a).astype(b_o.dtype)


def _rw_prep(z, mu, w0, ww2, a0, aw2, gw2, k_k, k_a, bd, tm):
    b, t, cols = z.shape
    tm = min(tm, t)
    full = lambda a: pl.BlockSpec(a.shape, lambda i, j: (0,) * a.ndim)
    out = [jax.ShapeDtypeStruct((b, t, RW_C), F32 if i == 1 else BF16) for i in range(7)]
    ospec = pl.BlockSpec((1, tm, RW_C), lambda i, j: (i, j, 0))
    return pl.pallas_call(
        _rw_prep_kernel,
        grid=(b, t // tm),
        in_specs=[pl.BlockSpec((1, tm, cols), lambda i, j: (i, j, 0)),
                  pl.BlockSpec((1, 16, cols), lambda i, j: (i, jnp.maximum(j * (tm // 16) - 1, 0), 0)),
                  full(mu), full(w0), full(ww2), full(a0), full(aw2), full(gw2), full(k_k),
                  full(k_a), full(bd)],
        out_specs=[ospec] * 7,
        out_shape=out,
        compiler_params=_cparams(2),
        name="rwkv_prep",
    )(z, z, mu, w0, ww2, a0, aw2, gw2, k_k, k_a, bd)


def _rw_chunk_kernel(r_ref, w_ref, k_ref, v_ref, kk_ref, b_ref, g_ref, rk_ref, lnw_ref, lnb_ref,
                     o_ref, s_ref):
    cs = RW_CHUNK
    hd = RW_HEAD

    @pl.when(pl.program_id(1) == 0)
    def _():
        s_ref[...] = jnp.zeros(s_ref.shape, F32)

    r = r_ref[0].astype(F32)
    w = w_ref[0]
    k = k_ref[0].astype(F32)
    v = v_ref[0].astype(F32)
    kk = kk_ref[0].astype(F32)
    b = b_ref[0].astype(F32)
    pairs = range(RW_HEADS // 2)
    pw = 2 * hd

    row = lax.broadcasted_iota(jnp.int32, (cs, pw), 0)
    col = lax.broadcasted_iota(jnp.int32, (cs, pw), 1)
    colh = jnp.bitwise_and(col, hd - 1)
    lo = col < hd
    strict = row > colh
    eye = jnp.where(row == colh, 1.0, 0.0).astype(F32)
    row2 = lax.broadcasted_iota(jnp.int32, (cs, 2 * pw), 0)
    col2 = lax.broadcasted_iota(jnp.int32, (cs, 2 * pw), 1)
    incl2 = row2 >= jnp.bitwise_and(col2, hd - 1)
    row3 = lax.broadcasted_iota(jnp.int32, (cs, 3 * cs), 0)
    col3 = lax.broadcasted_iota(jnp.int32, (cs, 3 * cs), 1)
    tri3 = jnp.where(row3 >= jnp.bitwise_and(col3, cs - 1), 1.0, 0.0).astype(BF16)

    def bdiag(x):
        zero = jnp.zeros_like(x)
        return jnp.concatenate([jnp.where(lo, x, zero), jnp.where(lo, zero, x)], axis=0)

    def cat3_lhs(x):
        hi = x.astype(BF16)
        return jnp.concatenate([hi, (x - hi.astype(F32)).astype(BF16), hi], axis=1)

    def cat3_rhs(x):
        hi, lw = _split2(x)
        bh = bdiag(hi)
        return jnp.concatenate([bh, bh, bdiag(lw)], axis=0)

    w1 = w.astype(BF16)
    w2 = (w - w1.astype(F32)).astype(BF16)
    w3 = (w - w1.astype(F32) - w2.astype(F32)).astype(BF16)
    logp = _dot(tri3, jnp.concatenate([w1, w2, w3], axis=0))
    logp_end = logp[cs - 1:cs, :]
    p_tail = jnp.exp(logp_end - logp)
    p_end = jnp.exp(logp_end)
    p_inv = jnp.exp(-logp)

    a_t = -kk * jnp.exp(logp - w)
    r_t = r * jnp.exp(logp)
    b_t = b * p_inv
    k_t = k * p_inv
    b_h = b * p_tail
    k_h = k * p_tail

    sls = [slice(p * pw, (p + 1) * pw) for p in pairs]
    lhs = [jnp.concatenate([a_t[:, sl], r_t[:, sl]], axis=0).astype(BF16) for sl in sls]
    rhs = [jnp.concatenate([bdiag(b_t[:, sl].astype(BF16)), bdiag(k_t[:, sl].astype(BF16))], axis=0)
           for sl in sls]
    gm = [_dot_nt(lhs[p], rhs[p]) for p in pairs]
    s0 = [s_ref[p] for p in pairs]
    ls = [_dot_nt(lhs[p], bdiag(s0[p].astype(BF16))) for p in pairs]
    vb = [v[:, sl].astype(BF16) for sl in sls]
    bv = [bdiag(vb[p]) for p in pairs]
    a_ak = [jnp.where(strict, gm[p][:cs, pw:], 0.0).astype(BF16) for p in pairs]
    rhs_u = [ls[p][:cs] + _dot(a_ak[p], bv[p]) for p in pairs]

    x = [jnp.where(strict, gm[p][:cs, :pw], 0.0) for p in pairs]
    tm = [eye + x[p] for p in pairs]
    x = [_dot(cat3_lhs(x[p]), cat3_rhs(x[p])) for p in pairs]
    for _ in range(4):
        both = [_dot(cat3_lhs(jnp.concatenate([x[p], tm[p]], axis=0)), cat3_rhs(x[p])) for p in pairs]
        x = [both[p][:cs] for p in pairs]
        tm = [tm[p] + both[p][cs:] for p in pairs]
    tm = [tm[p] + _dot(cat3_lhs(tm[p]), cat3_rhs(x[p])) for p in pairs]

    u = [_dot(cat3_lhs(tm[p]), cat3_rhs(rhs_u[p])) for p in pairs]
    ub = [u[p].astype(BF16) for p in pairs]
    a_r = [jnp.where(incl2, gm[p][cs:, :], 0.0).astype(BF16) for p in pairs]
    y = [ls[p][cs:] + _dot(a_r[p], jnp.concatenate([bdiag(ub[p]), bv[p]], axis=0)) for p in pairs]
    for p in pairs:
        uv = jnp.concatenate([ub[p], vb[p]], axis=0)
        bk = jnp.concatenate([b_h[:, sls[p]], k_h[:, sls[p]]], axis=0).astype(BF16)
        full = _dot_tn(uv, bk)
        s_ref[p] = s0[p] * p_end[:, sls[p]] + jnp.where(lo, full[:hd], full[hd:])

    def head_sum(t):
        s_lo = jnp.sum(jnp.where(lo, t, 0.0), axis=-1, keepdims=True)
        s_all = jnp.sum(t, axis=-1, keepdims=True)
        return jnp.where(lo, s_lo, s_all - s_lo)

    rk_all = r * k * rk_ref[...]
    g = g_ref[0].astype(F32)
    for p in pairs:
        sl = sls[p]
        yc = y[p] - head_sum(y[p]) * (1.0 / hd)
        var = head_sum(yc * yc) * (1.0 / hd)
        yn = yc * lax.rsqrt(var + RW_GN_EPS) * lnw_ref[:, sl] + lnb_ref[:, sl]
        bonus = head_sum(rk_all[:, sl]) * v[:, sl]
        o_ref[0, :, sl] = ((yn + bonus) * g[:, sl]).astype(o_ref.dtype)


def _rw_chunk(r, w, k, v, kk, bb, g, r_k, ln_w, ln_b):
    b, t, c = r.shape
    cs = RW_CHUNK
    full = lambda a: pl.BlockSpec(a.shape, lambda i, j: (0,) * a.ndim)
    spec = pl.BlockSpec((1, cs, c), lambda i, j: (i, j, 0))
    return pl.pallas_call(
        _rw_chunk_kernel,
        grid=(b, t // cs),
        in_specs=[spec] * 7 + [full(r_k), full(ln_w), full(ln_b)],
        out_specs=spec,
        out_shape=jax.ShapeDtypeStruct((b, t, c), BF16),
        scratch_shapes=[pltpu.VMEM((RW_HEADS // 2, RW_HEAD, 2 * RW_HEAD), F32)],
        compiler_params=_cparams(2),
        name="rwkv_chunk",
    )(r, w, k, v, kk, bb, g, r_k, ln_w, ln_b)


def _pack_pair(a, b):
    ua = lax.bitcast_convert_type(a.astype(BF16).astype(F32), jnp.uint32)
    ub = lax.bitcast_convert_type(b.astype(BF16).astype(F32), jnp.uint32)
    return lax.shift_right_logical(ua, jnp.uint32(16)) | (ub & jnp.uint32(0xFFFF0000))


def _unpack_pair(u):
    lo = lax.bitcast_convert_type(lax.shift_left(u, jnp.uint32(16)), F32)
    hi = lax.bitcast_convert_type(u & jnp.uint32(0xFFFF0000), F32)
    return lo, hi


def _store_rows(ref, packed):
    m = packed.shape[0]
    for s in range(ROW_SUB):
        ref[pl.ds(s, m, stride=ROW_SUB), :] = packed[:, s * 128:(s + 1) * 128]


def _load_rows(ref):
    m = ref.shape[-2] // ROW_SUB
    los, his = [], []
    for s in range(ROW_SUB):
        lo, hi = _unpack_pair(ref[pl.ds(s, m, stride=ROW_SUB), :])
        los.append(lo.astype(BF16))
        his.append(hi.astype(BF16))
    return jnp.concatenate(los, axis=1), jnp.concatenate(his, axis=1)


def _out_proj_kernel(ya_ref, yr_ref, x_ref, w_ref, g_ref, wrh_ref, wrl_ref, h_ref, hn_ref, lg_ref):
    half = ya_ref.shape[1]
    acc = _dot(ya_ref[...], w_ref[0:half, :]) + _dot(yr_ref[...], w_ref[half:, :])
    h = x_ref[...] + acc
    h_ref[...] = h
    hn = _rms(h, g_ref[...])
    d2 = hn.shape[1] // 2
    _store_rows(hn_ref, _pack_pair(hn[:, :d2], hn[:, d2:]))
    hh, hl = _split2(hn)
    lg_ref[...] = (_dot_nt(wrh_ref[...], hh) + _dot_nt(wrh_ref[...], hl)
                   + _dot_nt(wrl_ref[...], hh))


def _out_proj(ya, yr, x, w, g, wr, tm):
    wr_hi, wr_lo = _split2(wr.T)
    m, d = x.shape
    tm = min(tm, m)
    full = lambda a: pl.BlockSpec(a.shape, lambda i: (0,) * a.ndim)
    return pl.pallas_call(
        _out_proj_kernel,
        grid=(m // tm,),
        in_specs=[pl.BlockSpec((tm, ya.shape[1]), lambda i: (i, 0)),
                  pl.BlockSpec((tm, yr.shape[1]), lambda i: (i, 0)),
                  pl.BlockSpec((tm, d), lambda i: (i, 0)),
                  full(w), full(g), full(wr_hi), full(wr_lo)],
        out_specs=[pl.BlockSpec((tm, d), lambda i: (i, 0)),
                   pl.BlockSpec((tm * ROW_SUB, 128), lambda i: (i, 0)),
                   pl.BlockSpec((N_EXPERTS, tm), lambda i: (0, i))],
        out_shape=[jax.ShapeDtypeStruct((m, d), F32), jax.ShapeDtypeStruct((m * ROW_SUB, 128), jnp.uint32),
                   jax.ShapeDtypeStruct((N_EXPERTS, m), F32)],
        compiler_params=_cparams(1),
        name="out_proj_router_logits",
    )(ya, yr, x, w, g, wr_hi, wr_lo)


def _router_kernel(lg_ref, bias_ref, e_ref, r_ref, w_ref, cnt_ref, carry_ref):
    tm = lg_ref.shape[1]
    gsz = N_EXPERTS // N_GROUPS

    @pl.when(pl.program_id(0) == 0)
    def _():
        carry_ref[...] = jnp.zeros(carry_ref.shape, F32)

    scores = jax.nn.sigmoid(lg_ref[...]).reshape(N_GROUPS, gsz, tm)
    biased = scores + bias_ref[...].reshape(N_GROUPS, gsz, 1)
    neg = -jnp.inf

    eidx = lax.broadcasted_iota(jnp.int32, biased.shape, 1)
    m1 = jnp.max(biased, axis=1, keepdims=True)
    first = jnp.min(jnp.where(biased == m1, eidx, gsz), axis=1, keepdims=True)
    m2 = jnp.max(jnp.where(eidx == first, neg, biased), axis=1, keepdims=True)
    gscore = m1 + m2

    gidx = lax.broadcasted_iota(jnp.int32, gscore.shape, 0)
    gsel = jnp.zeros(gscore.shape, jnp.bool_)
    work = gscore
    for _ in range(TOPK_GROUPS):
        best = jnp.max(work, axis=0, keepdims=True)
        pick = jnp.min(jnp.where(work == best, gidx, N_GROUPS), axis=0, keepdims=True)
        hit = gidx == pick
        gsel = jnp.logical_or(gsel, hit)
        work = jnp.where(hit, neg, work)

    flat = lax.broadcasted_iota(jnp.int32, biased.shape, 0) * gsz + eidx
    work = jnp.where(gsel, biased, neg)
    hits = []
    picks = []
    for _ in range(TOP_K):
        best = jnp.max(jnp.max(work, axis=1, keepdims=True), axis=0, keepdims=True)
        cand = jnp.where(work == best, flat, N_EXPERTS)
        pick = jnp.min(jnp.min(cand, axis=1, keepdims=True), axis=0, keepdims=True)
        hit = flat == pick
        hits.append(hit)
        picks.append(pick)
        work = jnp.where(hit, neg, work)

    def pick_value(hit, val):
        s = jnp.sum(jnp.sum(jnp.where(hit, val, 0.0), axis=1, keepdims=True), axis=0, keepdims=True)
        return s.reshape(1, tm)

    sel = jnp.zeros(biased.shape, F32)
    for hit in hits:
        sel = jnp.where(hit, 1.0, sel)
    sel2 = sel.reshape(N_EXPERTS, tm)
    r_i = lax.broadcasted_iota(jnp.int32, (tm, tm), 0)
    c_i = lax.broadcasted_iota(jnp.int32, (tm, tm), 1)
    upper = jnp.where(r_i < c_i, 1.0, 0.0).astype(BF16)
    rank = _dot(sel2.astype(BF16), upper) + carry_ref[...]
    carry = carry_ref[...] + jnp.sum(sel2, axis=1, keepdims=True)
    carry_ref[...] = carry
    cnt_ref[...] = carry
    rank3 = rank.reshape(N_GROUPS, gsz, tm)

    raw = [pick_value(hit, scores) for hit in hits]
    tot = raw[0]
    for x in raw[1:]:
        tot = tot + x
    wrow = lax.broadcasted_iota(jnp.int32, (N_EXPERTS, tm), 0)
    wt = jnp.zeros((N_EXPERTS, tm), F32)
    for j in range(TOP_K):
        e_ref[j:j + 1, :] = picks[j].reshape(1, tm)
        r_ref[j:j + 1, :] = pick_value(hits[j], rank3).astype(jnp.int32)
        wt = jnp.where(wrow == j, raw[j] / tot * ROUTED_SCALE, wt)
    w_ref[...] = wt.T


def _router(lg_t, bias, tm):
    e, m = lg_t.shape
    tm = min(tm, m)
    row = pl.BlockSpec((TOP_K, tm), lambda i: (0, i))
    return pl.pallas_call(
        _router_kernel,
        grid=(m // tm,),
        in_specs=[pl.BlockSpec((e, tm), lambda i: (0, i)),
                  pl.BlockSpec((e, 1), lambda i: (0, 0))],
        out_specs=[row, row, pl.BlockSpec((tm, e), lambda i: (i, 0)),
                   pl.BlockSpec((e, 1), lambda i: (0, 0))],
        out_shape=[jax.ShapeDtypeStruct((TOP_K, m), jnp.int32),
                   jax.ShapeDtypeStruct((TOP_K, m), jnp.int32),
                   jax.ShapeDtypeStruct((m, e), F32),
                   jax.ShapeDtypeStruct((e, 1), F32)],
        scratch_shapes=[pltpu.VMEM((e, 1), F32)],
        compiler_params=_cparams(1),
        name="router_topk",
    )(lg_t, bias)


def _dest_kernel(off_ref, e_ref, r_ref, d_ref):
    e = e_ref[...]
    acc = r_ref[...]
    for x in range(N_EXPERTS):
        acc = acc + jnp.where(e == x, off_ref[x], 0)
    d_ref[...] = acc * ROW_SUB


def _dest(off, eidx, rank, tm):
    k, m = eidx.shape
    tm = min(tm, m)
    spec = pl.BlockSpec((k, tm), lambda i: (0, i))
    return pl.pallas_call(
        _dest_kernel,
        grid=(m // tm,),
        in_specs=[pl.BlockSpec(memory_space=pltpu.SMEM), spec, spec],
        out_specs=spec,
        out_shape=jax.ShapeDtypeStruct((k, m), jnp.int32),
        compiler_params=_cparams(1),
        name="moe_dest_rows",
    )(off, eidx, rank)


def _scatter_kernel(zt_ref, d_ref, x_ref, z_ref, xs_hbm, sem, *, tm, tile):
    i = pl.program_id(0)
    rs = ROW_SUB

    @pl.when(i == 0)
    def _():
        def zero(e, c):
            start = pl.multiple_of(zt_ref[e] * (tile * rs), tile * rs)
            pltpu.make_async_copy(z_ref, xs_hbm.at[pl.ds(start, tile * rs)], sem).start()
            return c

        lax.fori_loop(0, N_EXPERTS, zero, 0)

        def zwait(e, c):
            pltpu.make_async_copy(z_ref, xs_hbm.at[pl.ds(0, tile * rs)], sem).wait()
            return c

        lax.fori_loop(0, N_EXPERTS, zwait, 0)

    def body(n, c):
        src = x_ref.at[pl.ds(pl.multiple_of(n * rs, rs), rs)]
        for j in range(TOP_K):
            dst = xs_hbm.at[pl.ds(pl.multiple_of(d_ref[n * TOP_K + j], rs), rs)]
            pltpu.make_async_copy(src, dst, sem).start(priority=j % 2)
        return c

    lax.fori_loop(0, tm, body, 0)

    def wait(n, c):
        for j in range(TOP_K):
            pltpu.make_async_copy(x_ref.at[pl.ds(0, rs)], xs_hbm.at[pl.ds(0, rs)], sem).wait()
        return c

    lax.fori_loop(0, tm, wait, 0)


def _scatter_rows(zero_tile, dest, xp, rows, tile, tm):
    m = xp.shape[0] // ROW_SUB
    tm = min(tm, m)
    return pl.pallas_call(
        functools.partial(_scatter_kernel, tm=tm, tile=tile),
        grid_spec=pltpu.PrefetchScalarGridSpec(
            num_scalar_prefetch=1,
            grid=(m // tm,),
            in_specs=[pl.BlockSpec((tm * TOP_K,), lambda i, zt: (i,), memory_space=pltpu.SMEM),
                      pl.BlockSpec((tm * ROW_SUB, 128), lambda i, zt: (i, 0)),
                      pl.BlockSpec((tile * ROW_SUB, 128), lambda i, zt: (0, 0))],
            out_specs=pl.BlockSpec(memory_space=pl.ANY),
            scratch_shapes=[pltpu.SemaphoreType.DMA(())],
        ),
        out_shape=jax.ShapeDtypeStruct((rows * ROW_SUB, 128), jnp.uint32),
        compiler_params=_cparams(1),
        name="moe_scatter_rows",
    )(zero_tile, dest, xp, jnp.zeros((tile * ROW_SUB, 128), jnp.uint32))


def _expert_kernel(te_ref, nx_ref, gi_ref, nu_ref, x_ref, wg_hbm, wu_hbm, wd_hbm, o_ref,
                   wgf, wuf, wdf, wgb, wub, wdb, sem):
    t = pl.program_id(0)
    tc = jnp.minimum(t, nu_ref[0] - 1)
    prev = jnp.maximum(tc - 1, 0)
    new_expert = jnp.logical_or(t == 0, te_ref[tc] != te_ref[prev])
    slot = jnp.bitwise_and(gi_ref[tc], 1)

    def copies(e, s):
        return (pltpu.make_async_copy(wg_hbm.at[e], wgf.at[s], sem.at[s]),
                pltpu.make_async_copy(wu_hbm.at[e], wuf.at[s], sem.at[s]),
                pltpu.make_async_copy(wd_hbm.at[e], wdf.at[s], sem.at[s]))

    @pl.when(t == 0)
    def _():
        for c in copies(te_ref[0], 0):
            c.start()

    @pl.when(jnp.logical_and(new_expert, t < nu_ref[0]))
    def _():
        for c in copies(te_ref[tc], slot):
            c.wait()
        wgb[...] = wgf[slot].astype(BF16)
        wub[...] = wuf[slot].astype(BF16)
        wdb[...] = wdf[slot].astype(BF16)

        @pl.when(nx_ref[tc] >= 0)
        def _():
            for c in copies(nx_ref[tc], 1 - slot):
                c.start()

    @pl.when(t < nu_ref[0])
    def _():
        lo, hi = _load_rows(x_ref)
        d2 = lo.shape[1]
        gate = _dot(lo, wgb[0:d2, :]) + _dot(hi, wgb[d2:, :])
        up = _dot(lo, wub[0:d2, :]) + _dot(hi, wub[d2:, :])
        mid = (gate * jax.nn.sigmoid(gate) * up).astype(BF16)
        y = _dot(mid, wdb[...])
        _store_rows(o_ref, _pack_pair(y[:, :d2], y[:, d2:]))


def _experts(tile_expert, next_expert, group_idx, n_used, xs, wg, wu, wd, tile):
    rows = xs.shape[0] // ROW_SUB
    d = wg.shape[1]
    blk = (tile * ROW_SUB, 128)
    tile_map = lambda t, te, nx, gi, nu: (jnp.minimum(t, nu[0] - 1), 0)
    hbm = pl.BlockSpec(memory_space=pl.ANY)
    return pl.pallas_call(
        _expert_kernel,
        grid_spec=pltpu.PrefetchScalarGridSpec(
            num_scalar_prefetch=4,
            grid=(rows // tile,),
            in_specs=[pl.BlockSpec(blk, tile_map), hbm, hbm, hbm],
            out_specs=pl.BlockSpec(blk, tile_map),
            scratch_shapes=[pltpu.VMEM((2, d, EXPERT_DIM), F32), pltpu.VMEM((2, d, EXPERT_DIM), F32),
                            pltpu.VMEM((2, EXPERT_DIM, d), F32),
                            pltpu.VMEM((d, EXPERT_DIM), BF16), pltpu.VMEM((d, EXPERT_DIM), BF16),
                            pltpu.VMEM((EXPERT_DIM, d), BF16), pltpu.SemaphoreType.DMA((2,))],
        ),
        out_shape=jax.ShapeDtypeStruct(xs.shape, jnp.uint32),
        compiler_params=_cparams(1),
        name="moe_experts",
    )(tile_expert, next_expert, group_idx, n_used, xs, wg, wu, wd)


def _combine_kernel(d_ref, dn_ref, ys_hbm, w_ref, h_ref, xp_ref, sg_ref, su_ref, sd_ref, o_ref, buf, sem,
                    *, tm, nt):
    i = pl.program_id(0)
    slot = jnp.bitwise_and(i, 1)
    nxt = 1 - slot
    rs = ROW_SUB

    def start(idx_ref, n, s):
        for j in range(TOP_K):
            src = ys_hbm.at[pl.ds(pl.multiple_of(idx_ref[n * TOP_K + j], rs), rs)]
            pltpu.make_async_copy(src, buf.at[s, j, pl.ds(n * rs, rs)], sem.at[s]).start(priority=j % 2)

    def wait_all(s):
        for j in range(TOP_K):
            pltpu.make_async_copy(ys_hbm.at[pl.ds(0, tm * rs)], buf.at[s, j], sem.at[s]).wait()

    @pl.when(i == 0)
    def _():
        def body(n, c):
            for j in range(TOP_K):
                src = ys_hbm.at[pl.ds(pl.multiple_of(d_ref[n * TOP_K + j], rs), rs)]
                dst = buf.at[0, j, pl.ds(pl.multiple_of(n * rs, rs), rs)]
                pltpu.make_async_copy(src, dst, sem.at[0]).start(priority=j % 2)
            return c

        lax.fori_loop(0, tm, body, 0)

    per = tm // (rs * TOP_K)
    batches = iter(range(0, tm, per))

    def start_batch():
        n0 = next(batches)
        for n in range(n0, n0 + per):
            start(dn_ref, n, nxt)

    lo, hi = _load_rows(xp_ref)
    d2 = lo.shape[1]
    gate = _dot(lo, sg_ref[0:d2, :]) + _dot(hi, sg_ref[d2:, :])
    up = _dot(lo, su_ref[0:d2, :]) + _dot(hi, su_ref[d2:, :])
    mid = (gate * jax.nn.sigmoid(gate) * up).astype(BF16)
    o_ref[...] = h_ref[...] + _dot(mid, sd_ref[...])

    wait_all(slot)
    wts = w_ref[...]
    wj = [wts[:, j:j + 1] for j in range(TOP_K)]
    for s in range(rs):
        r_lo = jnp.zeros((tm, 128), F32)
        r_hi = jnp.zeros((tm, 128), F32)
        for j in range(TOP_K):
            start_batch()
            ylo, yhi = _unpack_pair(buf[slot, j, pl.ds(s, tm, stride=rs), :])
            r_lo = r_lo + wj[j] * ylo
            r_hi = r_hi + wj[j] * yhi
        o_ref[:, s * 128:(s + 1) * 128] += r_lo
        o_ref[:, d2 + s * 128:d2 + (s + 1) * 128] += r_hi

    @pl.when(i == nt - 1)
    def _():
        wait_all(nxt)


def _combine(dest, ys, wcol, h, xp, sg, su, sd, tm):
    m, d = h.shape
    tm = min(tm, m)
    nt = m // tm
    full = lambda a: pl.BlockSpec(a.shape, lambda i: (0,) * a.ndim)
    idx = lambda f: pl.BlockSpec((tm * TOP_K,), f, memory_space=pltpu.SMEM)
    return pl.pallas_call(
        functools.partial(_combine_kernel, tm=tm, nt=nt),
        grid=(nt,),
        in_specs=[idx(lambda i: (i,)), idx(lambda i: (jnp.minimum(i + 1, nt - 1),)),
                  pl.BlockSpec(memory_space=pl.ANY),
                  pl.BlockSpec((tm, N_EXPERTS), lambda i: (i, 0)),
                  pl.BlockSpec((tm, d), lambda i: (i, 0)),
                  pl.BlockSpec((tm * ROW_SUB, 128), lambda i: (i, 0)),
                  full(sg), full(su), full(sd)],
        out_specs=pl.BlockSpec((tm, d), lambda i: (i, 0)),
        out_shape=jax.ShapeDtypeStruct((m, d), F32),
        scratch_shapes=[pltpu.VMEM((2, TOP_K, tm * ROW_SUB, 128), jnp.uint32),
                        pltpu.SemaphoreType.DMA((2,))],
        compiler_params=_cparams(1),
        name="moe_combine_shared",
    )(dest, dest, ys, wcol, h, xp, sg, su, sd)


def _moe(h1, hnp, lg_t, router_bias, w_gate, w_up, w_down, ws_gate, ws_up, ws_down):
    n = h1.shape[0]
    tile = MOE_TILE
    eidx, rank, wcol, cnt = _router(lg_t, router_bias.reshape(-1, 1), 512)
    counts = cnt[:, 0].astype(jnp.int32)
    tiles_per = (counts + tile - 1) // tile
    tile_end = jnp.cumsum(tiles_per)
    tile_start = tile_end - tiles_per
    n_tiles = (n * TOP_K) // tile + N_EXPERTS
    rows = n_tiles * tile
    tile_expert = jnp.minimum(jnp.sum(tile_end[None, :] <= jnp.arange(n_tiles)[:, None], axis=1),
                              N_EXPERTS - 1).astype(jnp.int32)
    n_used = tile_end[-1:].astype(jnp.int32)
    ids = jnp.arange(N_EXPERTS)
    later = jnp.logical_and(ids[None, :] > ids[:, None], tiles_per[None, :] > 0)
    nxt_of = jnp.min(jnp.where(later, ids[None, :], N_EXPERTS), axis=1)
    nxt_of = jnp.where(nxt_of == N_EXPERTS, -1, nxt_of)
    ord_of = jnp.cumsum((tiles_per > 0).astype(jnp.int32)) - 1
    onehot = tile_expert[:, None] == ids[None, :]
    next_expert = jnp.sum(jnp.where(onehot, nxt_of[None, :], 0), axis=1).astype(jnp.int32)
    group_idx = jnp.sum(jnp.where(onehot, ord_of[None, :], 0), axis=1).astype(jnp.int32)
    last_tile = jnp.clip(tile_end - 1, 0, n_tiles - 1).astype(jnp.int32)
    dest = _dest((tile_start * tile).astype(jnp.int32), eidx, rank, 2048)
    dest = dest.T.reshape(-1)
    xs = _scatter_rows(last_tile, dest, hnp, rows, tile, 512)
    ys = _experts(tile_expert, next_expert, group_idx, n_used, xs, w_gate, w_up, w_down, tile)
    return _combine(dest, ys, wcol, h1, hnp, ws_gate.astype(BF16), ws_up.astype(BF16),
                    ws_down.astype(BF16), 256)


def _ple_kernel(h_ref, p_ref, gin_ref, wg_ref, bg_ref, wp_ref, gout_ref, o_ref):
    h = h_ref[...]
    gate = jax.nn.sigmoid(_dot(_rms(h, gin_ref[...]).astype(BF16), wg_ref[...]) + bg_ref[...])
    pp = _dot(p_ref[...].astype(BF16), wp_ref[...])
    o_ref[...] = h + _rms(pp * gate, gout_ref[...])


def _ple(h, p, g_in, w_g, b_g, w_p, g_out, tm):
    m, d = h.shape
    tm = min(tm, m)
    full = lambda a: pl.BlockSpec(a.shape, lambda i: (0,) * a.ndim)
    return pl.pallas_call(
        _ple_kernel,
        grid=(m // tm,),
        in_specs=[pl.BlockSpec((tm, d), lambda i: (i, 0)),
                  pl.BlockSpec((tm, p.shape[1]), lambda i: (i, 0)),
                  full(g_in), full(w_g), full(b_g), full(w_p), full(g_out)],
        out_specs=pl.BlockSpec((tm, d), lambda i: (i, 0)),
        out_shape=jax.ShapeDtypeStruct((m, d), F32),
        compiler_params=_cparams(1),
        name="ple",
    )(h, p, g_in, w_g, b_g, w_p, g_out)


def _pad_cols(a, width):
    return jnp.pad(a, ((0, 0), (0, width - a.shape[1])))


def _pad_rows(a, rows):
    return jnp.pad(a, ((0, rows - a.shape[0]), (0, 0)))


def _row(a):
    return a.reshape(1, -1)


def _rw_layout(a):
    c3 = 3 * RW_C
    parts = [a[..., :c3]]
    off = c3
    for width, padded in ((RW_DECAY_LORA, 128), (RW_A_LORA, 128), (RW_GATE_LORA, 256)):
        pad = [(0, 0)] * (a.ndim - 1) + [(0, padded - width)]
        parts.append(jnp.pad(a[..., off:off + width], pad))
        off += width
    return jnp.concatenate(parts, axis=-1)


def _mla_from_pm(pm, positions, g_qa, w_uq, g_kva, w_ukv, g_qn, g_kn):
    b, t, _ = pm.shape
    wq = w_uq.reshape(MLA_Q_LORA, MLA_HEADS, MLA_QK)
    wq = jnp.concatenate([wq[:, :, :MLA_NOPE].reshape(MLA_Q_LORA, -1),
                          wq[:, :, MLA_NOPE:].reshape(MLA_Q_LORA, -1)], axis=1).astype(BF16)
    half = MLA_ROPE // 2
    inv = ROPE_THETA ** (-jnp.arange(half, dtype=F32) / half)
    invf = _row(jnp.concatenate([inv, inv]))
    sgn = _row(jnp.concatenate([-jnp.ones(half, F32), jnp.ones(half, F32)]))
    q, k, v = _mla_prep(pm, positions.reshape(b, t, 1), invf, sgn, _row(g_qa), wq, _row(g_kva),
                        w_ukv.astype(BF16), _row(g_qn), _row(g_kn), 512)
    return _attention(q, k, v, 1024, 512, 2)


def _rwkv_from_z(z, mu, w0, w_w2, a0, a_w2, g_w2, k_k, k_a, r_k, ln_w, ln_b):
    head_of = jnp.arange(RW_C) // RW_HEAD
    bd = (head_of[:, None] == head_of[None, :]).astype(BF16)
    rr, ww, kx, vx, kkx, bx, gx = _rw_prep(
        z, _row(_rw_layout(mu)), _row(w0), _pad_rows(w_w2, 128).astype(BF16), _row(a0),
        _pad_rows(a_w2, 128).astype(BF16), _pad_rows(g_w2, 256).astype(BF16), _row(k_k), _row(k_a),
        bd, 256)
    return _rw_chunk(rr, ww, kx, vx, kkx, bx, gx, _row(r_k), _row(ln_w), _row(ln_b))


def _layer(h, p, positions, g_mix, w_in, mla_g_qa, mla_w_uq, mla_g_kva, mla_w_ukv, mla_g_qn,
           mla_g_kn, rw_mu, rw_w0, rw_w_w2, rw_a0, rw_a_w2, rw_g_w2, rw_k_k, rw_k_a, rw_r_k,
           rw_ln_w, rw_ln_b, w_out, g_ffn, w_router, router_bias, w_gate, w_up, w_down, ws_gate,
           ws_up, ws_down, g_ple_in, w_ple_gate, b_ple_gate, w_ple_proj, g_ple_out):
    b, t, d = h.shape
    n = b * t
    x2 = h.reshape(n, d)

    w_mla = _pad_cols(w_in[:, :MLA_COLS], MLA_COLS_PAD).astype(BF16)
    w_rw = _rw_layout(w_in[:, MLA_COLS:]).astype(BF16)
    pm, z = _in_proj(x2, _row(g_mix), jnp.concatenate([w_mla, w_rw], axis=1), 1024)
    y_mla = _mla_from_pm(pm.reshape(b, t, -1), positions, mla_g_qa, mla_w_uq, mla_g_kva, mla_w_ukv,
                         mla_g_qn, mla_g_kn)
    y_rw = _rwkv_from_z(z.reshape(b, t, -1), rw_mu, rw_w0, rw_w_w2, rw_a0, rw_a_w2, rw_g_w2, rw_k_k,
                        rw_k_a, rw_r_k, rw_ln_w, rw_ln_b)

    h1, hnp, lg_t = _out_proj(y_mla.reshape(n, -1), y_rw.reshape(n, -1), x2, w_out.astype(BF16),
                              _row(g_ffn), w_router, 512)
    h2 = _moe(h1, hnp, lg_t, router_bias, w_gate, w_up, w_down, ws_gate, ws_up, ws_down)

    h3 = _ple(h2, p.reshape(n, -1), _row(g_ple_in), w_ple_gate.astype(BF16), _row(b_ple_gate),
              w_ple_proj.astype(BF16), _row(g_ple_out), 512)
    return h3.reshape(b, t, d)


def kernel(x, p, positions, g_mix, w_in, mla_g_qa, mla_w_uq, mla_g_kva, mla_w_ukv, mla_g_qn, mla_g_kn, rw_mu, rw_w0, rw_w_w2, rw_a0, rw_a_w2, rw_g_w2, rw_k_k, rw_k_a, rw_r_k, rw_ln_w, rw_ln_b, w_out, g_ffn, w_router, router_bias, w_gate, w_up, w_down, ws_gate, ws_up, ws_down, g_ple_in, w_ple_gate, b_ple_gate, w_ple_proj, g_ple_out):
    params = (g_mix, w_in, mla_g_qa, mla_w_uq, mla_g_kva, mla_w_ukv, mla_g_qn, mla_g_kn, rw_mu,
              rw_w0, rw_w_w2, rw_a0, rw_a_w2, rw_g_w2, rw_k_k, rw_k_a, rw_r_k, rw_ln_w, rw_ln_b,
              w_out, g_ffn, w_router, router_bias, w_gate, w_up, w_down, ws_gate, ws_up, ws_down,
              g_ple_in, w_ple_gate, b_ple_gate, w_ple_proj, g_ple_out)
    h = x
    for i in range(g_mix.shape[0]):
        h = _layer(h, p[i], positions, *[a[i] for a in params])
    return h
```

```python
import functools

import jax
import jax.numpy as jnp
from jax import lax
from jax.experimental import pallas as pl
from jax.experimental.pallas import tpu as pltpu

F32 = jnp.float32
BF16 = jnp.bfloat16

D_MODEL = 2048
PLE_DIM = 256
RMS_EPS = 1e-6

MLA_HEADS = 8
MLA_NOPE = 128
MLA_ROPE = 64
MLA_QK = MLA_NOPE + MLA_ROPE
MLA_V = 128
MLA_Q_LORA = 512
MLA_KV_LORA = 256
ROPE_THETA = 10000.0
MLA_COLS = MLA_Q_LORA + MLA_KV_LORA + MLA_ROPE
MLA_COLS_PAD = 1024

RW_HEADS = 16
RW_HEAD = 64
RW_C = RW_HEADS * RW_HEAD
RW_DECAY_LORA = 64
RW_A_LORA = 64
RW_GATE_LORA = 160
RW_GN_EPS = 64e-5
RW_LORA_PAD = 512
PROJ_COLS = MLA_COLS_PAD + 3 * RW_C + RW_LORA_PAD
PROJ_TN = 1536
RW_CHUNK = 64

N_EXPERTS = 64
TOP_K = 8
N_GROUPS = 8
TOPK_GROUPS = 4
EXPERT_DIM = 512
ROUTED_SCALE = 2.5
MOE_TILE = 512
ROW_SUB = D_MODEL // 2 // 128

VMEM_LIMIT = 56 * 1024 * 1024


def _cparams(n_axes):
    return pltpu.CompilerParams(dimension_semantics=("arbitrary",) * n_axes,
                                vmem_limit_bytes=VMEM_LIMIT)


def _rms(x, g):
    ms = jnp.mean(x * x, axis=-1, keepdims=True)
    return x * lax.rsqrt(ms + RMS_EPS) * g


def _dot(a, b):
    return jnp.dot(a, b, preferred_element_type=F32)


def _dot_nt(a, b):
    return lax.dot_general(a, b, (((1,), (1,)), ((), ())), preferred_element_type=F32)


def _dot_tn(a, b):
    return lax.dot_general(a, b, (((0,), (0,)), ((), ())), preferred_element_type=F32)


def _split2(x):
    hi = x.astype(BF16)
    lo = (x - hi.astype(F32)).astype(BF16)
    return hi, lo


def _in_proj_kernel(x_ref, g_ref, w_ref, o_ref, xn_ref):
    @pl.when(pl.program_id(1) == 0)
    def _():
        xn_ref[...] = _rms(x_ref[...], g_ref[...]).astype(BF16)

    o_ref[...] = _dot(xn_ref[...], w_ref[...]).astype(o_ref.dtype)


def _in_proj(x, g, w, tm):
    m, k = x.shape
    n = w.shape[1]
    tn = PROJ_TN
    tm = min(tm, m)
    return pl.pallas_call(
        _in_proj_kernel,
        grid=(m // tm, n // tn),
        in_specs=[pl.BlockSpec((tm, k), lambda i, j: (i, 0)),
                  pl.BlockSpec((1, k), lambda i, j: (0, 0)),
                  pl.BlockSpec((k, tn), lambda i, j: (0, j))],
        out_specs=pl.BlockSpec((tm, tn), lambda i, j: (i, j)),
        out_shape=jax.ShapeDtypeStruct((m, n), BF16),
        scratch_shapes=[pltpu.VMEM((tm, k), BF16)],
        compiler_params=_cparams(2),
        name="in_proj",
    )(x, g, w)


def _mla_prep_kernel(pm_ref, pos_ref, invf_ref, sgn_ref, gqa_ref, wuq_ref, gkva_ref, wukv_ref,
                     gqn_ref, gkn_ref, q_ref, k_ref, v_ref):
    pm = pm_ref[0].astype(F32)
    ang = pos_ref[0].astype(F32) * invf_ref[...]
    cos = jnp.cos(ang)
    sin = jnp.sin(ang) * sgn_ref[...]

    def rope(xr):
        half = MLA_ROPE // 2
        swapped = jnp.concatenate([xr[:, half:], xr[:, :half]], axis=1)
        return xr * cos + swapped * sin

    scale = MLA_QK ** -0.5 * 1.4426950408889634
    gqn = gqn_ref[...]
    gkn = gkn_ref[...]

    cq = _rms(pm[:, :MLA_Q_LORA], gqa_ref[...]).astype(BF16)
    q = _dot(cq, wuq_ref[...])
    ckv = _rms(pm[:, MLA_Q_LORA:MLA_Q_LORA + MLA_KV_LORA], gkva_ref[...]).astype(BF16)
    kv = _dot(ckv, wukv_ref[...])
    kpe = pm[:, MLA_Q_LORA + MLA_KV_LORA:MLA_COLS]
    kpe_ssq = jnp.sum(kpe * kpe, axis=-1, keepdims=True)
    kpe_rot = rope(kpe * gkn[:, MLA_NOPE:])

    for h in range(MLA_HEADS):
        qn = q[:, h * MLA_NOPE:(h + 1) * MLA_NOPE]
        qr = q[:, MLA_HEADS * MLA_NOPE + h * MLA_ROPE:MLA_HEADS * MLA_NOPE + (h + 1) * MLA_ROPE]
        ssq = jnp.sum(qn * qn, axis=-1, keepdims=True) + jnp.sum(qr * qr, axis=-1, keepdims=True)
        inv = lax.rsqrt(ssq * (1.0 / MLA_QK) + RMS_EPS) * scale
        q_ref[0, h, :, 0:MLA_NOPE] = (qn * inv * gqn[:, :MLA_NOPE]).astype(BF16)
        q_ref[0, h, :, MLA_NOPE:MLA_QK] = (rope(qr * gqn[:, MLA_NOPE:]) * inv).astype(BF16)

        kn = kv[:, h * 256:h * 256 + MLA_NOPE]
        ssq = jnp.sum(kn * kn, axis=-1, keepdims=True) + kpe_ssq
        inv = lax.rsqrt(ssq * (1.0 / MLA_QK) + RMS_EPS)
        k_ref[0, h, :, 0:MLA_NOPE] = (kn * inv * gkn[:, :MLA_NOPE]).astype(BF16)
        k_ref[0, h, :, MLA_NOPE:MLA_QK] = (kpe_rot * inv).astype(BF16)
        v_ref[0, h, :, 0:MLA_V] = kv[:, h * 256 + MLA_NOPE:(h + 1) * 256].astype(BF16)
        v_ref[0, h, :, MLA_V:] = jnp.ones((kv.shape[0], MLA_V), BF16)


def _mla_prep(pm, pos, invf, sgn, g_qa, w_uq, g_kva, w_ukv, g_qn, g_kn, tm):
    b, t, _ = pm.shape
    tm = min(tm, t)
    full = lambda a: pl.BlockSpec(a.shape, lambda i, j: (0,) * a.ndim)
    qk_shape = jax.ShapeDtypeStruct((b, MLA_HEADS, t, MLA_QK), BF16)
    return pl.pallas_call(
        _mla_prep_kernel,
        grid=(b, t // tm),
        in_specs=[pl.BlockSpec((1, tm, MLA_COLS_PAD), lambda i, j: (i, j, 0)),
                  pl.BlockSpec((1, tm, 1), lambda i, j: (i, j, 0)),
                  full(invf), full(sgn), full(g_qa), full(w_uq), full(g_kva), full(w_ukv),
                  full(g_qn), full(g_kn)],
        out_specs=[pl.BlockSpec((1, MLA_HEADS, tm, MLA_QK), lambda i, j: (i, 0, j, 0)),
                   pl.BlockSpec((1, MLA_HEADS, tm, MLA_QK), lambda i, j: (i, 0, j, 0)),
                   pl.BlockSpec((1, MLA_HEADS, tm, 2 * MLA_V), lambda i, j: (i, 0, j, 0))],
        out_shape=[qk_shape, qk_shape, jax.ShapeDtypeStruct((b, MLA_HEADS, t, 2 * MLA_V), BF16)],
        compiler_params=_cparams(2),
        name="mla_prep",
    )(pm, pos, invf, sgn, g_qa, w_uq, g_kva, w_ukv, g_qn, g_kn)


def _attn_kernel(q_ref, k_ref, v_ref, o_ref, m_ref, acc_ref, *, tq, tk, nh):
    qi = pl.program_id(2)
    m_ref[...] = jnp.full(m_ref.shape, -jnp.inf, F32)
    acc_ref[...] = jnp.zeros(acc_ref.shape, F32)
    heads = range(nh)
    def step(j, r0, nr, triangle):
        start = pl.multiple_of(j * tk, tk)
        rows = slice(r0, r0 + nr)
        ss = [_dot_nt(q_ref[0, h, rows, :], k_ref[0, h, pl.ds(start, tk), :]) for h in heads]
        for h in heads:
            s = ss[h]
            if triangle:
                keep = (lax.broadcasted_iota(jnp.int32, (nr, tk), 1)
                        <= lax.broadcasted_iota(jnp.int32, (nr, tk), 0))
                s = jnp.where(keep, s, -jnp.inf)
            m_old = m_ref[h, rows, :]
            m_new = jnp.maximum(m_old, jnp.max(s, axis=-1, keepdims=True))
            alpha = jnp.exp2(m_old - m_new)
            p = jnp.exp2(s - jnp.concatenate([m_new] * (tk // 128), axis=1))
            pv = _dot(p.astype(BF16), v_ref[0, h, pl.ds(start, tk), :])
            acc_ref[h, rows, :] = jnp.concatenate([alpha, alpha], axis=1) * acc_ref[h, rows, :] + pv
            m_ref[h, rows, :] = m_new

    per = tq // tk
    n_full = qi * per

    def body(jj, c):
        for d in range(per):
            step(jj * per + d, 0, tq, False)
        return c

    lax.fori_loop(0, qi, body, 0)
    for d in range(per):
        step(n_full + d, d * tk, tk, True)
        if d + 1 < per:
            step(n_full + d, (d + 1) * tk, tq - (d + 1) * tk, False)
    for h in heads:
        acc = acc_ref[h]
        o_ref[0, :, h * MLA_V:(h + 1) * MLA_V] = (acc[:, :MLA_V] / acc[:, MLA_V:]).astype(o_ref.dtype)


def _attention(q, k, v, tq, tk, nh):
    b, h, t, _ = q.shape
    tq = min(tq, t)
    tk = min(tk, tq)
    return pl.pallas_call(
        functools.partial(_attn_kernel, tq=tq, tk=tk, nh=nh),
        grid=(b, h // nh, t // tq),
        in_specs=[pl.BlockSpec((1, nh, tq, MLA_QK), lambda i, j, n: (i, j, n, 0)),
                  pl.BlockSpec((1, nh, t, MLA_QK), lambda i, j, n: (i, j, 0, 0)),
                  pl.BlockSpec((1, nh, t, 2 * MLA_V), lambda i, j, n: (i, j, 0, 0))],
        out_specs=pl.BlockSpec((1, tq, nh * MLA_V), lambda i, j, n: (i, n, j)),
        out_shape=jax.ShapeDtypeStruct((b, t, h * MLA_V), BF16),
        scratch_shapes=[pltpu.VMEM((nh, tq, 128), F32), pltpu.VMEM((nh, tq, 2 * MLA_V), F32)],
        compiler_params=_cparams(3),
        name="mla_attention",
    )(q, k, v)


def _rw_prep_kernel(r_ref, k_ref, v_ref, l_ref, rp_ref, kp_ref, vp_ref, lp_ref, mu_ref, w0_ref,
                    ww2_ref, a0_ref, aw2_ref, gw2_ref, kk_ref, ka_ref, bd_ref,
                    r_o, w_o, k_o, v_o, kk_o, b_o, g_o):
    first = pl.program_id(1) == 0
    c = RW_C

    def shifted(ref, pref, mu):
        z = ref[0].astype(F32)
        prev = pref[0][15:16, :].astype(F32)
        prev = jnp.where(first, jnp.zeros_like(prev), prev)
        row = lax.broadcasted_iota(jnp.int32, z.shape, 0)
        zs = jnp.where(row == 0, prev, pltpu.roll(z, 1, axis=0))
        return z + (zs - z) * mu

    o = MLA_COLS_PAD
    r = shifted(r_ref, rp_ref, mu_ref[:, o:o + c])
    k = shifted(k_ref, kp_ref, mu_ref[:, o + c:o + 2 * c])
    v = shifted(v_ref, vp_ref, mu_ref[:, o + 2 * c:o + 3 * c])
    lz = shifted(l_ref, lp_ref, mu_ref[:, o + 3 * c:])
    wl = lz[:, 0:128]
    al = lz[:, 128:256]
    gl = lz[:, 256:512]

    f = w0_ref[...] + _dot(jnp.tanh(wl).astype(BF16), ww2_ref[...])
    softplus_neg_f = jnp.maximum(-f, 0.0) + jnp.log(1.0 + jnp.exp(-jnp.abs(f)))
    w = -softplus_neg_f - 0.5
    w_o[0] = -jnp.exp(w)
    a = jax.nn.sigmoid(a0_ref[...] + _dot(al.astype(BF16), aw2_ref[...]))
    g_o[0] = _dot(jax.nn.sigmoid(gl).astype(BF16), gw2_ref[...]).astype(g_o.dtype)

    kk = k * kk_ref[...]
    hi, lo = _split2(kk * kk)
    ssq = _dot(hi, bd_ref[...]) + _dot(lo, bd_ref[...])
    kk = kk * lax.rsqrt(jnp.maximum(ssq, 1e-24))
    r_o[0] = r.astype(r_o.dtype)
    k_o[0] = (k * (1.0 + (a - 1.0) * ka_ref[...])).astype(k_o.dtype)
    v_o[0] = v.astype(v_o.dtype)
    kk_o[0] = kk.astype(kk_o.dtype)
    b_o[0] = (kk * a).astype(b_o.dtype)


def _rw_prep(proj, mu, w0, ww2, a0, aw2, gw2, k_k, k_a, bd, tm):
    b, t, _ = proj.shape
    tm = min(tm, t)
    c = RW_C
    full = lambda a: pl.BlockSpec(a.shape, lambda i, j: (0,) * a.ndim)
    cur = lambda w, cb: pl.BlockSpec((1, tm, w), lambda i, j: (i, j, cb))
    prev = lambda w, cb: pl.BlockSpec((1, 16, w), lambda i, j: (i, jnp.maximum(j * (tm // 16) - 1, 0), cb))
    lora_cb = (MLA_COLS_PAD + 3 * c) // RW_LORA_PAD
    groups = [(c, 1), (c, 2), (c, 3), (RW_LORA_PAD, lora_cb)]
    out = [jax.ShapeDtypeStruct((b, t, c), F32 if i == 1 else BF16) for i in range(7)]
    ospec = pl.BlockSpec((1, tm, c), lambda i, j: (i, j, 0))
    return pl.pallas_call(
        _rw_prep_kernel,
        grid=(b, t // tm),
        in_specs=[cur(*g) for g in groups] + [prev(*g) for g in groups]
        + [full(mu), full(w0), full(ww2), full(a0), full(aw2), full(gw2), full(k_k), full(k_a), full(bd)],
        out_specs=[ospec] * 7,
        out_shape=out,
        compiler_params=_cparams(2),
        name="rwkv_prep",
    )(*([proj] * 8), mu, w0, ww2, a0, aw2, gw2, k_k, k_a, bd)


def _rw_chunk_kernel(r_ref, w_ref, k_ref, v_ref, kk_ref, b_ref, g_ref, rk_ref, lnw_ref, lnb_ref,
                     o_ref, s_ref):
    cs = RW_CHUNK
    hd = RW_HEAD

    @pl.when(pl.program_id(1) == 0)
    def _():
        s_ref[...] = jnp.zeros(s_ref.shape, F32)

    r = r_ref[0].astype(F32)
    w = w_ref[0]
    k = k_ref[0].astype(F32)
    v = v_ref[0].astype(F32)
    kk = kk_ref[0].astype(F32)
    b = b_ref[0].astype(F32)
    pairs = range(RW_HEADS // 2)
    pw = 2 * hd

    row = lax.broadcasted_iota(jnp.int32, (cs, pw), 0)
    col = lax.broadcasted_iota(jnp.int32, (cs, pw), 1)
    colh = jnp.bitwise_and(col, hd - 1)
    lo = col < hd
    strict = row > colh
    eye = jnp.where(row == colh, 1.0, 0.0).astype(F32)
    row2 = lax.broadcasted_iota(jnp.int32, (cs, 2 * pw), 0)
    col2 = lax.broadcasted_iota(jnp.int32, (cs, 2 * pw), 1)
    incl2 = row2 >= jnp.bitwise_and(col2, hd - 1)
    row3 = lax.broadcasted_iota(jnp.int32, (cs, 3 * cs), 0)
    col3 = lax.broadcasted_iota(jnp.int32, (cs, 3 * cs), 1)
    tri3 = jnp.where(row3 >= jnp.bitwise_and(col3, cs - 1), 1.0, 0.0).astype(BF16)

    def bdiag(x):
        zero = jnp.zeros_like(x)
        return jnp.concatenate([jnp.where(lo, x, zero), jnp.where(lo, zero, x)], axis=0)

    def cat3_lhs(x):
        hi = x.astype(BF16)
        return jnp.concatenate([hi, (x - hi.astype(F32)).astype(BF16), hi], axis=1)

    def cat3_rhs(x):
        hi, lw = _split2(x)
        bh = bdiag(hi)
        return jnp.concatenate([bh, bh, bdiag(lw)], axis=0)

    w1 = w.astype(BF16)
    w2 = (w - w1.astype(F32)).astype(BF16)
    w3 = (w - w1.astype(F32) - w2.astype(F32)).astype(BF16)
    logp = _dot(tri3, jnp.concatenate([w1, w2, w3], axis=0))
    logp_end = logp[cs - 1:cs, :]
    p_tail = jnp.exp(logp_end - logp)
    p_end = jnp.exp(logp_end)
    p_inv = jnp.exp(-logp)

    a_t = -kk * jnp.exp(logp - w)
    r_t = r * jnp.exp(logp)
    b_t = b * p_inv
    k_t = k * p_inv
    b_h = b * p_tail
    k_h = k * p_tail

    sls = [slice(p * pw, (p + 1) * pw) for p in pairs]
    lhs = [jnp.concatenate([a_t[:, sl], r_t[:, sl]], axis=0).astype(BF16) for sl in sls]
    rhs = [jnp.concatenate([bdiag(b_t[:, sl].astype(BF16)), bdiag(k_t[:, sl].astype(BF16))], axis=0)
           for sl in sls]
    gm = [_dot_nt(lhs[p], rhs[p]) for p in pairs]
    s0 = [s_ref[p] for p in pairs]
    ls = [_dot_nt(lhs[p], bdiag(s0[p].astype(BF16))) for p in pairs]
    vb = [v[:, sl].astype(BF16) for sl in sls]
    bv = [bdiag(vb[p]) for p in pairs]
    a_ak = [jnp.where(strict, gm[p][:cs, pw:], 0.0).astype(BF16) for p in pairs]
    rhs_u = [ls[p][:cs] + _dot(a_ak[p], bv[p]) for p in pairs]

    x = [jnp.where(strict, gm[p][:cs, :pw], 0.0) for p in pairs]
    tm = [eye + x[p] for p in pairs]
    x = [_dot(cat3_lhs(x[p]), cat3_rhs(x[p])) for p in pairs]
    for _ in range(4):
        both = [_dot(cat3_lhs(jnp.concatenate([x[p], tm[p]], axis=0)), cat3_rhs(x[p])) for p in pairs]
        x = [both[p][:cs] for p in pairs]
        tm = [tm[p] + both[p][cs:] for p in pairs]
    tm = [tm[p] + _dot(cat3_lhs(tm[p]), cat3_rhs(x[p])) for p in pairs]

    u = [_dot(cat3_lhs(tm[p]), cat3_rhs(rhs_u[p])) for p in pairs]
    ub = [u[p].astype(BF16) for p in pairs]
    a_r = [jnp.where(incl2, gm[p][cs:, :], 0.0).astype(BF16) for p in pairs]
    y = [ls[p][cs:] + _dot(a_r[p], jnp.concatenate([bdiag(ub[p]), bv[p]], axis=0)) for p in pairs]
    for p in pairs:
        uv = jnp.concatenate([ub[p], vb[p]], axis=0)
        bk = jnp.concatenate([b_h[:, sls[p]], k_h[:, sls[p]]], axis=0).astype(BF16)
        full = _dot_tn(uv, bk)
        s_ref[p] = s0[p] * p_end[:, sls[p]] + jnp.where(lo, full[:hd], full[hd:])

    def head_sum(t):
        s_lo = jnp.sum(jnp.where(lo, t, 0.0), axis=-1, keepdims=True)
        s_all = jnp.sum(t, axis=-1, keepdims=True)
        return jnp.where(lo, s_lo, s_all - s_lo)

    rk_all = r * k * rk_ref[...]
    g = g_ref[0].astype(F32)
    for p in pairs:
        sl = sls[p]
        yc = y[p] - head_sum(y[p]) * (1.0 / hd)
        var = head_sum(yc * yc) * (1.0 / hd)
        yn = yc * lax.rsqrt(var + RW_GN_EPS) * lnw_ref[:, sl] + lnb_ref[:, sl]
        bonus = head_sum(rk_all[:, sl]) * v[:, sl]
        o_ref[0, :, sl] = ((yn + bonus) * g[:, sl]).astype(o_ref.dtype)


def _rw_chunk(r, w, k, v, kk, bb, g, r_k, ln_w, ln_b):
    b, t, c = r.shape
    cs = RW_CHUNK
    full = lambda a: pl.BlockSpec(a.shape, lambda i, j: (0,) * a.ndim)
    spec = pl.BlockSpec((1, cs, c), lambda i, j: (i, j, 0))
    return pl.pallas_call(
        _rw_chunk_kernel,
        grid=(b, t // cs),
        in_specs=[spec] * 7 + [full(r_k), full(ln_w), full(ln_b)],
        out_specs=spec,
        out_shape=jax.ShapeDtypeStruct((b, t, c), BF16),
        scratch_shapes=[pltpu.VMEM((RW_HEADS // 2, RW_HEAD, 2 * RW_HEAD), F32)],
        compiler_params=_cparams(2),
        name="rwkv_chunk",
    )(r, w, k, v, kk, bb, g, r_k, ln_w, ln_b)


def _pack_pair(a, b):
    ua = lax.bitcast_convert_type(a.astype(BF16).astype(F32), jnp.uint32)
    ub = lax.bitcast_convert_type(b.astype(BF16).astype(F32), jnp.uint32)
    return lax.shift_right_logical(ua, jnp.uint32(16)) | (ub & jnp.uint32(0xFFFF0000))


def _unpack_pair(u):
    lo = lax.bitcast_convert_type(lax.shift_left(u, jnp.uint32(16)), F32)
    hi = lax.bitcast_convert_type(u & jnp.uint32(0xFFFF0000), F32)
    return lo, hi


def _store_rows(ref, packed):
    m = packed.shape[0]
    for s in range(ROW_SUB):
        ref[pl.ds(s, m, stride=ROW_SUB), :] = packed[:, s * 128:(s + 1) * 128]


def _load_rows(ref):
    m = ref.shape[-2] // ROW_SUB
    los, his = [], []
    for s in range(ROW_SUB):
        lo, hi = _unpack_pair(ref[pl.ds(s, m, stride=ROW_SUB), :])
        los.append(lo.astype(BF16))
        his.append(hi.astype(BF16))
    return jnp.concatenate(los, axis=1), jnp.concatenate(his, axis=1)


def _out_proj_kernel(ya_ref, yr_ref, x_ref, w_ref, g_ref, wrh_ref, wrl_ref, h_ref, hn_ref, lg_ref):
    half = ya_ref.shape[1]
    acc = _dot(ya_ref[...], w_ref[0:half, :]) + _dot(yr_ref[...], w_ref[half:, :])
    h = x_ref[...] + acc
    h_ref[...] = h
    hn = _rms(h, g_ref[...])
    d2 = hn.shape[1] // 2
    _store_rows(hn_ref, _pack_pair(hn[:, :d2], hn[:, d2:]))
    hh, hl = _split2(hn)
    lg_ref[...] = (_dot_nt(wrh_ref[...], hh) + _dot_nt(wrh_ref[...], hl)
                   + _dot_nt(wrl_ref[...], hh))


def _out_proj(ya, yr, x, w, g, wr, tm):
    wr_hi, wr_lo = _split2(wr.T)
    m, d = x.shape
    tm = min(tm, m)
    full = lambda a: pl.BlockSpec(a.shape, lambda i: (0,) * a.ndim)
    return pl.pallas_call(
        _out_proj_kernel,
        grid=(m // tm,),
        in_specs=[pl.BlockSpec((tm, ya.shape[1]), lambda i: (i, 0)),
                  pl.BlockSpec((tm, yr.shape[1]), lambda i: (i, 0)),
                  pl.BlockSpec((tm, d), lambda i: (i, 0)),
                  full(w), full(g), full(wr_hi), full(wr_lo)],
        out_specs=[pl.BlockSpec((tm, d), lambda i: (i, 0)),
                   pl.BlockSpec((tm * ROW_SUB, 128), lambda i: (i, 0)),
                   pl.BlockSpec((N_EXPERTS, tm), lambda i: (0, i))],
        out_shape=[jax.ShapeDtypeStruct((m, d), F32), jax.ShapeDtypeStruct((m * ROW_SUB, 128), jnp.uint32),
                   jax.ShapeDtypeStruct((N_EXPERTS, m), F32)],
        compiler_params=_cparams(1),
        name="out_proj_router_logits",
    )(ya, yr, x, w, g, wr_hi, wr_lo)


def _router_kernel(lg_ref, bias_ref, e_ref, r_ref, w_ref, cnt_ref, carry_ref):
    tm = lg_ref.shape[1]
    gsz = N_EXPERTS // N_GROUPS

    @pl.when(pl.program_id(0) == 0)
    def _():
        carry_ref[...] = jnp.zeros(carry_ref.shape, F32)

    scores = jax.nn.sigmoid(lg_ref[...]).reshape(N_GROUPS, gsz, tm)
    biased = scores + bias_ref[...].reshape(N_GROUPS, gsz, 1)
    neg = -jnp.inf

    eidx = lax.broadcasted_iota(jnp.int32, biased.shape, 1)
    m1 = jnp.max(biased, axis=1, keepdims=True)
    first = jnp.min(jnp.where(biased == m1, eidx, gsz), axis=1, keepdims=True)
    m2 = jnp.max(jnp.where(eidx == first, neg, biased), axis=1, keepdims=True)
    gscore = m1 + m2

    gidx = lax.broadcasted_iota(jnp.int32, gscore.shape, 0)
    gsel = jnp.zeros(gscore.shape, jnp.bool_)
    work = gscore
    for _ in range(TOPK_GROUPS):
        best = jnp.max(work, axis=0, keepdims=True)
        pick = jnp.min(jnp.where(work == best, gidx, N_GROUPS), axis=0, keepdims=True)
        hit = gidx == pick
        gsel = jnp.logical_or(gsel, hit)
        work = jnp.where(hit, neg, work)

    flat = lax.broadcasted_iota(jnp.int32, biased.shape, 0) * gsz + eidx
    work = jnp.where(gsel, biased, neg)
    hits = []
    picks = []
    for _ in range(TOP_K):
        best = jnp.max(jnp.max(work, axis=1, keepdims=True), axis=0, keepdims=True)
        cand = jnp.where(work == best, flat, N_EXPERTS)
        pick = jnp.min(jnp.min(cand, axis=1, keepdims=True), axis=0, keepdims=True)
        hit = flat == pick
        hits.append(hit)
        picks.append(pick)
        work = jnp.where(hit, neg, work)

    def pick_value(hit, val):
        s = jnp.sum(jnp.sum(jnp.where(hit, val, 0.0), axis=1, keepdims=True), axis=0, keepdims=True)
        return s.reshape(1, tm)

    sel = jnp.zeros(biased.shape, F32)
    for hit in hits:
        sel = jnp.where(hit, 1.0, sel)
    sel2 = sel.reshape(N_EXPERTS, tm)
    r_i = lax.broadcasted_iota(jnp.int32, (tm, tm), 0)
    c_i = lax.broadcasted_iota(jnp.int32, (tm, tm), 1)
    upper = jnp.where(r_i < c_i, 1.0, 0.0).astype(BF16)
    rank = _dot(sel2.astype(BF16), upper) + carry_ref[...]
    carry = carry_ref[...] + jnp.sum(sel2, axis=1, keepdims=True)
    carry_ref[...] = carry
    cnt_ref[...] = carry
    rank3 = rank.reshape(N_GROUPS, gsz, tm)

    raw = [pick_value(hit, scores) for hit in hits]
    tot = raw[0]
    for x in raw[1:]:
        tot = tot + x
    wrow = lax.broadcasted_iota(jnp.int32, (N_EXPERTS, tm), 0)
    wt = jnp.zeros((N_EXPERTS, tm), F32)
    for j in range(TOP_K):
        e_ref[j:j + 1, :] = picks[j].reshape(1, tm)
        r_ref[j:j + 1, :] = pick_value(hits[j], rank3).astype(jnp.int32)
        wt = jnp.where(wrow == j, raw[j] / tot * ROUTED_SCALE, wt)
    w_ref[...] = wt.T


def _router(lg_t, bias, tm):
    e, m = lg_t.shape
    tm = min(tm, m)
    row = pl.BlockSpec((TOP_K, tm), lambda i: (0, i))
    return pl.pallas_call(
        _router_kernel,
        grid=(m // tm,),
        in_specs=[pl.BlockSpec((e, tm), lambda i: (0, i)),
                  pl.BlockSpec((e, 1), lambda i: (0, 0))],
        out_specs=[row, row, pl.BlockSpec((tm, e), lambda i: (i, 0)),
                   pl.BlockSpec((e, 1), lambda i: (0, 0))],
        out_shape=[jax.ShapeDtypeStruct((TOP_K, m), jnp.int32),
                   jax.ShapeDtypeStruct((TOP_K, m), jnp.int32),
                   jax.ShapeDtypeStruct((m, e), F32),
                   jax.ShapeDtypeStruct((e, 1), F32)],
        scratch_shapes=[pltpu.VMEM((e, 1), F32)],
        compiler_params=_cparams(1),
        name="router_topk",
    )(lg_t, bias)


def _dest_kernel(off_ref, e_ref, r_ref, d_ref):
    e = e_ref[...]
    acc = r_ref[...]
    for x in range(N_EXPERTS):
        acc = acc + jnp.where(e == x, off_ref[x], 0)
    d_ref[...] = acc * ROW_SUB


def _dest(off, eidx, rank, tm):
    k, m = eidx.shape
    tm = min(tm, m)
    spec = pl.BlockSpec((k, tm), lambda i: (0, i))
    return pl.pallas_call(
        _dest_kernel,
        grid=(m // tm,),
        in_specs=[pl.BlockSpec(memory_space=pltpu.SMEM), spec, spec],
        out_specs=spec,
        out_shape=jax.ShapeDtypeStruct((k, m), jnp.int32),
        compiler_params=_cparams(1),
        name="moe_dest_rows",
    )(off, eidx, rank)


def _scatter_kernel(zt_ref, d_ref, x_ref, z_ref, xs_hbm, sem, *, tm, tile):
    i = pl.program_id(0)
    rs = ROW_SUB

    @pl.when(i == 0)
    def _():
        def zero(e, c):
            start = pl.multiple_of(zt_ref[e] * (tile * rs), tile * rs)
            pltpu.make_async_copy(z_ref, xs_hbm.at[pl.ds(start, tile * rs)], sem).start()
            return c

        lax.fori_loop(0, N_EXPERTS, zero, 0)

        def zwait(e, c):
            pltpu.make_async_copy(z_ref, xs_hbm.at[pl.ds(0, tile * rs)], sem).wait()
            return c

        lax.fori_loop(0, N_EXPERTS, zwait, 0)

    def body(n, c):
        src = x_ref.at[pl.ds(pl.multiple_of(n * rs, rs), rs)]
        for j in range(TOP_K):
            dst = xs_hbm.at[pl.ds(pl.multiple_of(d_ref[n * TOP_K + j], rs), rs)]
            pltpu.make_async_copy(src, dst, sem).start(priority=j % 2)
        return c

    lax.fori_loop(0, tm, body, 0)

    def wait(n, c):
        for j in range(TOP_K):
            pltpu.make_async_copy(x_ref.at[pl.ds(0, rs)], xs_hbm.at[pl.ds(0, rs)], sem).wait()
        return c

    lax.fori_loop(0, tm, wait, 0)


def _scatter_rows(zero_tile, dest, xp, rows, tile, tm):
    m = xp.shape[0] // ROW_SUB
    tm = min(tm, m)
    return pl.pallas_call(
        functools.partial(_scatter_kernel, tm=tm, tile=tile),
        grid_spec=pltpu.PrefetchScalarGridSpec(
            num_scalar_prefetch=1,
            grid=(m // tm,),
            in_specs=[pl.BlockSpec((tm * TOP_K,), lambda i, zt: (i,), memory_space=pltpu.SMEM),
                      pl.BlockSpec((tm * ROW_SUB, 128), lambda i, zt: (i, 0)),
                      pl.BlockSpec((tile * ROW_SUB, 128), lambda i, zt: (0, 0))],
            out_specs=pl.BlockSpec(memory_space=pl.ANY),
            scratch_shapes=[pltpu.SemaphoreType.DMA(())],
        ),
        out_shape=jax.ShapeDtypeStruct((rows * ROW_SUB, 128), jnp.uint32),
        compiler_params=_cparams(1),
        name="moe_scatter_rows",
    )(zero_tile, dest, xp, jnp.zeros((tile * ROW_SUB, 128), jnp.uint32))


def _expert_kernel(te_ref, nx_ref, gi_ref, nu_ref, x_ref, wg_hbm, wu_hbm, wd_hbm, o_ref,
                   wgf, wuf, wdf, wgb, wub, wdb, sem):
    t = pl.program_id(0)
    tc = jnp.minimum(t, nu_ref[0] - 1)
    prev = jnp.maximum(tc - 1, 0)
    new_expert = jnp.logical_or(t == 0, te_ref[tc] != te_ref[prev])
    slot = jnp.bitwise_and(gi_ref[tc], 1)

    def copies(e, s):
        return (pltpu.make_async_copy(wg_hbm.at[e], wgf.at[s], sem.at[s]),
                pltpu.make_async_copy(wu_hbm.at[e], wuf.at[s], sem.at[s]),
                pltpu.make_async_copy(wd_hbm.at[e], wdf.at[s], sem.at[s]))

    @pl.when(t == 0)
    def _():
        for c in copies(te_ref[0], 0):
            c.start()

    @pl.when(jnp.logical_and(new_expert, t < nu_ref[0]))
    def _():
        for c in copies(te_ref[tc], slot):
            c.wait()
        wgb[...] = wgf[slot].astype(BF16)
        wub[...] = wuf[slot].astype(BF16)
        wdb[...] = wdf[slot].astype(BF16)

        @pl.when(nx_ref[tc] >= 0)
        def _():
            for c in copies(nx_ref[tc], 1 - slot):
                c.start()

    @pl.when(t < nu_ref[0])
    def _():
        lo, hi = _load_rows(x_ref)
        d2 = lo.shape[1]
        gate = _dot(lo, wgb[0:d2, :]) + _dot(hi, wgb[d2:, :])
        up = _dot(lo, wub[0:d2, :]) + _dot(hi, wub[d2:, :])
        mid = (gate * jax.nn.sigmoid(gate) * up).astype(BF16)
        y = _dot(mid, wdb[...])
        _store_rows(o_ref, _pack_pair(y[:, :d2], y[:, d2:]))


def _experts(tile_expert, next_expert, group_idx, n_used, xs, wg, wu, wd, tile):
    rows = xs.shape[0] // ROW_SUB
    d = wg.shape[1]
    blk = (tile * ROW_SUB, 128)
    tile_map = lambda t, te, nx, gi, nu: (jnp.minimum(t, nu[0] - 1), 0)
    hbm = pl.BlockSpec(memory_space=pl.ANY)
    return pl.pallas_call(
        _expert_kernel,
        grid_spec=pltpu.PrefetchScalarGridSpec(
            num_scalar_prefetch=4,
            grid=(rows // tile,),
            in_specs=[pl.BlockSpec(blk, tile_map), hbm, hbm, hbm],
            out_specs=pl.BlockSpec(blk, tile_map),
            scratch_shapes=[pltpu.VMEM((2, d, EXPERT_DIM), F32), pltpu.VMEM((2, d, EXPERT_DIM), F32),
                            pltpu.VMEM((2, EXPERT_DIM, d), F32),
                            pltpu.VMEM((d, EXPERT_DIM), BF16), pltpu.VMEM((d, EXPERT_DIM), BF16),
                            pltpu.VMEM((EXPERT_DIM, d), BF16), pltpu.SemaphoreType.DMA((2,))],
        ),
        out_shape=jax.ShapeDtypeStruct(xs.shape, jnp.uint32),
        compiler_params=_cparams(1),
        name="moe_experts",
    )(tile_expert, next_expert, group_idx, n_used, xs, wg, wu, wd)


def _combine_kernel(d_ref, dn_ref, ys_hbm, w_ref, h_ref, xp_ref, sg_ref, su_ref, sd_ref, o_ref, buf, sem,
                    *, tm, nt):
    i = pl.program_id(0)
    slot = jnp.bitwise_and(i, 1)
    nxt = 1 - slot
    rs = ROW_SUB

    def start(idx_ref, n, s):
        for j in range(TOP_K):
            src = ys_hbm.at[pl.ds(pl.multiple_of(idx_ref[n * TOP_K + j], rs), rs)]
            pltpu.make_async_copy(src, buf.at[s, j, pl.ds(n * rs, rs)], sem.at[s]).start(priority=j % 2)

    def wait_all(s):
        for j in range(TOP_K):
            pltpu.make_async_copy(ys_hbm.at[pl.ds(0, tm * rs)], buf.at[s, j], sem.at[s]).wait()

    @pl.when(i == 0)
    def _():
        def body(n, c):
            for j in range(TOP_K):
                src = ys_hbm.at[pl.ds(pl.multiple_of(d_ref[n * TOP_K + j], rs), rs)]
                dst = buf.at[0, j, pl.ds(pl.multiple_of(n * rs, rs), rs)]
                pltpu.make_async_copy(src, dst, sem.at[0]).start(priority=j % 2)
            return c

        lax.fori_loop(0, tm, body, 0)

    per = tm // (rs * TOP_K)
    batches = iter(range(0, tm, per))

    def start_batch():
        n0 = next(batches)
        for n in range(n0, n0 + per):
            start(dn_ref, n, nxt)

    lo, hi = _load_rows(xp_ref)
    d2 = lo.shape[1]
    gate = _dot(lo, sg_ref[0:d2, :]) + _dot(hi, sg_ref[d2:, :])
    up = _dot(lo, su_ref[0:d2, :]) + _dot(hi, su_ref[d2:, :])
    mid = (gate * jax.nn.sigmoid(gate) * up).astype(BF16)
    o_ref[...] = h_ref[...] + _dot(mid, sd_ref[...])

    wait_all(slot)
    wts = w_ref[...]
    wj = [wts[:, j:j + 1] for j in range(TOP_K)]
    for s in range(rs):
        r_lo = jnp.zeros((tm, 128), F32)
        r_hi = jnp.zeros((tm, 128), F32)
        for j in range(TOP_K):
            start_batch()
            ylo, yhi = _unpack_pair(buf[slot, j, pl.ds(s, tm, stride=rs), :])
            r_lo = r_lo + wj[j] * ylo
            r_hi = r_hi + wj[j] * yhi
        o_ref[:, s * 128:(s + 1) * 128] += r_lo
        o_ref[:, d2 + s * 128:d2 + (s + 1) * 128] += r_hi

    @pl.when(i == nt - 1)
    def _():
        wait_all(nxt)


def _combine(dest, ys, wcol, h, xp, sg, su, sd, tm):
    m, d = h.shape
    tm = min(tm, m)
    nt = m // tm
    full = lambda a: pl.BlockSpec(a.shape, lambda i: (0,) * a.ndim)
    idx = lambda f: pl.BlockSpec((tm * TOP_K,), f, memory_space=pltpu.SMEM)
    return pl.pallas_call(
        functools.partial(_combine_kernel, tm=tm, nt=nt),
        grid=(nt,),
        in_specs=[idx(lambda i: (i,)), idx(lambda i: (jnp.minimum(i + 1, nt - 1),)),
                  pl.BlockSpec(memory_space=pl.ANY),
                  pl.BlockSpec((tm, N_EXPERTS), lambda i: (i, 0)),
                  pl.BlockSpec((tm, d), lambda i: (i, 0)),
                  pl.BlockSpec((tm * ROW_SUB, 128), lambda i: (i, 0)),
                  full(sg), full(su), full(sd)],
        out_specs=pl.BlockSpec((tm, d), lambda i: (i, 0)),
        out_shape=jax.ShapeDtypeStruct((m, d), F32),
        scratch_shapes=[pltpu.VMEM((2, TOP_K, tm * ROW_SUB, 128), jnp.uint32),
                        pltpu.SemaphoreType.DMA((2,))],
        compiler_params=_cparams(1),
        name="moe_combine_shared",
    )(dest, dest, ys, wcol, h, xp, sg, su, sd)


def _moe(h1, hnp, lg_t, router_bias, w_gate, w_up, w_down, ws_gate, ws_up, ws_down):
    n = h1.shape[0]
    tile = MOE_TILE
    eidx, rank, wcol, cnt = _router(lg_t, router_bias.reshape(-1, 1), 512)
    counts = cnt[:, 0].astype(jnp.int32)
    tiles_per = (counts + tile - 1) // tile
    tile_end = jnp.cumsum(tiles_per)
    tile_start = tile_end - tiles_per
    n_tiles = (n * TOP_K) // tile + N_EXPERTS
    rows = n_tiles * tile
    tile_expert = jnp.minimum(jnp.sum(tile_end[None, :] <= jnp.arange(n_tiles)[:, None], axis=1),
                              N_EXPERTS - 1).astype(jnp.int32)
    n_used = tile_end[-1:].astype(jnp.int32)
    ids = jnp.arange(N_EXPERTS)
    later = jnp.logical_and(ids[None, :] > ids[:, None], tiles_per[None, :] > 0)
    nxt_of = jnp.min(jnp.where(later, ids[None, :], N_EXPERTS), axis=1)
    nxt_of = jnp.where(nxt_of == N_EXPERTS, -1, nxt_of)
    ord_of = jnp.cumsum((tiles_per > 0).astype(jnp.int32)) - 1
    onehot = tile_expert[:, None] == ids[None, :]
    next_expert = jnp.sum(jnp.where(onehot, nxt_of[None, :], 0), axis=1).astype(jnp.int32)
    group_idx = jnp.sum(jnp.where(onehot, ord_of[None, :], 0), axis=1).astype(jnp.int32)
    last_tile = jnp.clip(tile_end - 1, 0, n_tiles - 1).astype(jnp.int32)
    dest = _dest((tile_start * tile).astype(jnp.int32), eidx, rank, 2048)
    dest = dest.T.reshape(-1)
    xs = _scatter_rows(last_tile, dest, hnp, rows, tile, 512)
    ys = _experts(tile_expert, next_expert, group_idx, n_used, xs, w_gate, w_up, w_down, tile)
    return _combine(dest, ys, wcol, h1, hnp, ws_gate.astype(BF16), ws_up.astype(BF16),
                    ws_down.astype(BF16), 256)


def _ple_kernel(h_ref, p_ref, gin_ref, wg_ref, bg_ref, wp_ref, gout_ref, o_ref):
    h = h_ref[...]
    gate = jax.nn.sigmoid(_dot(_rms(h, gin_ref[...]).astype(BF16), wg_ref[...]) + bg_ref[...])
    pp = _dot(p_ref[...].astype(BF16), wp_ref[...])
    o_ref[...] = h + _rms(pp * gate, gout_ref[...])


def _ple(h, p, g_in, w_g, b_g, w_p, g_out, tm):
    m, d = h.shape
    tm = min(tm, m)
    full = lambda a: pl.BlockSpec(a.shape, lambda i: (0,) * a.ndim)
    return pl.pallas_call(
        _ple_kernel,
        grid=(m // tm,),
        in_specs=[pl.BlockSpec((tm, d), lambda i: (i, 0)),
                  pl.BlockSpec((tm, p.shape[1]), lambda i: (i, 0)),
                  full(g_in), full(w_g), full(b_g), full(w_p), full(g_out)],
        out_specs=pl.BlockSpec((tm, d), lambda i: (i, 0)),
        out_shape=jax.ShapeDtypeStruct((m, d), F32),
        compiler_params=_cparams(1),
        name="ple",
    )(h, p, g_in, w_g, b_g, w_p, g_out)


def _pad_rows(a, rows):
    return jnp.pad(a, ((0, rows - a.shape[0]), (0, 0)))


def _row(a):
    return a.reshape(1, -1)


def _proj_layout(a):
    def padded(x, width):
        return jnp.pad(x, [(0, 0)] * (x.ndim - 1) + [(0, width - x.shape[-1])])

    parts = [padded(a[..., :MLA_COLS], MLA_COLS_PAD), a[..., MLA_COLS:MLA_COLS + 3 * RW_C]]
    off = MLA_COLS + 3 * RW_C
    for width, wide in ((RW_DECAY_LORA, 128), (RW_A_LORA, 128), (RW_GATE_LORA, 256)):
        parts.append(padded(a[..., off:off + width], wide))
        off += width
    return jnp.concatenate(parts, axis=-1)


def _mla_from_proj(proj, positions, g_qa, w_uq, g_kva, w_ukv, g_qn, g_kn):
    b, t, _ = proj.shape
    wq = w_uq.reshape(MLA_Q_LORA, MLA_HEADS, MLA_QK)
    wq = jnp.concatenate([wq[:, :, :MLA_NOPE].reshape(MLA_Q_LORA, -1),
                          wq[:, :, MLA_NOPE:].reshape(MLA_Q_LORA, -1)], axis=1).astype(BF16)
    half = MLA_ROPE // 2
    inv = ROPE_THETA ** (-jnp.arange(half, dtype=F32) / half)
    invf = _row(jnp.concatenate([inv, inv]))
    sgn = _row(jnp.concatenate([-jnp.ones(half, F32), jnp.ones(half, F32)]))
    q, k, v = _mla_prep(proj, positions.reshape(b, t, 1), invf, sgn, _row(g_qa), wq, _row(g_kva),
                        w_ukv.astype(BF16), _row(g_qn), _row(g_kn), 512)
    return _attention(q, k, v, 1024, 512, 2)


def _rwkv_from_proj(proj, mu, w0, w_w2, a0, a_w2, g_w2, k_k, k_a, r_k, ln_w, ln_b):
    head_of = jnp.arange(RW_C) // RW_HEAD
    bd = (head_of[:, None] == head_of[None, :]).astype(BF16)
    mu_all = jnp.concatenate([jnp.zeros((MLA_COLS,), F32), mu])
    rr, ww, kx, vx, kkx, bx, gx = _rw_prep(
        proj, _row(_proj_layout(mu_all)), _row(w0), _pad_rows(w_w2, 128).astype(BF16), _row(a0),
        _pad_rows(a_w2, 128).astype(BF16), _pad_rows(g_w2, 256).astype(BF16), _row(k_k), _row(k_a),
        bd, 256)
    return _rw_chunk(rr, ww, kx, vx, kkx, bx, gx, _row(r_k), _row(ln_w), _row(ln_b))


def _layer(h, p, positions, g_mix, w_in, mla_g_qa, mla_w_uq, mla_g_kva, mla_w_ukv, mla_g_qn,
           mla_g_kn, rw_mu, rw_w0, rw_w_w2, rw_a0, rw_a_w2, rw_g_w2, rw_k_k, rw_k_a, rw_r_k,
           rw_ln_w, rw_ln_b, w_out, g_ffn, w_router, router_bias, w_gate, w_up, w_down, ws_gate,
           ws_up, ws_down, g_ple_in, w_ple_gate, b_ple_gate, w_ple_proj, g_ple_out):
    b, t, d = h.shape
    n = b * t
    x2 = h.reshape(n, d)

    proj = _in_proj(x2, _row(g_mix), _proj_layout(w_in).astype(BF16), 1024).reshape(b, t, -1)
    y_mla = _mla_from_proj(proj, positions, mla_g_qa, mla_w_uq, mla_g_kva, mla_w_ukv, mla_g_qn, mla_g_kn)
    y_rw = _rwkv_from_proj(proj, rw_mu, rw_w0, rw_w_w2, rw_a0, rw_a_w2, rw_g_w2, rw_k_k, rw_k_a, rw_r_k,
                           rw_ln_w, rw_ln_b)

    h1, hnp, lg_t = _out_proj(y_mla.reshape(n, -1), y_rw.reshape(n, -1), x2, w_out.astype(BF16),
                              _row(g_ffn), w_router, 512)
    h2 = _moe(h1, hnp, lg_t, router_bias, w_gate, w_up, w_down, ws_gate, ws_up, ws_down)

    h3 = _ple(h2, p.reshape(n, -1), _row(g_ple_in), w_ple_gate.astype(BF16), _row(b_ple_gate),
              w_ple_proj.astype(BF16), _row(g_ple_out), 512)
    return h3.reshape(b, t, d)


def kernel(x, p, positions, g_mix, w_in, mla_g_qa, mla_w_uq, mla_g_kva, mla_w_ukv, mla_g_qn, mla_g_kn, rw_mu, rw_w0, rw_w_w2, rw_a0, rw_a_w2, rw_g_w2, rw_k_k, rw_k_a, rw_r_k, rw_ln_w, rw_ln_b, w_out, g_ffn, w_router, router_bias, w_gate, w_up, w_down, ws_gate, ws_up, ws_down, g_ple_in, w_ple_gate, b_ple_gate, w_ple_proj, g_ple_out):
    params = (g_mix, w_in, mla_g_qa, mla_w_uq, mla_g_kva, mla_w_ukv, mla_g_qn, mla_g_kn, rw_mu,
              rw_w0, rw_w_w2, rw_a0, rw_a_w2, rw_g_w2, rw_k_k, rw_k_a, rw_r_k, rw_ln_w, rw_ln_b,
              w_out, g_ffn, w_router, router_bias, w_gate, w_up, w_down, ws_gate, ws_up, ws_down,
              g_ple_in, w_ple_gate, b_ple_gate, w_ple_proj, g_ple_out)
    h = x
    for i in range(g_mix.shape[0]):
        h = _layer(h, p[i], positions, *[a[i] for a in params])
    return h
```

```python
import functools

import jax
import jax.numpy as jnp
from jax import lax
from jax.experimental import pallas as pl
from jax.experimental.pallas import tpu as pltpu

F32 = jnp.float32
BF16 = jnp.bfloat16

D_MODEL = 2048
PLE_DIM = 256
RMS_EPS = 1e-6

MLA_HEADS = 8
MLA_NOPE = 128
MLA_ROPE = 64
MLA_QK = MLA_NOPE + MLA_ROPE
MLA_V = 128
MLA_Q_LORA = 512
MLA_KV_LORA = 256
ROPE_THETA = 10000.0
MLA_COLS = MLA_Q_LORA + MLA_KV_LORA + MLA_ROPE
MLA_COLS_PAD = 1024

RW_HEADS = 16
RW_HEAD = 64
RW_C = RW_HEADS * RW_HEAD
RW_DECAY_LORA = 64
RW_A_LORA = 64
RW_GATE_LORA = 160
RW_GN_EPS = 64e-5
RW_LORA_PAD = 512
PROJ_COLS = MLA_COLS_PAD + 3 * RW_C + RW_LORA_PAD
PROJ_TN = 1536
RW_CHUNK = 64

N_EXPERTS = 64
TOP_K = 8
N_GROUPS = 8
TOPK_GROUPS = 4
EXPERT_DIM = 512
ROUTED_SCALE = 2.5
MOE_TILE = 512
ROW_SUB = D_MODEL // 2 // 128

VMEM_LIMIT = 56 * 1024 * 1024


def _cparams(n_axes):
    return pltpu.CompilerParams(dimension_semantics=("arbitrary",) * n_axes,
                                vmem_limit_bytes=VMEM_LIMIT)


def _rms(x, g):
    ms = jnp.mean(x * x, axis=-1, keepdims=True)
    return x * lax.rsqrt(ms + RMS_EPS) * g


def _dot(a, b):
    return jnp.dot(a, b, preferred_element_type=F32)


def _dot_nt(a, b):
    return lax.dot_general(a, b, (((1,), (1,)), ((), ())), preferred_element_type=F32)


def _dot_tn(a, b):
    return lax.dot_general(a, b, (((0,), (0,)), ((), ())), preferred_element_type=F32)


def _split2(x):
    hi = x.astype(BF16)
    lo = (x - hi.astype(F32)).astype(BF16)
    return hi, lo


def _in_proj_kernel(x_ref, g_ref, w_ref, o_ref, xn_ref):
    @pl.when(pl.program_id(1) == 0)
    def _():
        xn_ref[...] = _rms(x_ref[...], g_ref[...]).astype(BF16)

    o_ref[...] = _dot(xn_ref[...], w_ref[...]).astype(o_ref.dtype)


def _in_proj(x, g, w, tm):
    m, k = x.shape
    n = w.shape[1]
    tn = PROJ_TN
    tm = min(tm, m)
    return pl.pallas_call(
        _in_proj_kernel,
        grid=(m // tm, n // tn),
        in_specs=[pl.BlockSpec((tm, k), lambda i, j: (i, 0)),
                  pl.BlockSpec((1, k), lambda i, j: (0, 0)),
                  pl.BlockSpec((k, tn), lambda i, j: (0, j))],
        out_specs=pl.BlockSpec((tm, tn), lambda i, j: (i, j)),
        out_shape=jax.ShapeDtypeStruct((m, n), BF16),
        scratch_shapes=[pltpu.VMEM((tm, k), BF16)],
        compiler_params=_cparams(2),
        name="in_proj",
    )(x, g, w)


def _mla_prep_kernel(pm_ref, pos_ref, invf_ref, sgn_ref, gqa_ref, wuq_ref, gkva_ref, wukv_ref,
                     gqn_ref, gkn_ref, q_ref, k_ref, v_ref):
    pm = pm_ref[0].astype(F32)
    ang = pos_ref[0].astype(F32) * invf_ref[...]
    cos = jnp.cos(ang)
    sin = jnp.sin(ang) * sgn_ref[...]

    def rope(xr):
        half = MLA_ROPE // 2
        swapped = jnp.concatenate([xr[:, half:], xr[:, :half]], axis=1)
        return xr * cos + swapped * sin

    scale = MLA_QK ** -0.5 * 1.4426950408889634
    gqn = gqn_ref[...]
    gkn = gkn_ref[...]

    cq = _rms(pm[:, :MLA_Q_LORA], gqa_ref[...]).astype(BF16)
    q = _dot(cq, wuq_ref[...])
    ckv = _rms(pm[:, MLA_Q_LORA:MLA_Q_LORA + MLA_KV_LORA], gkva_ref[...]).astype(BF16)
    kv = _dot(ckv, wukv_ref[...])
    kpe = pm[:, MLA_Q_LORA + MLA_KV_LORA:MLA_COLS]
    kpe_ssq = jnp.sum(kpe * kpe, axis=-1, keepdims=True)
    kpe_rot = rope(kpe * gkn[:, MLA_NOPE:])

    for h in range(MLA_HEADS):
        qn = q[:, h * MLA_NOPE:(h + 1) * MLA_NOPE]
        qr = q[:, MLA_HEADS * MLA_NOPE + h * MLA_ROPE:MLA_HEADS * MLA_NOPE + (h + 1) * MLA_ROPE]
        ssq = jnp.sum(qn * qn, axis=-1, keepdims=True) + jnp.sum(qr * qr, axis=-1, keepdims=True)
        inv = lax.rsqrt(ssq * (1.0 / MLA_QK) + RMS_EPS) * scale
        q_ref[0, h, :, 0:MLA_NOPE] = (qn * inv * gqn[:, :MLA_NOPE]).astype(BF16)
        q_ref[0, h, :, MLA_NOPE:MLA_QK] = (rope(qr * gqn[:, MLA_NOPE:]) * inv).astype(BF16)

        kn = kv[:, h * 256:h * 256 + MLA_NOPE]
        ssq = jnp.sum(kn * kn, axis=-1, keepdims=True) + kpe_ssq
        inv = lax.rsqrt(ssq * (1.0 / MLA_QK) + RMS_EPS)
        k_ref[0, h, :, 0:MLA_NOPE] = (kn * inv * gkn[:, :MLA_NOPE]).astype(BF16)
        k_ref[0, h, :, MLA_NOPE:MLA_QK] = (kpe_rot * inv).astype(BF16)
        v_ref[0, h, :, 0:MLA_V] = kv[:, h * 256 + MLA_NOPE:(h + 1) * 256].astype(BF16)
        v_ref[0, h, :, MLA_V:] = jnp.ones((kv.shape[0], MLA_V), BF16)


def _mla_prep(pm, pos, invf, sgn, g_qa, w_uq, g_kva, w_ukv, g_qn, g_kn, tm):
    b, t, _ = pm.shape
    tm = min(tm, t)
    full = lambda a: pl.BlockSpec(a.shape, lambda i, j: (0,) * a.ndim)
    qk_shape = jax.ShapeDtypeStruct((b, MLA_HEADS, t, MLA_QK), BF16)
    return pl.pallas_call(
        _mla_prep_kernel,
        grid=(b, t // tm),
        in_specs=[pl.BlockSpec((1, tm, MLA_COLS_PAD), lambda i, j: (i, j, 0)),
                  pl.BlockSpec((1, tm, 1), lambda i, j: (i, j, 0)),
                  full(invf), full(sgn), full(g_qa), full(w_uq), full(g_kva), full(w_ukv),
                  full(g_qn), full(g_kn)],
        out_specs=[pl.BlockSpec((1, MLA_HEADS, tm, MLA_QK), lambda i, j: (i, 0, j, 0)),
                   pl.BlockSpec((1, MLA_HEADS, tm, MLA_QK), lambda i, j: (i, 0, j, 0)),
                   pl.BlockSpec((1, MLA_HEADS, tm, 2 * MLA_V), lambda i, j: (i, 0, j, 0))],
        out_shape=[qk_shape, qk_shape, jax.ShapeDtypeStruct((b, MLA_HEADS, t, 2 * MLA_V), BF16)],
        compiler_params=_cparams(2),
        name="mla_prep",
    )(pm, pos, invf, sgn, g_qa, w_uq, g_kva, w_ukv, g_qn, g_kn)


def _attn_kernel(q_ref, k_ref, v_ref, o_ref, m_ref, acc_ref, *, tq, tk, nh):
    qi = pl.program_id(2)
    m_ref[...] = jnp.full(m_ref.shape, -jnp.inf, F32)
    acc_ref[...] = jnp.zeros(acc_ref.shape, F32)
    heads = range(nh)
    def step(j, r0, nr, triangle):
        start = pl.multiple_of(j * tk, tk)
        rows = slice(r0, r0 + nr)
        ss = [_dot_nt(q_ref[0, h, rows, :], k_ref[0, h, pl.ds(start, tk), :]) for h in heads]
        for h in heads:
            s = ss[h]
            if triangle:
                keep = (lax.broadcasted_iota(jnp.int32, (nr, tk), 1)
                        <= lax.broadcasted_iota(jnp.int32, (nr, tk), 0))
                s = jnp.where(keep, s, -jnp.inf)
            m_old = m_ref[h, rows, :]
            m_new = jnp.maximum(m_old, jnp.max(s, axis=-1, keepdims=True))
            alpha = jnp.exp2(m_old - m_new)
            p = jnp.exp2(s - jnp.concatenate([m_new] * (tk // 128), axis=1))
            pv = _dot(p.astype(BF16), v_ref[0, h, pl.ds(start, tk), :])
            acc_ref[h, rows, :] = jnp.concatenate([alpha, alpha], axis=1) * acc_ref[h, rows, :] + pv
            m_ref[h, rows, :] = m_new

    per = tq // tk
    n_full = qi * per

    def body(jj, c):
        for d in range(per):
            step(jj * per + d, 0, tq, False)
        return c

    lax.fori_loop(0, qi, body, 0)
    for d in range(per):
        step(n_full + d, d * tk, tk, True)
        if d + 1 < per:
            step(n_full + d, (d + 1) * tk, tq - (d + 1) * tk, False)
    for h in heads:
        acc = acc_ref[h]
        o_ref[0, :, h * MLA_V:(h + 1) * MLA_V] = (acc[:, :MLA_V] / acc[:, MLA_V:]).astype(o_ref.dtype)


def _attention(q, k, v, tq, tk, nh):
    b, h, t, _ = q.shape
    tq = min(tq, t)
    tk = min(tk, tq)
    return pl.pallas_call(
        functools.partial(_attn_kernel, tq=tq, tk=tk, nh=nh),
        grid=(b, h // nh, t // tq),
        in_specs=[pl.BlockSpec((1, nh, tq, MLA_QK), lambda i, j, n: (i, j, n, 0)),
                  pl.BlockSpec((1, nh, t, MLA_QK), lambda i, j, n: (i, j, 0, 0)),
                  pl.BlockSpec((1, nh, t, 2 * MLA_V), lambda i, j, n: (i, j, 0, 0))],
        out_specs=pl.BlockSpec((1, tq, nh * MLA_V), lambda i, j, n: (i, n, j)),
        out_shape=jax.ShapeDtypeStruct((b, t, h * MLA_V), BF16),
        scratch_shapes=[pltpu.VMEM((nh, tq, 128), F32), pltpu.VMEM((nh, tq, 2 * MLA_V), F32)],
        compiler_params=_cparams(3),
        name="mla_attention",
    )(q, k, v)


def _rw_prep_kernel(r_ref, k_ref, v_ref, l_ref, rp_ref, kp_ref, vp_ref, lp_ref, mu_ref, w0_ref,
                    ww2_ref, a0_ref, aw2_ref, gw2_ref, kk_ref, ka_ref, bd_ref,
                    r_o, w_o, k_o, v_o, kk_o, b_o, g_o):
    first = pl.program_id(1) == 0
    c = RW_C

    def shifted(ref, pref, mu):
        z = ref[0].astype(F32)
        prev = pref[0][15:16, :].astype(F32)
        prev = jnp.where(first, jnp.zeros_like(prev), prev)
        row = lax.broadcasted_iota(jnp.int32, z.shape, 0)
        zs = jnp.where(row == 0, prev, pltpu.roll(z, 1, axis=0))
        return z + (zs - z) * mu

    o = MLA_COLS_PAD
    r = shifted(r_ref, rp_ref, mu_ref[:, o:o + c])
    k = shifted(k_ref, kp_ref, mu_ref[:, o + c:o + 2 * c])
    v = shifted(v_ref, vp_ref, mu_ref[:, o + 2 * c:o + 3 * c])
    lz = shifted(l_ref, lp_ref, mu_ref[:, o + 3 * c:])
    wl = lz[:, 0:128]
    al = lz[:, 128:256]
    gl = lz[:, 256:512]

    f = w0_ref[...] + _dot(jnp.tanh(wl).astype(BF16), ww2_ref[...])
    softplus_neg_f = jnp.maximum(-f, 0.0) + jnp.log(1.0 + jnp.exp(-jnp.abs(f)))
    w = -softplus_neg_f - 0.5
    w_o[0] = -jnp.exp(w)
    a = jax.nn.sigmoid(a0_ref[...] + _dot(al.astype(BF16), aw2_ref[...]))
    g_o[0] = _dot(jax.nn.sigmoid(gl).astype(BF16), gw2_ref[...]).astype(g_o.dtype)

    kk = k * kk_ref[...]
    hi, lo = _split2(kk * kk)
    ssq = _dot(hi, bd_ref[...]) + _dot(lo, bd_ref[...])
    kk = kk * lax.rsqrt(jnp.maximum(ssq, 1e-24))
    r_o[0] = r.astype(r_o.dtype)
    k_o[0] = (k * (1.0 + (a - 1.0) * ka_ref[...])).astype(k_o.dtype)
    v_o[0] = v.astype(v_o.dtype)
    kk_o[0] = kk.astype(kk_o.dtype)
    b_o[0] = (kk * a).astype(b_o.dtype)


def _rw_prep(proj, mu, w0, ww2, a0, aw2, gw2, k_k, k_a, bd, tm):
    b, t, _ = proj.shape
    tm = min(tm, t)
    c = RW_C
    full = lambda a: pl.BlockSpec(a.shape, lambda i, j: (0,) * a.ndim)
    cur = lambda w, cb: pl.BlockSpec((1, tm, w), lambda i, j: (i, j, cb))
    prev = lambda w, cb: pl.BlockSpec((1, 16, w), lambda i, j: (i, jnp.maximum(j * (tm // 16) - 1, 0), cb))
    lora_cb = (MLA_COLS_PAD + 3 * c) // RW_LORA_PAD
    groups = [(c, 1), (c, 2), (c, 3), (RW_LORA_PAD, lora_cb)]
    out = [jax.ShapeDtypeStruct((b, t, c), F32 if i == 1 else BF16) for i in range(7)]
    ospec = pl.BlockSpec((1, tm, c), lambda i, j: (i, j, 0))
    return pl.pallas_call(
        _rw_prep_kernel,
        grid=(b, t // tm),
        in_specs=[cur(*g) for g in groups] + [prev(*g) for g in groups]
        + [full(mu), full(w0), full(ww2), full(a0), full(aw2), full(gw2), full(k_k), full(k_a), full(bd)],
        out_specs=[ospec] * 7,
        out_shape=out,
        compiler_params=_cparams(2),
        name="rwkv_prep",
    )(*([proj] * 8), mu, w0, ww2, a0, aw2, gw2, k_k, k_a, bd)


def _rw_chunk_kernel(r_ref, w_ref, k_ref, v_ref, kk_ref, b_ref, g_ref, rk_ref, lnw_ref, lnb_ref,
                     o_ref, s_ref):
    cs = RW_CHUNK
    hd = RW_HEAD

    @pl.when(pl.program_id(1) == 0)
    def _():
        s_ref[...] = jnp.zeros(s_ref.shape, F32)

    r = r_ref[0].astype(F32)
    w = w_ref[0]
    k = k_ref[0].astype(F32)
    v = v_ref[0].astype(F32)
    kk = kk_ref[0].astype(F32)
    b = b_ref[0].astype(F32)
    pairs = range(RW_HEADS // 2)
    pw = 2 * hd

    row = lax.broadcasted_iota(jnp.int32, (cs, pw), 0)
    col = lax.broadcasted_iota(jnp.int32, (cs, pw), 1)
    colh = jnp.bitwise_and(col, hd - 1)
    lo = col < hd
    strict = row > colh
    eye = jnp.where(row == colh, 1.0, 0.0).astype(F32)
    row2 = lax.broadcasted_iota(jnp.int32, (cs, 2 * pw), 0)
    col2 = lax.broadcasted_iota(jnp.int32, (cs, 2 * pw), 1)
    incl2 = row2 >= jnp.bitwise_and(col2, hd - 1)
    row3 = lax.broadcasted_iota(jnp.int32, (cs, 3 * cs), 0)
    col3 = lax.broadcasted_iota(jnp.int32, (cs, 3 * cs), 1)
    tri3 = jnp.where(row3 >= jnp.bitwise_and(col3, cs - 1), 1.0, 0.0).astype(BF16)

    def bdiag(x):
        zero = jnp.zeros_like(x)
        return jnp.concatenate([jnp.where(lo, x, zero), jnp.where(lo, zero, x)], axis=0)

    def cat3_lhs(x):
        hi = x.astype(BF16)
        return jnp.concatenate([hi, (x - hi.astype(F32)).astype(BF16), hi], axis=1)

    def cat3_rhs(x):
        hi, lw = _split2(x)
        bh = bdiag(hi)
        return jnp.concatenate([bh, bh, bdiag(lw)], axis=0)

    w1 = w.astype(BF16)
    w2 = (w - w1.astype(F32)).astype(BF16)
    w3 = (w - w1.astype(F32) - w2.astype(F32)).astype(BF16)
    logp = _dot(tri3, jnp.concatenate([w1, w2, w3], axis=0))
    logp_end = logp[cs - 1:cs, :]
    p_tail = jnp.exp(logp_end - logp)
    p_end = jnp.exp(logp_end)
    p_inv = jnp.exp(-logp)

    a_t = -kk * jnp.exp(logp - w)
    r_t = r * jnp.exp(logp)
    b_t = b * p_inv
    k_t = k * p_inv
    b_h = b * p_tail
    k_h = k * p_tail

    sls = [slice(p * pw, (p + 1) * pw) for p in pairs]
    lhs = [jnp.concatenate([a_t[:, sl], r_t[:, sl]], axis=0).astype(BF16) for sl in sls]
    rhs = [jnp.concatenate([bdiag(b_t[:, sl].astype(BF16)), bdiag(k_t[:, sl].astype(BF16))], axis=0)
           for sl in sls]
    gm = [_dot_nt(lhs[p], rhs[p]) for p in pairs]
    s0 = [s_ref[p] for p in pairs]
    ls = [_dot_nt(lhs[p], bdiag(s0[p].astype(BF16))) for p in pairs]
    vb = [v[:, sl].astype(BF16) for sl in sls]
    bv = [bdiag(vb[p]) for p in pairs]
    a_ak = [jnp.where(strict, gm[p][:cs, pw:], 0.0).astype(BF16) for p in pairs]
    rhs_u = [ls[p][:cs] + _dot(a_ak[p], bv[p]) for p in pairs]

    x = [jnp.where(strict, gm[p][:cs, :pw], 0.0) for p in pairs]
    tm = [eye + x[p] for p in pairs]
    x = [_dot(cat3_lhs(x[p]), cat3_rhs(x[p])) for p in pairs]
    for _ in range(4):
        both = [_dot(cat3_lhs(jnp.concatenate([x[p], tm[p]], axis=0)), cat3_rhs(x[p])) for p in pairs]
        x = [both[p][:cs] for p in pairs]
        tm = [tm[p] + both[p][cs:] for p in pairs]
    tm = [tm[p] + _dot(cat3_lhs(tm[p]), cat3_rhs(x[p])) for p in pairs]

    u = [_dot(cat3_lhs(tm[p]), cat3_rhs(rhs_u[p])) for p in pairs]
    ub = [u[p].astype(BF16) for p in pairs]
    a_r = [jnp.where(incl2, gm[p][cs:, :], 0.0).astype(BF16) for p in pairs]
    y = [ls[p][cs:] + _dot(a_r[p], jnp.concatenate([bdiag(ub[p]), bv[p]], axis=0)) for p in pairs]
    for p in pairs:
        uv = jnp.concatenate([ub[p], vb[p]], axis=0)
        bk = jnp.concatenate([b_h[:, sls[p]], k_h[:, sls[p]]], axis=0).astype(BF16)
        full = _dot_tn(uv, bk)
        s_ref[p] = s0[p] * p_end[:, sls[p]] + jnp.where(lo, full[:hd], full[hd:])

    def head_sum(t):
        s_lo = jnp.sum(jnp.where(lo, t, 0.0), axis=-1, keepdims=True)
        s_all = jnp.sum(t, axis=-1, keepdims=True)
        return jnp.where(lo, s_lo, s_all - s_lo)

    rk_all = r * k * rk_ref[...]
    g = g_ref[0].astype(F32)
    for p in pairs:
        sl = sls[p]
        yc = y[p] - head_sum(y[p]) * (1.0 / hd)
        var = head_sum(yc * yc) * (1.0 / hd)
        yn = yc * lax.rsqrt(var + RW_GN_EPS) * lnw_ref[:, sl] + lnb_ref[:, sl]
        bonus = head_sum(rk_all[:, sl]) * v[:, sl]
        o_ref[0, :, sl] = ((yn + bonus) * g[:, sl]).astype(o_ref.dtype)


def _rw_chunk(r, w, k, v, kk, bb, g, r_k, ln_w, ln_b):
    b, t, c = r.shape
    cs = RW_CHUNK
    full = lambda a: pl.BlockSpec(a.shape, lambda i, j: (0,) * a.ndim)
    spec = pl.BlockSpec((1, cs, c), lambda i, j: (i, j, 0))
    return pl.pallas_call(
        _rw_chunk_kernel,
        grid=(b, t // cs),
        in_specs=[spec] * 7 + [full(r_k), full(ln_w), full(ln_b)],
        out_specs=spec,
        out_shape=jax.ShapeDtypeStruct((b, t, c), BF16),
        scratch_shapes=[pltpu.VMEM((RW_HEADS // 2, RW_HEAD, 2 * RW_HEAD), F32)],
        compiler_params=_cparams(2),
        name="rwkv_chunk",
    )(r, w, k, v, kk, bb, g, r_k, ln_w, ln_b)


def _pack_pair(a, b):
    ua = lax.bitcast_convert_type(a.astype(BF16).astype(F32), jnp.uint32)
    ub = lax.bitcast_convert_type(b.astype(BF16).astype(F32), jnp.uint32)
    return lax.shift_right_logical(ua, jnp.uint32(16)) | (ub & jnp.uint32(0xFFFF0000))


def _unpack_pair(u):
    lo = lax.bitcast_convert_type(lax.shift_left(u, jnp.uint32(16)), F32)
    hi = lax.bitcast_convert_type(u & jnp.uint32(0xFFFF0000), F32)
    return lo, hi


def _store_rows(ref, packed):
    m = packed.shape[0]
    for s in range(ROW_SUB):
        ref[pl.ds(s, m, stride=ROW_SUB), :] = packed[:, s * 128:(s + 1) * 128]


def _load_rows(ref):
    m = ref.shape[-2] // ROW_SUB
    los, his = [], []
    for s in range(ROW_SUB):
        lo, hi = _unpack_pair(ref[pl.ds(s, m, stride=ROW_SUB), :])
        los.append(lo.astype(BF16))
        his.append(hi.astype(BF16))
    return jnp.concatenate(los, axis=1), jnp.concatenate(his, axis=1)


def _out_proj_kernel(ya_ref, yr_ref, x_ref, w_ref, g_ref, wrh_ref, wrl_ref, h_ref, hn_ref, lg_ref):
    half = ya_ref.shape[1]
    acc = _dot(ya_ref[...], w_ref[0:half, :]) + _dot(yr_ref[...], w_ref[half:, :])
    h = x_ref[...] + acc
    h_ref[...] = h
    hn = _rms(h, g_ref[...])
    d2 = hn.shape[1] // 2
    _store_rows(hn_ref, _pack_pair(hn[:, :d2], hn[:, d2:]))
    hh, hl = _split2(hn)
    lg_ref[...] = (_dot_nt(wrh_ref[...], hh) + _dot_nt(wrh_ref[...], hl)
                   + _dot_nt(wrl_ref[...], hh))


def _out_proj(ya, yr, x, w, g, wr, tm):
    wr_hi, wr_lo = _split2(wr.T)
    m, d = x.shape
    tm = min(tm, m)
    full = lambda a: pl.BlockSpec(a.shape, lambda i: (0,) * a.ndim)
    return pl.pallas_call(
        _out_proj_kernel,
        grid=(m // tm,),
        in_specs=[pl.BlockSpec((tm, ya.shape[1]), lambda i: (i, 0)),
                  pl.BlockSpec((tm, yr.shape[1]), lambda i: (i, 0)),
                  pl.BlockSpec((tm, d), lambda i: (i, 0)),
                  full(w), full(g), full(wr_hi), full(wr_lo)],
        out_specs=[pl.BlockSpec((tm, d), lambda i: (i, 0)),
                   pl.BlockSpec((tm * ROW_SUB, 128), lambda i: (i, 0)),
                   pl.BlockSpec((N_EXPERTS, tm), lambda i: (0, i))],
        out_shape=[jax.ShapeDtypeStruct((m, d), F32), jax.ShapeDtypeStruct((m * ROW_SUB, 128), jnp.uint32),
                   jax.ShapeDtypeStruct((N_EXPERTS, m), F32)],
        compiler_params=_cparams(1),
        name="out_proj_router_logits",
    )(ya, yr, x, w, g, wr_hi, wr_lo)


def _router_kernel(lg_ref, bias_ref, e_ref, r_ref, w_ref, cnt_ref, carry_ref):
    tm = lg_ref.shape[1]
    gsz = N_EXPERTS // N_GROUPS

    @pl.when(pl.program_id(0) == 0)
    def _():
        carry_ref[...] = jnp.zeros(carry_ref.shape, F32)

    scores = jax.nn.sigmoid(lg_ref[...]).reshape(N_GROUPS, gsz, tm)
    biased = scores + bias_ref[...].reshape(N_GROUPS, gsz, 1)
    neg = -jnp.inf

    eidx = lax.broadcasted_iota(jnp.int32, biased.shape, 1)
    m1 = jnp.max(biased, axis=1, keepdims=True)
    first = jnp.min(jnp.where(biased == m1, eidx, gsz), axis=1, keepdims=True)
    m2 = jnp.max(jnp.where(eidx == first, neg, biased), axis=1, keepdims=True)
    gscore = m1 + m2

    gidx = lax.broadcasted_iota(jnp.int32, gscore.shape, 0)
    gsel = jnp.zeros(gscore.shape, jnp.bool_)
    work = gscore
    for _ in range(TOPK_GROUPS):
        best = jnp.max(work, axis=0, keepdims=True)
        pick = jnp.min(jnp.where(work == best, gidx, N_GROUPS), axis=0, keepdims=True)
        hit = gidx == pick
        gsel = jnp.logical_or(gsel, hit)
        work = jnp.where(hit, neg, work)

    flat = lax.broadcasted_iota(jnp.int32, biased.shape, 0) * gsz + eidx
    work = jnp.where(gsel, biased, neg)
    hits = []
    picks = []
    for _ in range(TOP_K):
        best = jnp.max(jnp.max(work, axis=1, keepdims=True), axis=0, keepdims=True)
        cand = jnp.where(work == best, flat, N_EXPERTS)
        pick = jnp.min(jnp.min(cand, axis=1, keepdims=True), axis=0, keepdims=True)
        hit = flat == pick
        hits.append(hit)
        picks.append(pick)
        work = jnp.where(hit, neg, work)

    def pick_value(hit, val):
        s = jnp.sum(jnp.sum(jnp.where(hit, val, 0.0), axis=1, keepdims=True), axis=0, keepdims=True)
        return s.reshape(1, tm)

    sel = jnp.zeros(biased.shape, F32)
    for hit in hits:
        sel = jnp.where(hit, 1.0, sel)
    sel2 = sel.reshape(N_EXPERTS, tm)
    r_i = lax.broadcasted_iota(jnp.int32, (tm, tm), 0)
    c_i = lax.broadcasted_iota(jnp.int32, (tm, tm), 1)
    upper = jnp.where(r_i < c_i, 1.0, 0.0).astype(BF16)
    rank = _dot(sel2.astype(BF16), upper) + carry_ref[...]
    carry = carry_ref[...] + jnp.sum(sel2, axis=1, keepdims=True)
    carry_ref[...] = carry
    cnt_ref[...] = carry
    rank3 = rank.reshape(N_GROUPS, gsz, tm)

    raw = [pick_value(hit, scores) for hit in hits]
    tot = raw[0]
    for x in raw[1:]:
        tot = tot + x
    wrow = lax.broadcasted_iota(jnp.int32, (N_EXPERTS, tm), 0)
    wt = jnp.zeros((N_EXPERTS, tm), F32)
    for j in range(TOP_K):
        e_ref[j:j + 1, :] = picks[j].reshape(1, tm)
        r_ref[j:j + 1, :] = pick_value(hits[j], rank3).astype(jnp.int32)
        wt = jnp.where(wrow == j, raw[j] / tot * ROUTED_SCALE, wt)
    w_ref[...] = wt.T


def _router(lg_t, bias, tm):
    e, m = lg_t.shape
    tm = min(tm, m)
    row = pl.BlockSpec((TOP_K, tm), lambda i: (0, i))
    return pl.pallas_call(
        _router_kernel,
        grid=(m // tm,),
        in_specs=[pl.BlockSpec((e, tm), lambda i: (0, i)),
                  pl.BlockSpec((e, 1), lambda i: (0, 0))],
        out_specs=[row, row, pl.BlockSpec((tm, e), lambda i: (i, 0)),
                   pl.BlockSpec((e, 1), lambda i: (0, 0))],
        out_shape=[jax.ShapeDtypeStruct((TOP_K, m), jnp.int32),
                   jax.ShapeDtypeStruct((TOP_K, m), jnp.int32),
                   jax.ShapeDtypeStruct((m, e), F32),
                   jax.ShapeDtypeStruct((e, 1), F32)],
        scratch_shapes=[pltpu.VMEM((e, 1), F32)],
        compiler_params=_cparams(1),
        name="router_topk",
    )(lg_t, bias)


def _dest_kernel(off_ref, e_ref, r_ref, d_ref):
    e = e_ref[...]
    acc = r_ref[...]
    for x in range(N_EXPERTS):
        acc = acc + jnp.where(e == x, off_ref[x], 0)
    d_ref[...] = acc * ROW_SUB


def _dest(off, eidx, rank, tm):
    k, m = eidx.shape
    tm = min(tm, m)
    spec = pl.BlockSpec((k, tm), lambda i: (0, i))
    return pl.pallas_call(
        _dest_kernel,
        grid=(m // tm,),
        in_specs=[pl.BlockSpec(memory_space=pltpu.SMEM), spec, spec],
        out_specs=spec,
        out_shape=jax.ShapeDtypeStruct((k, m), jnp.int32),
        compiler_params=_cparams(1),
        name="moe_dest_rows",
    )(off, eidx, rank)


def _scatter_kernel(zt_ref, d_ref, x_ref, z_ref, h_ref, sg_ref, su_ref, sd_ref, xs_hbm, hs_ref, sem,
                    *, tm, tile):
    i = pl.program_id(0)
    rs = ROW_SUB

    @pl.when(i == 0)
    def _():
        def zero(e, c):
            start = pl.multiple_of(zt_ref[e] * (tile * rs), tile * rs)
            pltpu.make_async_copy(z_ref, xs_hbm.at[pl.ds(start, tile * rs)], sem).start()
            return c

        lax.fori_loop(0, N_EXPERTS, zero, 0)

        def zwait(e, c):
            pltpu.make_async_copy(z_ref, xs_hbm.at[pl.ds(0, tile * rs)], sem).wait()
            return c

        lax.fori_loop(0, N_EXPERTS, zwait, 0)

    def body(n, c):
        src = x_ref.at[pl.ds(pl.multiple_of(n * rs, rs), rs)]
        for j in range(TOP_K):
            dst = xs_hbm.at[pl.ds(pl.multiple_of(d_ref[n * TOP_K + j], rs), rs)]
            pltpu.make_async_copy(src, dst, sem).start(priority=j % 2)
        return c

    lax.fori_loop(0, tm, body, 0)

    lo, hi = _load_rows(x_ref)
    d2 = lo.shape[1]
    gate = _dot(lo, sg_ref[0:d2, :]) + _dot(hi, sg_ref[d2:, :])
    up = _dot(lo, su_ref[0:d2, :]) + _dot(hi, su_ref[d2:, :])
    mid = (gate * jax.nn.sigmoid(gate) * up).astype(BF16)
    hs_ref[...] = h_ref[...] + _dot(mid, sd_ref[...])

    for j in range(TOP_K):
        pltpu.make_async_copy(x_ref, xs_hbm.at[pl.ds(0, tm * rs)], sem).wait()


def _scatter_rows(zero_tile, dest, xp, h, sg, su, sd, rows, tile, tm):
    m, d = h.shape
    tm = min(tm, m)
    full = lambda a: pl.BlockSpec(a.shape, lambda i, zt: (0,) * a.ndim)
    return pl.pallas_call(
        functools.partial(_scatter_kernel, tm=tm, tile=tile),
        grid_spec=pltpu.PrefetchScalarGridSpec(
            num_scalar_prefetch=1,
            grid=(m // tm,),
            in_specs=[pl.BlockSpec((tm * TOP_K,), lambda i, zt: (i,), memory_space=pltpu.SMEM),
                      pl.BlockSpec((tm * ROW_SUB, 128), lambda i, zt: (i, 0)),
                      pl.BlockSpec((tile * ROW_SUB, 128), lambda i, zt: (0, 0)),
                      pl.BlockSpec((tm, d), lambda i, zt: (i, 0)),
                      full(sg), full(su), full(sd)],
            out_specs=[pl.BlockSpec(memory_space=pl.ANY), pl.BlockSpec((tm, d), lambda i, zt: (i, 0))],
            scratch_shapes=[pltpu.SemaphoreType.DMA(())],
        ),
        out_shape=[jax.ShapeDtypeStruct((rows * ROW_SUB, 128), jnp.uint32),
                   jax.ShapeDtypeStruct((m, d), F32)],
        compiler_params=_cparams(1),
        name="moe_scatter_rows_shared",
    )(zero_tile, dest, xp, jnp.zeros((tile * ROW_SUB, 128), jnp.uint32), h, sg, su, sd)


def _expert_kernel(te_ref, nx_ref, gi_ref, nu_ref, x_ref, wg_hbm, wu_hbm, wd_hbm, o_ref,
                   wgf, wuf, wdf, wgb, wub, wdb, sem):
    t = pl.program_id(0)
    tc = jnp.minimum(t, nu_ref[0] - 1)
    prev = jnp.maximum(tc - 1, 0)
    new_expert = jnp.logical_or(t == 0, te_ref[tc] != te_ref[prev])
    slot = jnp.bitwise_and(gi_ref[tc], 1)

    def copies(e, s):
        return (pltpu.make_async_copy(wg_hbm.at[e], wgf.at[s], sem.at[s]),
                pltpu.make_async_copy(wu_hbm.at[e], wuf.at[s], sem.at[s]),
                pltpu.make_async_copy(wd_hbm.at[e], wdf.at[s], sem.at[s]))

    @pl.when(t == 0)
    def _():
        for c in copies(te_ref[0], 0):
            c.start()

    @pl.when(jnp.logical_and(new_expert, t < nu_ref[0]))
    def _():
        for c in copies(te_ref[tc], slot):
            c.wait()
        wgb[...] = wgf[slot].astype(BF16)
        wub[...] = wuf[slot].astype(BF16)
        wdb[...] = wdf[slot].astype(BF16)

        @pl.when(nx_ref[tc] >= 0)
        def _():
            for c in copies(nx_ref[tc], 1 - slot):
                c.start()

    @pl.when(t < nu_ref[0])
    def _():
        lo, hi = _load_rows(x_ref)
        d2 = lo.shape[1]
        gate = _dot(lo, wgb[0:d2, :]) + _dot(hi, wgb[d2:, :])
        up = _dot(lo, wub[0:d2, :]) + _dot(hi, wub[d2:, :])
        mid = (gate * jax.nn.sigmoid(gate) * up).astype(BF16)
        y = _dot(mid, wdb[...])
        _store_rows(o_ref, _pack_pair(y[:, :d2], y[:, d2:]))


def _experts(tile_expert, next_expert, group_idx, n_used, xs, wg, wu, wd, tile):
    rows = xs.shape[0] // ROW_SUB
    d = wg.shape[1]
    blk = (tile * ROW_SUB, 128)
    tile_map = lambda t, te, nx, gi, nu: (jnp.minimum(t, nu[0] - 1), 0)
    hbm = pl.BlockSpec(memory_space=pl.ANY)
    return pl.pallas_call(
        _expert_kernel,
        grid_spec=pltpu.PrefetchScalarGridSpec(
            num_scalar_prefetch=4,
            grid=(rows // tile,),
            in_specs=[pl.BlockSpec(blk, tile_map), hbm, hbm, hbm],
            out_specs=pl.BlockSpec(blk, tile_map),
            scratch_shapes=[pltpu.VMEM((2, d, EXPERT_DIM), F32), pltpu.VMEM((2, d, EXPERT_DIM), F32),
                            pltpu.VMEM((2, EXPERT_DIM, d), F32),
                            pltpu.VMEM((d, EXPERT_DIM), BF16), pltpu.VMEM((d, EXPERT_DIM), BF16),
                            pltpu.VMEM((EXPERT_DIM, d), BF16), pltpu.SemaphoreType.DMA((2,))],
        ),
        out_shape=jax.ShapeDtypeStruct(xs.shape, jnp.uint32),
        compiler_params=_cparams(1),
        name="moe_experts",
    )(tile_expert, next_expert, group_idx, n_used, xs, wg, wu, wd)


def _combine_kernel(d_ref, dn_ref, ys_hbm, w_ref, h_ref, o_ref, buf, sem, *, tm, nt):
    i = pl.program_id(0)
    slot = jnp.bitwise_and(i, 1)
    nxt = 1 - slot
    rs = ROW_SUB

    def start(idx_ref, n, s):
        for j in range(TOP_K):
            src = ys_hbm.at[pl.ds(pl.multiple_of(idx_ref[n * TOP_K + j], rs), rs)]
            pltpu.make_async_copy(src, buf.at[s, j, pl.ds(n * rs, rs)], sem.at[s]).start(priority=j % 2)

    def wait_all(s):
        for j in range(TOP_K):
            pltpu.make_async_copy(ys_hbm.at[pl.ds(0, tm * rs)], buf.at[s, j], sem.at[s]).wait()

    @pl.when(i == 0)
    def _():
        def body(n, c):
            for j in range(TOP_K):
                src = ys_hbm.at[pl.ds(pl.multiple_of(d_ref[n * TOP_K + j], rs), rs)]
                dst = buf.at[0, j, pl.ds(pl.multiple_of(n * rs, rs), rs)]
                pltpu.make_async_copy(src, dst, sem.at[0]).start(priority=j % 2)
            return c

        lax.fori_loop(0, tm, body, 0)

    per = tm // (rs * TOP_K)
    batches = iter(range(0, tm, per))

    def start_batch():
        n0 = next(batches)
        for n in range(n0, n0 + per):
            start(dn_ref, n, nxt)

    d2 = h_ref.shape[1] // 2
    o_ref[...] = h_ref[...]
    wait_all(slot)
    wts = w_ref[...]
    wj = [wts[:, j:j + 1] for j in range(TOP_K)]
    for s in range(rs):
        r_lo = jnp.zeros((tm, 128), F32)
        r_hi = jnp.zeros((tm, 128), F32)
        for j in range(TOP_K):
            start_batch()
            ylo, yhi = _unpack_pair(buf[slot, j, pl.ds(s, tm, stride=rs), :])
            r_lo = r_lo + wj[j] * ylo
            r_hi = r_hi + wj[j] * yhi
        o_ref[:, s * 128:(s + 1) * 128] += r_lo
        o_ref[:, d2 + s * 128:d2 + (s + 1) * 128] += r_hi

    @pl.when(i == nt - 1)
    def _():
        wait_all(nxt)


def _combine(dest, ys, wcol, h, tm):
    m, d = h.shape
    tm = min(tm, m)
    nt = m // tm
    idx = lambda f: pl.BlockSpec((tm * TOP_K,), f, memory_space=pltpu.SMEM)
    return pl.pallas_call(
        functools.partial(_combine_kernel, tm=tm, nt=nt),
        grid=(nt,),
        in_specs=[idx(lambda i: (i,)), idx(lambda i: (jnp.minimum(i + 1, nt - 1),)),
                  pl.BlockSpec(memory_space=pl.ANY),
                  pl.BlockSpec((tm, N_EXPERTS), lambda i: (i, 0)),
                  pl.BlockSpec((tm, d), lambda i: (i, 0))],
        out_specs=pl.BlockSpec((tm, d), lambda i: (i, 0)),
        out_shape=jax.ShapeDtypeStruct((m, d), F32),
        scratch_shapes=[pltpu.VMEM((2, TOP_K, tm * ROW_SUB, 128), jnp.uint32),
                        pltpu.SemaphoreType.DMA((2,))],
        compiler_params=_cparams(1),
        name="moe_combine",
    )(dest, dest, ys, wcol, h)


def _moe(h1, hnp, lg_t, router_bias, w_gate, w_up, w_down, ws_gate, ws_up, ws_down):
    n = h1.shape[0]
    tile = MOE_TILE
    eidx, rank, wcol, cnt = _router(lg_t, router_bias.reshape(-1, 1), 512)
    counts = cnt[:, 0].astype(jnp.int32)
    tiles_per = (counts + tile - 1) // tile
    tile_end = jnp.cumsum(tiles_per)
    tile_start = tile_end - tiles_per
    n_tiles = (n * TOP_K) // tile + N_EXPERTS
    rows = n_tiles * tile
    tile_expert = jnp.minimum(jnp.sum(tile_end[None, :] <= jnp.arange(n_tiles)[:, None], axis=1),
                              N_EXPERTS - 1).astype(jnp.int32)
    n_used = tile_end[-1:].astype(jnp.int32)
    ids = jnp.arange(N_EXPERTS)
    later = jnp.logical_and(ids[None, :] > ids[:, None], tiles_per[None, :] > 0)
    nxt_of = jnp.min(jnp.where(later, ids[None, :], N_EXPERTS), axis=1)
    nxt_of = jnp.where(nxt_of == N_EXPERTS, -1, nxt_of)
    ord_of = jnp.cumsum((tiles_per > 0).astype(jnp.int32)) - 1
    onehot = tile_expert[:, None] == ids[None, :]
    next_expert = jnp.sum(jnp.where(onehot, nxt_of[None, :], 0), axis=1).astype(jnp.int32)
    group_idx = jnp.sum(jnp.where(onehot, ord_of[None, :], 0), axis=1).astype(jnp.int32)
    last_tile = jnp.clip(tile_end - 1, 0, n_tiles - 1).astype(jnp.int32)
    dest = _dest((tile_start * tile).astype(jnp.int32), eidx, rank, 2048)
    dest = dest.T.reshape(-1)
    xs, hs = _scatter_rows(last_tile, dest, hnp, h1, ws_gate.astype(BF16), ws_up.astype(BF16),
                           ws_down.astype(BF16), rows, tile, 512)
    ys = _experts(tile_expert, next_expert, group_idx, n_used, xs, w_gate, w_up, w_down, tile)
    return _combine(dest, ys, wcol, hs, 256)


def _ple_kernel(h_ref, p_ref, gin_ref, wg_ref, bg_ref, wp_ref, gout_ref, o_ref):
    h = h_ref[...]
    gate = jax.nn.sigmoid(_dot(_rms(h, gin_ref[...]).astype(BF16), wg_ref[...]) + bg_ref[...])
    pp = _dot(p_ref[...].astype(BF16), wp_ref[...])
    o_ref[...] = h + _rms(pp * gate, gout_ref[...])


def _ple(h, p, g_in, w_g, b_g, w_p, g_out, tm):
    m, d = h.shape
    tm = min(tm, m)
    full = lambda a: pl.BlockSpec(a.shape, lambda i: (0,) * a.ndim)
    return pl.pallas_call(
        _ple_kernel,
        grid=(m // tm,),
        in_specs=[pl.BlockSpec((tm, d), lambda i: (i, 0)),
                  pl.BlockSpec((tm, p.shape[1]), lambda i: (i, 0)),
                  full(g_in), full(w_g), full(b_g), full(w_p), full(g_out)],
        out_specs=pl.BlockSpec((tm, d), lambda i: (i, 0)),
        out_shape=jax.ShapeDtypeStruct((m, d), F32),
        compiler_params=_cparams(1),
        name="ple",
    )(h, p, g_in, w_g, b_g, w_p, g_out)


def _pad_rows(a, rows):
    return jnp.pad(a, ((0, rows - a.shape[0]), (0, 0)))


def _row(a):
    return a.reshape(1, -1)


def _proj_layout(a):
    def padded(x, width):
        return jnp.pad(x, [(0, 0)] * (x.ndim - 1) + [(0, width - x.shape[-1])])

    parts = [padded(a[..., :MLA_COLS], MLA_COLS_PAD), a[..., MLA_COLS:MLA_COLS + 3 * RW_C]]
    off = MLA_COLS + 3 * RW_C
    for width, wide in ((RW_DECAY_LORA, 128), (RW_A_LORA, 128), (RW_GATE_LORA, 256)):
        parts.append(padded(a[..., off:off + width], wide))
        off += width
    return jnp.concatenate(parts, axis=-1)


def _mla_from_proj(proj, positions, g_qa, w_uq, g_kva, w_ukv, g_qn, g_kn):
    b, t, _ = proj.shape
    wq = w_uq.reshape(MLA_Q_LORA, MLA_HEADS, MLA_QK)
    wq = jnp.concatenate([wq[:, :, :MLA_NOPE].reshape(MLA_Q_LORA, -1),
                          wq[:, :, MLA_NOPE:].reshape(MLA_Q_LORA, -1)], axis=1).astype(BF16)
    half = MLA_ROPE // 2
    inv = ROPE_THETA ** (-jnp.arange(half, dtype=F32) / half)
    invf = _row(jnp.concatenate([inv, inv]))
    sgn = _row(jnp.concatenate([-jnp.ones(half, F32), jnp.ones(half, F32)]))
    q, k, v = _mla_prep(proj, positions.reshape(b, t, 1), invf, sgn, _row(g_qa), wq, _row(g_kva),
                        w_ukv.astype(BF16), _row(g_qn), _row(g_kn), 512)
    return _attention(q, k, v, 1024, 512, 2)


def _rwkv_from_proj(proj, mu, w0, w_w2, a0, a_w2, g_w2, k_k, k_a, r_k, ln_w, ln_b):
    head_of = jnp.arange(RW_C) // RW_HEAD
    bd = (head_of[:, None] == head_of[None, :]).astype(BF16)
    mu_all = jnp.concatenate([jnp.zeros((MLA_COLS,), F32), mu])
    rr, ww, kx, vx, kkx, bx, gx = _rw_prep(
        proj, _row(_proj_layout(mu_all)), _row(w0), _pad_rows(w_w2, 128).astype(BF16), _row(a0),
        _pad_rows(a_w2, 128).astype(BF16), _pad_rows(g_w2, 256).astype(BF16), _row(k_k), _row(k_a),
        bd, 256)
    return _rw_chunk(rr, ww, kx, vx, kkx, bx, gx, _row(r_k), _row(ln_w), _row(ln_b))


def _layer(h, p, positions, g_mix, w_in, mla_g_qa, mla_w_uq, mla_g_kva, mla_w_ukv, mla_g_qn,
           mla_g_kn, rw_mu, rw_w0, rw_w_w2, rw_a0, rw_a_w2, rw_g_w2, rw_k_k, rw_k_a, rw_r_k,
           rw_ln_w, rw_ln_b, w_out, g_ffn, w_router, router_bias, w_gate, w_up, w_down, ws_gate,
           ws_up, ws_down, g_ple_in, w_ple_gate, b_ple_gate, w_ple_proj, g_ple_out):
    b, t, d = h.shape
    n = b * t
    x2 = h.reshape(n, d)

    proj = _in_proj(x2, _row(g_mix), _proj_layout(w_in).astype(BF16), 1024).reshape(b, t, -1)
    y_mla = _mla_from_proj(proj, positions, mla_g_qa, mla_w_uq, mla_g_kva, mla_w_ukv, mla_g_qn, mla_g_kn)
    y_rw = _rwkv_from_proj(proj, rw_mu, rw_w0, rw_w_w2, rw_a0, rw_a_w2, rw_g_w2, rw_k_k, rw_k_a, rw_r_k,
                           rw_ln_w, rw_ln_b)

    h1, hnp, lg_t = _out_proj(y_mla.reshape(n, -1), y_rw.reshape(n, -1), x2, w_out.astype(BF16),
                              _row(g_ffn), w_router, 512)
    h2 = _moe(h1, hnp, lg_t, router_bias, w_gate, w_up, w_down, ws_gate, ws_up, ws_down)

    h3 = _ple(h2, p.reshape(n, -1), _row(g_ple_in), w_ple_gate.astype(BF16), _row(b_ple_gate),
              w_ple_proj.astype(BF16), _row(g_ple_out), 512)
    return h3.reshape(b, t, d)


def kernel(x, p, positions, g_mix, w_in, mla_g_qa, mla_w_uq, mla_g_kva, mla_w_ukv, mla_g_qn, mla_g_kn, rw_mu, rw_w0, rw_w_w2, rw_a0, rw_a_w2, rw_g_w2, rw_k_k, rw_k_a, rw_r_k, rw_ln_w, rw_ln_b, w_out, g_ffn, w_router, router_bias, w_gate, w_up, w_down, ws_gate, ws_up, ws_down, g_ple_in, w_ple_gate, b_ple_gate, w_ple_proj, g_ple_out):
    params = (g_mix, w_in, mla_g_qa, mla_w_uq, mla_g_kva, mla_w_ukv, mla_g_qn, mla_g_kn, rw_mu,
              rw_w0, rw_w_w2, rw_a0, rw_a_w2, rw_g_w2, rw_k_k, rw_k_a, rw_r_k, rw_ln_w, rw_ln_b,
              w_out, g_ffn, w_router, router_bias, w_gate, w_up, w_down, ws_gate, ws_up, ws_down,
              g_ple_in, w_ple_gate, b_ple_gate, w_ple_proj, g_ple_out)
    h = x
    for i in range(g_mix.shape[0]):
        h = _layer(h, p[i], positions, *[a[i] for a in params])
    return h
```

```python
import functools

import jax
import jax.numpy as jnp
from jax import lax
from jax.experimental import pallas as pl
from jax.experimental.pallas import tpu as pltpu

F32 = jnp.float32
BF16 = jnp.bfloat16

D_MODEL = 2048
PLE_DIM = 256
RMS_EPS = 1e-6

MLA_HEADS = 8
MLA_NOPE = 128
MLA_ROPE = 64
MLA_QK = MLA_NOPE + MLA_ROPE
MLA_V = 128
MLA_Q_LORA = 512
MLA_KV_LORA = 256
ROPE_THETA = 10000.0
MLA_COLS = MLA_Q_LORA + MLA_KV_LORA + MLA_ROPE
MLA_COLS_PAD = 1024

RW_HEADS = 16
RW_HEAD = 64
RW_C = RW_HEADS * RW_HEAD
RW_DECAY_LORA = 64
RW_A_LORA = 64
RW_GATE_LORA = 160
RW_GN_EPS = 64e-5
RW_LORA_PAD = 512
PROJ_COLS = MLA_COLS_PAD + 3 * RW_C + RW_LORA_PAD
PROJ_TN = 1536
RW_CHUNK = 64

N_EXPERTS = 64
TOP_K = 8
N_GROUPS = 8
TOPK_GROUPS = 4
EXPERT_DIM = 512
ROUTED_SCALE = 2.5
MOE_TILE = 512
ROW_SUB = D_MODEL // 2 // 128

VMEM_LIMIT = 56 * 1024 * 1024


def _cparams(n_axes):
    return pltpu.CompilerParams(dimension_semantics=("arbitrary",) * n_axes,
                                vmem_limit_bytes=VMEM_LIMIT)


def _rms(x, g):
    ms = jnp.mean(x * x, axis=-1, keepdims=True)
    return x * lax.rsqrt(ms + RMS_EPS) * g


def _dot(a, b):
    return jnp.dot(a, b, preferred_element_type=F32)


def _dot_nt(a, b):
    return lax.dot_general(a, b, (((1,), (1,)), ((), ())), preferred_element_type=F32)


def _dot_tn(a, b):
    return lax.dot_general(a, b, (((0,), (0,)), ((), ())), preferred_element_type=F32)


def _split2(x):
    hi = x.astype(BF16)
    lo = (x - hi.astype(F32)).astype(BF16)
    return hi, lo


def _in_proj_kernel(x_ref, g_ref, w_ref, o_ref, xn_ref):
    @pl.when(pl.program_id(1) == 0)
    def _():
        xn_ref[...] = _rms(x_ref[...], g_ref[...]).astype(BF16)

    o_ref[...] = _dot(xn_ref[...], w_ref[...]).astype(o_ref.dtype)


def _in_proj(x, g, w, tm):
    m, k = x.shape
    n = w.shape[1]
    tn = PROJ_TN
    tm = min(tm, m)
    return pl.pallas_call(
        _in_proj_kernel,
        grid=(m // tm, n // tn),
        in_specs=[pl.BlockSpec((tm, k), lambda i, j: (i, 0)),
                  pl.BlockSpec((1, k), lambda i, j: (0, 0)),
                  pl.BlockSpec((k, tn), lambda i, j: (0, j))],
        out_specs=pl.BlockSpec((tm, tn), lambda i, j: (i, j)),
        out_shape=jax.ShapeDtypeStruct((m, n), BF16),
        scratch_shapes=[pltpu.VMEM((tm, k), BF16)],
        compiler_params=_cparams(2),
        name="in_proj",
    )(x, g, w)


def _mla_prep_kernel(pm_ref, pos_ref, invf_ref, sgn_ref, gqa_ref, wuq_ref, gkva_ref, wukv_ref,
                     gqn_ref, gkn_ref, q_ref, k_ref, v_ref):
    pm = pm_ref[0].astype(F32)
    ang = pos_ref[0].astype(F32) * invf_ref[...]
    cos = jnp.cos(ang)
    sin = jnp.sin(ang) * sgn_ref[...]

    def rope(xr):
        half = MLA_ROPE // 2
        swapped = jnp.concatenate([xr[:, half:], xr[:, :half]], axis=1)
        return xr * cos + swapped * sin

    scale = MLA_QK ** -0.5 * 1.4426950408889634
    gqn = gqn_ref[...]
    gkn = gkn_ref[...]

    cq = _rms(pm[:, :MLA_Q_LORA], gqa_ref[...]).astype(BF16)
    q = _dot(cq, wuq_ref[...])
    ckv = _rms(pm[:, MLA_Q_LORA:MLA_Q_LORA + MLA_KV_LORA], gkva_ref[...]).astype(BF16)
    kv = _dot(ckv, wukv_ref[...])
    kpe = pm[:, MLA_Q_LORA + MLA_KV_LORA:MLA_COLS]
    kpe_ssq = jnp.sum(kpe * kpe, axis=-1, keepdims=True)
    kpe_rot = rope(kpe * gkn[:, MLA_NOPE:])

    for h in range(MLA_HEADS):
        qn = q[:, h * MLA_NOPE:(h + 1) * MLA_NOPE]
        qr = q[:, MLA_HEADS * MLA_NOPE + h * MLA_ROPE:MLA_HEADS * MLA_NOPE + (h + 1) * MLA_ROPE]
        ssq = jnp.sum(qn * qn, axis=-1, keepdims=True) + jnp.sum(qr * qr, axis=-1, keepdims=True)
        inv = lax.rsqrt(ssq * (1.0 / MLA_QK) + RMS_EPS) * scale
        q_ref[0, h, :, 0:MLA_NOPE] = (qn * inv * gqn[:, :MLA_NOPE]).astype(BF16)
        q_ref[0, h, :, MLA_NOPE:MLA_QK] = (rope(qr * gqn[:, MLA_NOPE:]) * inv).astype(BF16)

        kn = kv[:, h * 256:h * 256 + MLA_NOPE]
        ssq = jnp.sum(kn * kn, axis=-1, keepdims=True) + kpe_ssq
        inv = lax.rsqrt(ssq * (1.0 / MLA_QK) + RMS_EPS)
        k_ref[0, h, :, 0:MLA_NOPE] = (kn * inv * gkn[:, :MLA_NOPE]).astype(BF16)
        k_ref[0, h, :, MLA_NOPE:MLA_QK] = (kpe_rot * inv).astype(BF16)
        v_ref[0, h, :, 0:MLA_V] = kv[:, h * 256 + MLA_NOPE:(h + 1) * 256].astype(BF16)
        v_ref[0, h, :, MLA_V:] = jnp.ones((kv.shape[0], MLA_V), BF16)


def _mla_prep(pm, pos, invf, sgn, g_qa, w_uq, g_kva, w_ukv, g_qn, g_kn, tm):
    b, t, _ = pm.shape
    tm = min(tm, t)
    full = lambda a: pl.BlockSpec(a.shape, lambda i, j: (0,) * a.ndim)
    qk_shape = jax.ShapeDtypeStruct((b, MLA_HEADS, t, MLA_QK), BF16)
    return pl.pallas_call(
        _mla_prep_kernel,
        grid=(b, t // tm),
        in_specs=[pl.BlockSpec((1, tm, MLA_COLS_PAD), lambda i, j: (i, j, 0)),
                  pl.BlockSpec((1, tm, 1), lambda i, j: (i, j, 0)),
                  full(invf), full(sgn), full(g_qa), full(w_uq), full(g_kva), full(w_ukv),
                  full(g_qn), full(g_kn)],
        out_specs=[pl.BlockSpec((1, MLA_HEADS, tm, MLA_QK), lambda i, j: (i, 0, j, 0)),
                   pl.BlockSpec((1, MLA_HEADS, tm, MLA_QK), lambda i, j: (i, 0, j, 0)),
                   pl.BlockSpec((1, MLA_HEADS, tm, 2 * MLA_V), lambda i, j: (i, 0, j, 0))],
        out_shape=[qk_shape, qk_shape, jax.ShapeDtypeStruct((b, MLA_HEADS, t, 2 * MLA_V), BF16)],
        compiler_params=_cparams(2),
        name="mla_prep",
    )(pm, pos, invf, sgn, g_qa, w_uq, g_kva, w_ukv, g_qn, g_kn)


def _attn_kernel(q_ref, k_ref, v_ref, o_ref, m_ref, acc_ref, *, tq, tk, nh):
    qi = pl.program_id(2)
    m_ref[...] = jnp.full(m_ref.shape, -jnp.inf, F32)
    acc_ref[...] = jnp.zeros(acc_ref.shape, F32)
    heads = range(nh)
    def scores(j, r0, nr):
        start = pl.multiple_of(j * tk, tk)
        return [_dot_nt(q_ref[0, h, r0:r0 + nr, :], k_ref[0, h, pl.ds(start, tk), :]) for h in heads]

    def update(j, r0, nr, ss, triangle):
        start = pl.multiple_of(j * tk, tk)
        rows = slice(r0, r0 + nr)
        for h in heads:
            s = ss[h]
            if triangle:
                keep = (lax.broadcasted_iota(jnp.int32, (nr, tk), 1)
                        <= lax.broadcasted_iota(jnp.int32, (nr, tk), 0))
                s = jnp.where(keep, s, -jnp.inf)
            m_old = m_ref[h, rows, :]
            m_new = jnp.maximum(m_old, jnp.max(s, axis=-1, keepdims=True))
            alpha = jnp.exp2(m_old - m_new)
            p = jnp.exp2(s - jnp.concatenate([m_new] * (tk // 128), axis=1))
            pv = _dot(p.astype(BF16), v_ref[0, h, pl.ds(start, tk), :])
            acc_ref[h, rows, :] = jnp.concatenate([alpha, alpha], axis=1) * acc_ref[h, rows, :] + pv
            m_ref[h, rows, :] = m_new

    per = tq // tk
    n_full = qi * per

    def body(jj, c):
        ss = [scores(jj * per + d, 0, tq) for d in range(per)]
        for d in range(per):
            update(jj * per + d, 0, tq, ss[d], False)
        return c

    lax.fori_loop(0, qi, body, 0)
    blocks = []
    for d in range(per):
        blocks.append((n_full + d, d * tk, tk, True))
        if d + 1 < per:
            blocks.append((n_full + d, (d + 1) * tk, tq - (d + 1) * tk, False))
    ss = [scores(j, r0, nr) for j, r0, nr, _ in blocks]
    for (j, r0, nr, tri), s in zip(blocks, ss):
        update(j, r0, nr, s, tri)
    for h in heads:
        acc = acc_ref[h]
        o_ref[0, :, h * MLA_V:(h + 1) * MLA_V] = (acc[:, :MLA_V] / acc[:, MLA_V:]).astype(o_ref.dtype)


def _attention(q, k, v, tq, tk, nh):
    b, h, t, _ = q.shape
    tq = min(tq, t)
    tk = min(tk, tq)
    return pl.pallas_call(
        functools.partial(_attn_kernel, tq=tq, tk=tk, nh=nh),
        grid=(b, h // nh, t // tq),
        in_specs=[pl.BlockSpec((1, nh, tq, MLA_QK), lambda i, j, n: (i, j, n, 0)),
                  pl.BlockSpec((1, nh, t, MLA_QK), lambda i, j, n: (i, j, 0, 0)),
                  pl.BlockSpec((1, nh, t, 2 * MLA_V), lambda i, j, n: (i, j, 0, 0))],
        out_specs=pl.BlockSpec((1, tq, nh * MLA_V), lambda i, j, n: (i, n, j)),
        out_shape=jax.ShapeDtypeStruct((b, t, h * MLA_V), BF16),
        scratch_shapes=[pltpu.VMEM((nh, tq, 128), F32), pltpu.VMEM((nh, tq, 2 * MLA_V), F32)],
        compiler_params=_cparams(3),
        name="mla_attention",
    )(q, k, v)


def _rw_prep_kernel(r_ref, k_ref, v_ref, l_ref, rp_ref, kp_ref, vp_ref, lp_ref, mu_ref, w0_ref,
                    ww2_ref, a0_ref, aw2_ref, gw2_ref, kk_ref, ka_ref, bd_ref,
                    r_o, w_o, k_o, v_o, kk_o, b_o, g_o):
    first = pl.program_id(1) == 0
    c = RW_C

    def shifted(ref, pref, mu):
        z = ref[0].astype(F32)
        prev = pref[0][15:16, :].astype(F32)
        prev = jnp.where(first, jnp.zeros_like(prev), prev)
        row = lax.broadcasted_iota(jnp.int32, z.shape, 0)
        zs = jnp.where(row == 0, prev, pltpu.roll(z, 1, axis=0))
        return z + (zs - z) * mu

    o = MLA_COLS_PAD
    r = shifted(r_ref, rp_ref, mu_ref[:, o:o + c])
    k = shifted(k_ref, kp_ref, mu_ref[:, o + c:o + 2 * c])
    v = shifted(v_ref, vp_ref, mu_ref[:, o + 2 * c:o + 3 * c])
    lz = shifted(l_ref, lp_ref, mu_ref[:, o + 3 * c:])
    wl = lz[:, 0:128]
    al = lz[:, 128:256]
    gl = lz[:, 256:512]

    f = w0_ref[...] + _dot(jnp.tanh(wl).astype(BF16), ww2_ref[...])
    softplus_neg_f = jnp.maximum(-f, 0.0) + jnp.log(1.0 + jnp.exp(-jnp.abs(f)))
    w = -softplus_neg_f - 0.5
    w_o[0] = -jnp.exp(w)
    a = jax.nn.sigmoid(a0_ref[...] + _dot(al.astype(BF16), aw2_ref[...]))
    g_o[0] = _dot(jax.nn.sigmoid(gl).astype(BF16), gw2_ref[...]).astype(g_o.dtype)

    kk = k * kk_ref[...]
    hi, lo = _split2(kk * kk)
    ssq = _dot(hi, bd_ref[...]) + _dot(lo, bd_ref[...])
    kk = kk * lax.rsqrt(jnp.maximum(ssq, 1e-24))
    r_o[0] = r.astype(r_o.dtype)
    k_o[0] = (k * (1.0 + (a - 1.0) * ka_ref[...])).astype(k_o.dtype)
    v_o[0] = v.astype(v_o.dtype)
    kk_o[0] = kk.astype(kk_o.dtype)
    b_o[0] = (kk * a).astype(b_o.dtype)


def _rw_prep(proj, mu, w0, ww2, a0, aw2, gw2, k_k, k_a, bd, tm):
    b, t, _ = proj.shape
    tm = min(tm, t)
    c = RW_C
    full = lambda a: pl.BlockSpec(a.shape, lambda i, j: (0,) * a.ndim)
    cur = lambda w, cb: pl.BlockSpec((1, tm, w), lambda i, j: (i, j, cb))
    prev = lambda w, cb: pl.BlockSpec((1, 16, w), lambda i, j: (i, jnp.maximum(j * (tm // 16) - 1, 0), cb))
    lora_cb = (MLA_COLS_PAD + 3 * c) // RW_LORA_PAD
    groups = [(c, 1), (c, 2), (c, 3), (RW_LORA_PAD, lora_cb)]
    out = [jax.ShapeDtypeStruct((b, t, c), F32 if i == 1 else BF16) for i in range(7)]
    ospec = pl.BlockSpec((1, tm, c), lambda i, j: (i, j, 0))
    return pl.pallas_call(
        _rw_prep_kernel,
        grid=(b, t // tm),
        in_specs=[cur(*g) for g in groups] + [prev(*g) for g in groups]
        + [full(mu), full(w0), full(ww2), full(a0), full(aw2), full(gw2), full(k_k), full(k_a), full(bd)],
        out_specs=[ospec] * 7,
        out_shape=out,
        compiler_params=_cparams(2),
        name="rwkv_prep",
    )(*([proj] * 8), mu, w0, ww2, a0, aw2, gw2, k_k, k_a, bd)


def _rw_chunk_kernel(r_ref, w_ref, k_ref, v_ref, kk_ref, b_ref, g_ref, rk_ref, lnw_ref, lnb_ref,
                     o_ref, s_ref):
    cs = RW_CHUNK
    hd = RW_HEAD

    @pl.when(pl.program_id(1) == 0)
    def _():
        s_ref[...] = jnp.zeros(s_ref.shape, F32)

    r = r_ref[0].astype(F32)
    w = w_ref[0]
    k = k_ref[0].astype(F32)
    v = v_ref[0].astype(F32)
    kk = kk_ref[0].astype(F32)
    b = b_ref[0].astype(F32)
    pairs = range(RW_HEADS // 2)
    pw = 2 * hd

    row = lax.broadcasted_iota(jnp.int32, (cs, pw), 0)
    col = lax.broadcasted_iota(jnp.int32, (cs, pw), 1)
    colh = jnp.bitwise_and(col, hd - 1)
    lo = col < hd
    strict = row > colh
    eye = jnp.where(row == colh, 1.0, 0.0).astype(F32)
    row2 = lax.broadcasted_iota(jnp.int32, (cs, 2 * pw), 0)
    col2 = lax.broadcasted_iota(jnp.int32, (cs, 2 * pw), 1)
    incl2 = row2 >= jnp.bitwise_and(col2, hd - 1)
    row3 = lax.broadcasted_iota(jnp.int32, (cs, 3 * cs), 0)
    col3 = lax.broadcasted_iota(jnp.int32, (cs, 3 * cs), 1)
    tri3 = jnp.where(row3 >= jnp.bitwise_and(col3, cs - 1), 1.0, 0.0).astype(BF16)

    def bdiag(x):
        zero = jnp.zeros_like(x)
        return jnp.concatenate([jnp.where(lo, x, zero), jnp.where(lo, zero, x)], axis=0)

    def cat3_lhs(x):
        hi = x.astype(BF16)
        return jnp.concatenate([hi, (x - hi.astype(F32)).astype(BF16), hi], axis=1)

    def cat3_rhs(x):
        hi, lw = _split2(x)
        bh = bdiag(hi)
        return jnp.concatenate([bh, bh, bdiag(lw)], axis=0)

    w1 = w.astype(BF16)
    w2 = (w - w1.astype(F32)).astype(BF16)
    w3 = (w - w1.astype(F32) - w2.astype(F32)).astype(BF16)
    logp = _dot(tri3, jnp.concatenate([w1, w2, w3], axis=0))
    logp_end = logp[cs - 1:cs, :]
    p_tail = jnp.exp(logp_end - logp)
    p_end = jnp.exp(logp_end)
    p_inv = jnp.exp(-logp)

    a_t = -kk * jnp.exp(logp - w)
    r_t = r * jnp.exp(logp)
    b_t = b * p_inv
    k_t = k * p_inv
    b_h = b * p_tail
    k_h = k * p_tail

    sls = [slice(p * pw, (p + 1) * pw) for p in pairs]
    lhs = [jnp.concatenate([a_t[:, sl], r_t[:, sl]], axis=0).astype(BF16) for sl in sls]
    rhs = [jnp.concatenate([bdiag(b_t[:, sl].astype(BF16)), bdiag(k_t[:, sl].astype(BF16))], axis=0)
           for sl in sls]
    gm = [_dot_nt(lhs[p], rhs[p]) for p in pairs]
    s0 = [s_ref[p] for p in pairs]
    ls = [_dot_nt(lhs[p], bdiag(s0[p].astype(BF16))) for p in pairs]
    vb = [v[:, sl].astype(BF16) for sl in sls]
    bv = [bdiag(vb[p]) for p in pairs]
    a_ak = [jnp.where(strict, gm[p][:cs, pw:], 0.0).astype(BF16) for p in pairs]
    rhs_u = [ls[p][:cs] + _dot(a_ak[p], bv[p]) for p in pairs]

    x = [jnp.where(strict, gm[p][:cs, :pw], 0.0) for p in pairs]
    tm = [eye + x[p] for p in pairs]
    x = [_dot(cat3_lhs(x[p]), cat3_rhs(x[p])) for p in pairs]
    for _ in range(4):
        both = [_dot(cat3_lhs(jnp.concatenate([x[p], tm[p]], axis=0)), cat3_rhs(x[p])) for p in pairs]
        x = [both[p][:cs] for p in pairs]
        tm = [tm[p] + both[p][cs:] for p in pairs]
    tm = [tm[p] + _dot(cat3_lhs(tm[p]), cat3_rhs(x[p])) for p in pairs]

    u = [_dot(cat3_lhs(tm[p]), cat3_rhs(rhs_u[p])) for p in pairs]
    ub = [u[p].astype(BF16) for p in pairs]
    a_r = [jnp.where(incl2, gm[p][cs:, :], 0.0).astype(BF16) for p in pairs]
    y = [ls[p][cs:] + _dot(a_r[p], jnp.concatenate([bdiag(ub[p]), bv[p]], axis=0)) for p in pairs]
    for p in pairs:
        uv = jnp.concatenate([ub[p], vb[p]], axis=0)
        bk = jnp.concatenate([b_h[:, sls[p]], k_h[:, sls[p]]], axis=0).astype(BF16)
        full = _dot_tn(uv, bk)
        s_ref[p] = s0[p] * p_end[:, sls[p]] + jnp.where(lo, full[:hd], full[hd:])

    def head_sum(t):
        s_lo = jnp.sum(jnp.where(lo, t, 0.0), axis=-1, keepdims=True)
        s_all = jnp.sum(t, axis=-1, keepdims=True)
        return jnp.where(lo, s_lo, s_all - s_lo)

    rk_all = r * k * rk_ref[...]
    g = g_ref[0].astype(F32)
    for p in pairs:
        sl = sls[p]
        yc = y[p] - head_sum(y[p]) * (1.0 / hd)
        var = head_sum(yc * yc) * (1.0 / hd)
        yn = yc * lax.rsqrt(var + RW_GN_EPS) * lnw_ref[:, sl] + lnb_ref[:, sl]
        bonus = head_sum(rk_all[:, sl]) * v[:, sl]
        o_ref[0, :, sl] = ((yn + bonus) * g[:, sl]).astype(o_ref.dtype)


def _rw_chunk(r, w, k, v, kk, bb, g, r_k, ln_w, ln_b):
    b, t, c = r.shape
    cs = RW_CHUNK
    full = lambda a: pl.BlockSpec(a.shape, lambda i, j: (0,) * a.ndim)
    spec = pl.BlockSpec((1, cs, c), lambda i, j: (i, j, 0))
    return pl.pallas_call(
        _rw_chunk_kernel,
        grid=(b, t // cs),
        in_specs=[spec] * 7 + [full(r_k), full(ln_w), full(ln_b)],
        out_specs=spec,
        out_shape=jax.ShapeDtypeStruct((b, t, c), BF16),
        scratch_shapes=[pltpu.VMEM((RW_HEADS // 2, RW_HEAD, 2 * RW_HEAD), F32)],
        compiler_params=_cparams(2),
        name="rwkv_chunk",
    )(r, w, k, v, kk, bb, g, r_k, ln_w, ln_b)


def _pack_pair(a, b):
    ua = lax.bitcast_convert_type(a.astype(BF16).astype(F32), jnp.uint32)
    ub = lax.bitcast_convert_type(b.astype(BF16).astype(F32), jnp.uint32)
    return lax.shift_right_logical(ua, jnp.uint32(16)) | (ub & jnp.uint32(0xFFFF0000))


def _unpack_pair(u):
    lo = lax.bitcast_convert_type(lax.shift_left(u, jnp.uint32(16)), F32)
    hi = lax.bitcast_convert_type(u & jnp.uint32(0xFFFF0000), F32)
    return lo, hi


def _store_rows(ref, packed):
    m = packed.shape[0]
    for s in range(ROW_SUB):
        ref[pl.ds(s, m, stride=ROW_SUB), :] = packed[:, s * 128:(s + 1) * 128]


def _load_rows(ref):
    m = ref.shape[-2] // ROW_SUB
    los, his = [], []
    for s in range(ROW_SUB):
        lo, hi = _unpack_pair(ref[pl.ds(s, m, stride=ROW_SUB), :])
        los.append(lo.astype(BF16))
        his.append(hi.astype(BF16))
    return jnp.concatenate(los, axis=1), jnp.concatenate(his, axis=1)


def _out_proj_kernel(ya_ref, yr_ref, x_ref, w_ref, g_ref, wrh_ref, wrl_ref, h_ref, hn_ref, lg_ref):
    half = ya_ref.shape[1]
    acc = _dot(ya_ref[...], w_ref[0:half, :]) + _dot(yr_ref[...], w_ref[half:, :])
    h = x_ref[...] + acc
    h_ref[...] = h
    hn = _rms(h, g_ref[...])
    d2 = hn.shape[1] // 2
    _store_rows(hn_ref, _pack_pair(hn[:, :d2], hn[:, d2:]))
    hh, hl = _split2(hn)
    lg_ref[...] = (_dot_nt(wrh_ref[...], hh) + _dot_nt(wrh_ref[...], hl)
                   + _dot_nt(wrl_ref[...], hh))


def _out_proj(ya, yr, x, w, g, wr, tm):
    wr_hi, wr_lo = _split2(wr.T)
    m, d = x.shape
    tm = min(tm, m)
    full = lambda a: pl.BlockSpec(a.shape, lambda i: (0,) * a.ndim)
    return pl.pallas_call(
        _out_proj_kernel,
        grid=(m // tm,),
        in_specs=[pl.BlockSpec((tm, ya.shape[1]), lambda i: (i, 0)),
                  pl.BlockSpec((tm, yr.shape[1]), lambda i: (i, 0)),
                  pl.BlockSpec((tm, d), lambda i: (i, 0)),
                  full(w), full(g), full(wr_hi), full(wr_lo)],
        out_specs=[pl.BlockSpec((tm, d), lambda i: (i, 0)),
                   pl.BlockSpec((tm * ROW_SUB, 128), lambda i: (i, 0)),
                   pl.BlockSpec((N_EXPERTS, tm), lambda i: (0, i))],
        out_shape=[jax.ShapeDtypeStruct((m, d), F32), jax.ShapeDtypeStruct((m * ROW_SUB, 128), jnp.uint32),
                   jax.ShapeDtypeStruct((N_EXPERTS, m), F32)],
        compiler_params=_cparams(1),
        name="out_proj_router_logits",
    )(ya, yr, x, w, g, wr_hi, wr_lo)


def _router_kernel(lg_ref, bias_ref, e_ref, r_ref, w_ref, cnt_ref, carry_ref):
    tm = lg_ref.shape[1]
    gsz = N_EXPERTS // N_GROUPS

    @pl.when(pl.program_id(0) == 0)
    def _():
        carry_ref[...] = jnp.zeros(carry_ref.shape, F32)

    scores = jax.nn.sigmoid(lg_ref[...]).reshape(N_GROUPS, gsz, tm)
    biased = scores + bias_ref[...].reshape(N_GROUPS, gsz, 1)
    neg = -jnp.inf

    eidx = lax.broadcasted_iota(jnp.int32, biased.shape, 1)
    m1 = jnp.max(biased, axis=1, keepdims=True)
    first = jnp.min(jnp.where(biased == m1, eidx, gsz), axis=1, keepdims=True)
    m2 = jnp.max(jnp.where(eidx == first, neg, biased), axis=1, keepdims=True)
    gscore = m1 + m2

    gidx = lax.broadcasted_iota(jnp.int32, gscore.shape, 0)
    gsel = jnp.zeros(gscore.shape, jnp.bool_)
    work = gscore
    for _ in range(TOPK_GROUPS):
        best = jnp.max(work, axis=0, keepdims=True)
        pick = jnp.min(jnp.where(work == best, gidx, N_GROUPS), axis=0, keepdims=True)
        hit = gidx == pick
        gsel = jnp.logical_or(gsel, hit)
        work = jnp.where(hit, neg, work)

    flat = lax.broadcasted_iota(jnp.int32, biased.shape, 0) * gsz + eidx
    work = jnp.where(gsel, biased, neg)
    hits = []
    picks = []
    for _ in range(TOP_K):
        best = jnp.max(jnp.max(work, axis=1, keepdims=True), axis=0, keepdims=True)
        cand = jnp.where(work == best, flat, N_EXPERTS)
        pick = jnp.min(jnp.min(cand, axis=1, keepdims=True), axis=0, keepdims=True)
        hit = flat == pick
        hits.append(hit)
        picks.append(pick)
        work = jnp.where(hit, neg, work)

    def pick_value(hit, val):
        s = jnp.sum(jnp.sum(jnp.where(hit, val, 0.0), axis=1, keepdims=True), axis=0, keepdims=True)
        return s.reshape(1, tm)

    sel = jnp.zeros(biased.shape, F32)
    for hit in hits:
        sel = jnp.where(hit, 1.0, sel)
    sel2 = sel.reshape(N_EXPERTS, tm)
    r_i = lax.broadcasted_iota(jnp.int32, (tm, tm), 0)
    c_i = lax.broadcasted_iota(jnp.int32, (tm, tm), 1)
    upper = jnp.where(r_i < c_i, 1.0, 0.0).astype(BF16)
    rank = _dot(sel2.astype(BF16), upper) + carry_ref[...]
    carry = carry_ref[...] + jnp.sum(sel2, axis=1, keepdims=True)
    carry_ref[...] = carry
    cnt_ref[...] = carry
    rank3 = rank.reshape(N_GROUPS, gsz, tm)

    raw = [pick_value(hit, scores) for hit in hits]
    tot = raw[0]
    for x in raw[1:]:
        tot = tot + x
    wrow = lax.broadcasted_iota(jnp.int32, (N_EXPERTS, tm), 0)
    wt = jnp.zeros((N_EXPERTS, tm), F32)
    for j in range(TOP_K):
        e_ref[j:j + 1, :] = picks[j].reshape(1, tm)
        r_ref[j:j + 1, :] = pick_value(hits[j], rank3).astype(jnp.int32)
        wt = jnp.where(wrow == j, raw[j] / tot * ROUTED_SCALE, wt)
    w_ref[...] = wt.T


def _router(lg_t, bias, tm):
    e, m = lg_t.shape
    tm = min(tm, m)
    row = pl.BlockSpec((TOP_K, tm), lambda i: (0, i))
    return pl.pallas_call(
        _router_kernel,
        grid=(m // tm,),
        in_specs=[pl.BlockSpec((e, tm), lambda i: (0, i)),
                  pl.BlockSpec((e, 1), lambda i: (0, 0))],
        out_specs=[row, row, pl.BlockSpec((tm, e), lambda i: (i, 0)),
                   pl.BlockSpec((e, 1), lambda i: (0, 0))],
        out_shape=[jax.ShapeDtypeStruct((TOP_K, m), jnp.int32),
                   jax.ShapeDtypeStruct((TOP_K, m), jnp.int32),
                   jax.ShapeDtypeStruct((m, e), F32),
                   jax.ShapeDtypeStruct((e, 1), F32)],
        scratch_shapes=[pltpu.VMEM((e, 1), F32)],
        compiler_params=_cparams(1),
        name="router_topk",
    )(lg_t, bias)


def _dest_kernel(off_ref, e_ref, r_ref, d_ref):
    e = e_ref[...]
    acc = r_ref[...]
    for x in range(N_EXPERTS):
        acc = acc + jnp.where(e == x, off_ref[x], 0)
    d_ref[...] = acc * ROW_SUB


def _dest(off, eidx, rank, tm):
    k, m = eidx.shape
    tm = min(tm, m)
    spec = pl.BlockSpec((k, tm), lambda i: (0, i))
    return pl.pallas_call(
        _dest_kernel,
        grid=(m // tm,),
        in_specs=[pl.BlockSpec(memory_space=pltpu.SMEM), spec, spec],
        out_specs=spec,
        out_shape=jax.ShapeDtypeStruct((k, m), jnp.int32),
        compiler_params=_cparams(1),
        name="moe_dest_rows",
    )(off, eidx, rank)


def _scatter_kernel(zt_ref, d_ref, x_ref, z_ref, h_ref, sg_ref, su_ref, sd_ref, xs_hbm, hs_ref, sem,
                    *, tm, tile):
    i = pl.program_id(0)
    rs = ROW_SUB

    @pl.when(i == 0)
    def _():
        def zero(e, c):
            start = pl.multiple_of(zt_ref[e] * (tile * rs), tile * rs)
            pltpu.make_async_copy(z_ref, xs_hbm.at[pl.ds(start, tile * rs)], sem).start()
            return c

        lax.fori_loop(0, N_EXPERTS, zero, 0)

        def zwait(e, c):
            pltpu.make_async_copy(z_ref, xs_hbm.at[pl.ds(0, tile * rs)], sem).wait()
            return c

        lax.fori_loop(0, N_EXPERTS, zwait, 0)

    def body(n, c):
        src = x_ref.at[pl.ds(pl.multiple_of(n * rs, rs), rs)]
        for j in range(TOP_K):
            dst = xs_hbm.at[pl.ds(pl.multiple_of(d_ref[n * TOP_K + j], rs), rs)]
            pltpu.make_async_copy(src, dst, sem).start(priority=j % 2)
        return c

    q4 = tm // 4
    lax.fori_loop(0, q4, body, 0)
    lo, hi = _load_rows(x_ref)
    d2 = lo.shape[1]
    gate = _dot(lo, sg_ref[0:d2, :]) + _dot(hi, sg_ref[d2:, :])
    lax.fori_loop(q4, 2 * q4, body, 0)
    up = _dot(lo, su_ref[0:d2, :]) + _dot(hi, su_ref[d2:, :])
    mid = (gate * jax.nn.sigmoid(gate) * up).astype(BF16)
    lax.fori_loop(2 * q4, 3 * q4, body, 0)
    hs_ref[...] = h_ref[...] + _dot(mid, sd_ref[...])
    lax.fori_loop(3 * q4, tm, body, 0)

    for j in range(TOP_K):
        pltpu.make_async_copy(x_ref, xs_hbm.at[pl.ds(0, tm * rs)], sem).wait()


def _scatter_rows(zero_tile, dest, xp, h, sg, su, sd, rows, tile, tm):
    m, d = h.shape
    tm = min(tm, m)
    full = lambda a: pl.BlockSpec(a.shape, lambda i, zt: (0,) * a.ndim)
    return pl.pallas_call(
        functools.partial(_scatter_kernel, tm=tm, tile=tile),
        grid_spec=pltpu.PrefetchScalarGridSpec(
            num_scalar_prefetch=1,
            grid=(m // tm,),
            in_specs=[pl.BlockSpec((tm * TOP_K,), lambda i, zt: (i,), memory_space=pltpu.SMEM),
                      pl.BlockSpec((tm * ROW_SUB, 128), lambda i, zt: (i, 0)),
                      pl.BlockSpec((tile * ROW_SUB, 128), lambda i, zt: (0, 0)),
                      pl.BlockSpec((tm, d), lambda i, zt: (i, 0)),
                      full(sg), full(su), full(sd)],
            out_specs=[pl.BlockSpec(memory_space=pl.ANY), pl.BlockSpec((tm, d), lambda i, zt: (i, 0))],
            scratch_shapes=[pltpu.SemaphoreType.DMA(())],
        ),
        out_shape=[jax.ShapeDtypeStruct((rows * ROW_SUB, 128), jnp.uint32),
                   jax.ShapeDtypeStruct((m, d), F32)],
        compiler_params=_cparams(1),
        name="moe_scatter_rows_shared",
    )(zero_tile, dest, xp, jnp.zeros((tile * ROW_SUB, 128), jnp.uint32), h, sg, su, sd)


def _expert_kernel(te_ref, nx_ref, gi_ref, nu_ref, x_ref, wg_hbm, wu_hbm, wd_hbm, o_ref,
                   wgf, wuf, wdf, wgb, wub, wdb, sem):
    t = pl.program_id(0)
    tc = jnp.minimum(t, nu_ref[0] - 1)
    prev = jnp.maximum(tc - 1, 0)
    new_expert = jnp.logical_or(t == 0, te_ref[tc] != te_ref[prev])
    slot = jnp.bitwise_and(gi_ref[tc], 1)

    def copies(e, s):
        return (pltpu.make_async_copy(wg_hbm.at[e], wgf.at[s], sem.at[s]),
                pltpu.make_async_copy(wu_hbm.at[e], wuf.at[s], sem.at[s]),
                pltpu.make_async_copy(wd_hbm.at[e], wdf.at[s], sem.at[s]))

    @pl.when(t == 0)
    def _():
        for c in copies(te_ref[0], 0):
            c.start()

    @pl.when(jnp.logical_and(new_expert, t < nu_ref[0]))
    def _():
        for c in copies(te_ref[tc], slot):
            c.wait()
        wgb[...] = wgf[slot].astype(BF16)
        wub[...] = wuf[slot].astype(BF16)
        wdb[...] = wdf[slot].astype(BF16)

        @pl.when(nx_ref[tc] >= 0)
        def _():
            for c in copies(nx_ref[tc], 1 - slot):
                c.start()

    @pl.when(t < nu_ref[0])
    def _():
        lo, hi = _load_rows(x_ref)
        d2 = lo.shape[1]
        gate = _dot(lo, wgb[0:d2, :]) + _dot(hi, wgb[d2:, :])
        up = _dot(lo, wub[0:d2, :]) + _dot(hi, wub[d2:, :])
        mid = (gate * jax.nn.sigmoid(gate) * up).astype(BF16)
        y = _dot(mid, wdb[...])
        _store_rows(o_ref, _pack_pair(y[:, :d2], y[:, d2:]))


def _experts(tile_expert, next_expert, group_idx, n_used, xs, wg, wu, wd, tile):
    rows = xs.shape[0] // ROW_SUB
    d = wg.shape[1]
    blk = (tile * ROW_SUB, 128)
    tile_map = lambda t, te, nx, gi, nu: (jnp.minimum(t, nu[0] - 1), 0)
    hbm = pl.BlockSpec(memory_space=pl.ANY)
    return pl.pallas_call(
        _expert_kernel,
        grid_spec=pltpu.PrefetchScalarGridSpec(
            num_scalar_prefetch=4,
            grid=(rows // tile,),
            in_specs=[pl.BlockSpec(blk, tile_map), hbm, hbm, hbm],
            out_specs=pl.BlockSpec(blk, tile_map),
            scratch_shapes=[pltpu.VMEM((2, d, EXPERT_DIM), F32), pltpu.VMEM((2, d, EXPERT_DIM), F32),
                            pltpu.VMEM((2, EXPERT_DIM, d), F32),
                            pltpu.VMEM((d, EXPERT_DIM), BF16), pltpu.VMEM((d, EXPERT_DIM), BF16),
                            pltpu.VMEM((EXPERT_DIM, d), BF16), pltpu.SemaphoreType.DMA((2,))],
        ),
        out_shape=jax.ShapeDtypeStruct(xs.shape, jnp.uint32),
        compiler_params=_cparams(1),
        name="moe_experts",
    )(tile_expert, next_expert, group_idx, n_used, xs, wg, wu, wd)


def _combine_kernel(d_ref, dn_ref, ys_hbm, w_ref, h_ref, o_ref, buf, sem, *, tm, nt):
    i = pl.program_id(0)
    slot = jnp.bitwise_and(i, 1)
    nxt = 1 - slot
    rs = ROW_SUB

    def start(idx_ref, n, s):
        for j in range(TOP_K):
            src = ys_hbm.at[pl.ds(pl.multiple_of(idx_ref[n * TOP_K + j], rs), rs)]
            pltpu.make_async_copy(src, buf.at[s, j, pl.ds(n * rs, rs)], sem.at[s]).start(priority=j % 2)

    def wait_all(s):
        for j in range(TOP_K):
            pltpu.make_async_copy(ys_hbm.at[pl.ds(0, tm * rs)], buf.at[s, j], sem.at[s]).wait()

    @pl.when(i == 0)
    def _():
        def body(n, c):
            for j in range(TOP_K):
                src = ys_hbm.at[pl.ds(pl.multiple_of(d_ref[n * TOP_K + j], rs), rs)]
                dst = buf.at[0, j, pl.ds(pl.multiple_of(n * rs, rs), rs)]
                pltpu.make_async_copy(src, dst, sem.at[0]).start(priority=j % 2)
            return c

        lax.fori_loop(0, tm, body, 0)

    per = tm // (rs * TOP_K)
    batches = iter(range(0, tm, per))

    def start_batch():
        n0 = next(batches)
        for n in range(n0, n0 + per):
            start(dn_ref, n, nxt)

    d2 = h_ref.shape[1] // 2
    o_ref[...] = h_ref[...]
    wait_all(slot)
    wts = w_ref[...]
    wj = [wts[:, j:j + 1] for j in range(TOP_K)]
    for s in range(rs):
        r_lo = jnp.zeros((tm, 128), F32)
        r_hi = jnp.zeros((tm, 128), F32)
        for j in range(TOP_K):
            start_batch()
            ylo, yhi = _unpack_pair(buf[slot, j, pl.ds(s, tm, stride=rs), :])
            r_lo = r_lo + wj[j] * ylo
            r_hi = r_hi + wj[j] * yhi
        o_ref[:, s * 128:(s + 1) * 128] += r_lo
        o_ref[:, d2 + s * 128:d2 + (s + 1) * 128] += r_hi

    @pl.when(i == nt - 1)
    def _():
        wait_all(nxt)


def _combine(dest, ys, wcol, h, tm):
    m, d = h.shape
    tm = min(tm, m)
    nt = m // tm
    idx = lambda f: pl.BlockSpec((tm * TOP_K,), f, memory_space=pltpu.SMEM)
    return pl.pallas_call(
        functools.partial(_combine_kernel, tm=tm, nt=nt),
        grid=(nt,),
        in_specs=[idx(lambda i: (i,)), idx(lambda i: (jnp.minimum(i + 1, nt - 1),)),
                  pl.BlockSpec(memory_space=pl.ANY),
                  pl.BlockSpec((tm, N_EXPERTS), lambda i: (i, 0)),
                  pl.BlockSpec((tm, d), lambda i: (i, 0))],
        out_specs=pl.BlockSpec((tm, d), lambda i: (i, 0)),
        out_shape=jax.ShapeDtypeStruct((m, d), F32),
        scratch_shapes=[pltpu.VMEM((2, TOP_K, tm * ROW_SUB, 128), jnp.uint32),
                        pltpu.SemaphoreType.DMA((2,))],
        compiler_params=_cparams(1),
        name="moe_combine",
    )(dest, dest, ys, wcol, h)


def _moe(h1, hnp, lg_t, router_bias, w_gate, w_up, w_down, ws_gate, ws_up, ws_down):
    n = h1.shape[0]
    tile = MOE_TILE
    eidx, rank, wcol, cnt = _router(lg_t, router_bias.reshape(-1, 1), 512)
    counts = cnt[:, 0].astype(jnp.int32)
    tiles_per = (counts + tile - 1) // tile
    tile_end = jnp.cumsum(tiles_per)
    tile_start = tile_end - tiles_per
    n_tiles = (n * TOP_K) // tile + N_EXPERTS
    rows = n_tiles * tile
    tile_expert = jnp.minimum(jnp.sum(tile_end[None, :] <= jnp.arange(n_tiles)[:, None], axis=1),
                              N_EXPERTS - 1).astype(jnp.int32)
    n_used = tile_end[-1:].astype(jnp.int32)
    ids = jnp.arange(N_EXPERTS)
    later = jnp.logical_and(ids[None, :] > ids[:, None], tiles_per[None, :] > 0)
    nxt_of = jnp.min(jnp.where(later, ids[None, :], N_EXPERTS), axis=1)
    nxt_of = jnp.where(nxt_of == N_EXPERTS, -1, nxt_of)
    ord_of = jnp.cumsum((tiles_per > 0).astype(jnp.int32)) - 1
    onehot = tile_expert[:, None] == ids[None, :]
    next_expert = jnp.sum(jnp.where(onehot, nxt_of[None, :], 0), axis=1).astype(jnp.int32)
    group_idx = jnp.sum(jnp.where(onehot, ord_of[None, :], 0), axis=1).astype(jnp.int32)
    last_tile = jnp.clip(tile_end - 1, 0, n_tiles - 1).astype(jnp.int32)
    dest = _dest((tile_start * tile).astype(jnp.int32), eidx, rank, 2048)
    dest = dest.T.reshape(-1)
    xs, hs = _scatter_rows(last_tile, dest, hnp, h1, ws_gate.astype(BF16), ws_up.astype(BF16),
                           ws_down.astype(BF16), rows, tile, 512)
    ys = _experts(tile_expert, next_expert, group_idx, n_used, xs, w_gate, w_up, w_down, tile)
    return _combine(dest, ys, wcol, hs, 256)


def _ple_kernel(h_ref, p_ref, gin_ref, wg_ref, bg_ref, wp_ref, gout_ref, o_ref):
    h = h_ref[...]
    gate = jax.nn.sigmoid(_dot(_rms(h, gin_ref[...]).astype(BF16), wg_ref[...]) + bg_ref[...])
    pp = _dot(p_ref[...].astype(BF16), wp_ref[...])
    o_ref[...] = h + _rms(pp * gate, gout_ref[...])


def _ple(h, p, g_in, w_g, b_g, w_p, g_out, tm):
    m, d = h.shape
    tm = min(tm, m)
    full = lambda a: pl.BlockSpec(a.shape, lambda i: (0,) * a.ndim)
    return pl.pallas_call(
        _ple_kernel,
        grid=(m // tm,),
        in_specs=[pl.BlockSpec((tm, d), lambda i: (i, 0)),
                  pl.BlockSpec((tm, p.shape[1]), lambda i: (i, 0)),
                  full(g_in), full(w_g), full(b_g), full(w_p), full(g_out)],
        out_specs=pl.BlockSpec((tm, d), lambda i: (i, 0)),
        out_shape=jax.ShapeDtypeStruct((m, d), F32),
        compiler_params=_cparams(1),
        name="ple",
    )(h, p, g_in, w_g, b_g, w_p, g_out)


def _pad_rows(a, rows):
    return jnp.pad(a, ((0, rows - a.shape[0]), (0, 0)))


def _row(a):
    return a.reshape(1, -1)


def _proj_layout(a):
    def padded(x, width):
        return jnp.pad(x, [(0, 0)] * (x.ndim - 1) + [(0, width - x.shape[-1])])

    parts = [padded(a[..., :MLA_COLS], MLA_COLS_PAD), a[..., MLA_COLS:MLA_COLS + 3 * RW_C]]
    off = MLA_COLS + 3 * RW_C
    for width, wide in ((RW_DECAY_LORA, 128), (RW_A_LORA, 128), (RW_GATE_LORA, 256)):
        parts.append(padded(a[..., off:off + width], wide))
        off += width
    return jnp.concatenate(parts, axis=-1)


def _mla_from_proj(proj, positions, g_qa, w_uq, g_kva, w_ukv, g_qn, g_kn):
    b, t, _ = proj.shape
    wq = w_uq.reshape(MLA_Q_LORA, MLA_HEADS, MLA_QK)
    wq = jnp.concatenate([wq[:, :, :MLA_NOPE].reshape(MLA_Q_LORA, -1),
                          wq[:, :, MLA_NOPE:].reshape(MLA_Q_LORA, -1)], axis=1).astype(BF16)
    half = MLA_ROPE // 2
    inv = ROPE_THETA ** (-jnp.arange(half, dtype=F32) / half)
    invf = _row(jnp.concatenate([inv, inv]))
    sgn = _row(jnp.concatenate([-jnp.ones(half, F32), jnp.ones(half, F32)]))
    q, k, v = _mla_prep(proj, positions.reshape(b, t, 1), invf, sgn, _row(g_qa), wq, _row(g_kva),
                        w_ukv.astype(BF16), _row(g_qn), _row(g_kn), 512)
    return _attention(q, k, v, 1024, 512, 2)


def _rwkv_from_proj(proj, mu, w0, w_w2, a0, a_w2, g_w2, k_k, k_a, r_k, ln_w, ln_b):
    head_of = jnp.arange(RW_C) // RW_HEAD
    bd = (head_of[:, None] == head_of[None, :]).astype(BF16)
    mu_all = jnp.concatenate([jnp.zeros((MLA_COLS,), F32), mu])
    rr, ww, kx, vx, kkx, bx, gx = _rw_prep(
        proj, _row(_proj_layout(mu_all)), _row(w0), _pad_rows(w_w2, 128).astype(BF16), _row(a0),
        _pad_rows(a_w2, 128).astype(BF16), _pad_rows(g_w2, 256).astype(BF16), _row(k_k), _row(k_a),
        bd, 256)
    return _rw_chunk(rr, ww, kx, vx, kkx, bx, gx, _row(r_k), _row(ln_w), _row(ln_b))


def _layer(h, p, positions, g_mix, w_in, mla_g_qa, mla_w_uq, mla_g_kva, mla_w_ukv, mla_g_qn,
           mla_g_kn, rw_mu, rw_w0, rw_w_w2, rw_a0, rw_a_w2, rw_g_w2, rw_k_k, rw_k_a, rw_r_k,
           rw_ln_w, rw_ln_b, w_out, g_ffn, w_router, router_bias, w_gate, w_up, w_down, ws_gate,
           ws_up, ws_down, g_ple_in, w_ple_gate, b_ple_gate, w_ple_proj, g_ple_out):
    b, t, d = h.shape
    n = b * t
    x2 = h.reshape(n, d)

    proj = _in_proj(x2, _row(g_mix), _proj_layout(w_in).astype(BF16), 1024).reshape(b, t, -1)
    y_mla = _mla_from_proj(proj, positions, mla_g_qa, mla_w_uq, mla_g_kva, mla_w_ukv, mla_g_qn, mla_g_kn)
    y_rw = _rwkv_from_proj(proj, rw_mu, rw_w0, rw_w_w2, rw_a0, rw_a_w2, rw_g_w2, rw_k_k, rw_k_a, rw_r_k,
                           rw_ln_w, rw_ln_b)

    h1, hnp, lg_t = _out_proj(y_mla.reshape(n, -1), y_rw.reshape(n, -1), x2, w_out.astype(BF16),
                              _row(g_ffn), w_router, 512)
    h2 = _moe(h1, hnp, lg_t, router_bias, w_gate, w_up, w_down, ws_gate, ws_up, ws_down)

    h3 = _ple(h2, p.reshape(n, -1), _row(g_ple_in), w_ple_gate.astype(BF16), _row(b_ple_gate),
              w_ple_proj.astype(BF16), _row(g_ple_out), 512)
    return h3.reshape(b, t, d)


def kernel(x, p, positions, g_mix, w_in, mla_g_qa, mla_w_uq, mla_g_kva, mla_w_ukv, mla_g_qn, mla_g_kn, rw_mu, rw_w0, rw_w_w2, rw_a0, rw_a_w2, rw_g_w2, rw_k_k, rw_k_a, rw_r_k, rw_ln_w, rw_ln_b, w_out, g_ffn, w_router, router_bias, w_gate, w_up, w_down, ws_gate, ws_up, ws_down, g_ple_in, w_ple_gate, b_ple_gate, w_ple_proj, g_ple_out):
    params = (g_mix, w_in, mla_g_qa, mla_w_uq, mla_g_kva, mla_w_ukv, mla_g_qn, mla_g_kn, rw_mu,
              rw_w0, rw_w_w2, rw_a0, rw_a_w2, rw_g_w2, rw_k_k, rw_k_a, rw_r_k, rw_ln_w, rw_ln_b,
              w_out, g_ffn, w_router, router_bias, w_gate, w_up, w_down, ws_gate, ws_up, ws_down,
              g_ple_in, w_ple_gate, b_ple_gate, w_ple_proj, g_ple_out)
    h = x
    for i in range(g_mix.shape[0]):
        h = _layer(h, p[i], positions, *[a[i] for a in params])
    return h
```

```python
import functools

import jax
import jax.numpy as jnp
from jax import lax
from jax.experimental import pallas as pl
from jax.experimental.pallas import tpu as pltpu

F32 = jnp.float32
BF16 = jnp.bfloat16

D_MODEL = 2048
PLE_DIM = 256
RMS_EPS = 1e-6

MLA_HEADS = 8
MLA_NOPE = 128
MLA_ROPE = 64
MLA_QK = MLA_NOPE + MLA_ROPE
MLA_V = 128
MLA_Q_LORA = 512
MLA_KV_LORA = 256
ROPE_THETA = 10000.0
MLA_COLS = MLA_Q_LORA + MLA_KV_LORA + MLA_ROPE
MLA_COLS_PAD = 1024

RW_HEADS = 16
RW_HEAD = 64
RW_C = RW_HEADS * RW_HEAD
RW_DECAY_LORA = 64
RW_A_LORA = 64
RW_GATE_LORA = 160
RW_GN_EPS = 64e-5
RW_LORA_PAD = 512
PROJ_COLS = MLA_COLS_PAD + 3 * RW_C + RW_LORA_PAD
PROJ_TN = 1536
RW_CHUNK = 64

N_EXPERTS = 64
TOP_K = 8
N_GROUPS = 8
TOPK_GROUPS = 4
EXPERT_DIM = 512
ROUTED_SCALE = 2.5
MOE_TILE = 512
ROW_SUB = D_MODEL // 2 // 128

VMEM_LIMIT = 56 * 1024 * 1024


def _cparams(n_axes):
    return pltpu.CompilerParams(dimension_semantics=("arbitrary",) * n_axes,
                                vmem_limit_bytes=VMEM_LIMIT)


def _rms(x, g):
    ms = jnp.mean(x * x, axis=-1, keepdims=True)
    return x * lax.rsqrt(ms + RMS_EPS) * g


def _dot(a, b):
    return jnp.dot(a, b, preferred_element_type=F32)


def _dot_nt(a, b):
    return lax.dot_general(a, b, (((1,), (1,)), ((), ())), preferred_element_type=F32)


def _dot_tn(a, b):
    return lax.dot_general(a, b, (((0,), (0,)), ((), ())), preferred_element_type=F32)


def _split2(x):
    hi = x.astype(BF16)
    lo = (x - hi.astype(F32)).astype(BF16)
    return hi, lo


def _in_proj_kernel(x_ref, g_ref, w_ref, o_ref, xn_ref):
    @pl.when(pl.program_id(1) == 0)
    def _():
        xn_ref[...] = _rms(x_ref[...], g_ref[...]).astype(BF16)

    o_ref[...] = _dot(xn_ref[...], w_ref[...]).astype(o_ref.dtype)


def _in_proj(x, g, w, tm):
    m, k = x.shape
    n = w.shape[1]
    tn = PROJ_TN
    tm = min(tm, m)
    return pl.pallas_call(
        _in_proj_kernel,
        grid=(m // tm, n // tn),
        in_specs=[pl.BlockSpec((tm, k), lambda i, j: (i, 0)),
                  pl.BlockSpec((1, k), lambda i, j: (0, 0)),
                  pl.BlockSpec((k, tn), lambda i, j: (0, j))],
        out_specs=pl.BlockSpec((tm, tn), lambda i, j: (i, j)),
        out_shape=jax.ShapeDtypeStruct((m, n), BF16),
        scratch_shapes=[pltpu.VMEM((tm, k), BF16)],
        compiler_params=_cparams(2),
        name="in_proj",
    )(x, g, w)


def _mla_prep_kernel(pm_ref, pos_ref, invf_ref, sgn_ref, gqa_ref, wuq_ref, gkva_ref, wukv_ref,
                     gqn_ref, gkn_ref, q_ref, k_ref, v_ref):
    pm = pm_ref[0].astype(F32)
    ang = pos_ref[0].astype(F32) * invf_ref[...]
    cos = jnp.cos(ang)
    sin = jnp.sin(ang) * sgn_ref[...]

    def rope(xr):
        half = MLA_ROPE // 2
        swapped = jnp.concatenate([xr[:, half:], xr[:, :half]], axis=1)
        return xr * cos + swapped * sin

    scale = MLA_QK ** -0.5 * 1.4426950408889634
    gqn = gqn_ref[...]
    gkn = gkn_ref[...]

    cq = _rms(pm[:, :MLA_Q_LORA], gqa_ref[...]).astype(BF16)
    q = _dot(cq, wuq_ref[...])
    ckv = _rms(pm[:, MLA_Q_LORA:MLA_Q_LORA + MLA_KV_LORA], gkva_ref[...]).astype(BF16)
    kv = _dot(ckv, wukv_ref[...])
    kpe = pm[:, MLA_Q_LORA + MLA_KV_LORA:MLA_COLS]
    kpe_ssq = jnp.sum(kpe * kpe, axis=-1, keepdims=True)
    kpe_rot = rope(kpe * gkn[:, MLA_NOPE:])

    for h in range(MLA_HEADS):
        qn = q[:, h * MLA_NOPE:(h + 1) * MLA_NOPE]
        qr = q[:, MLA_HEADS * MLA_NOPE + h * MLA_ROPE:MLA_HEADS * MLA_NOPE + (h + 1) * MLA_ROPE]
        ssq = jnp.sum(qn * qn, axis=-1, keepdims=True) + jnp.sum(qr * qr, axis=-1, keepdims=True)
        inv = lax.rsqrt(ssq * (1.0 / MLA_QK) + RMS_EPS) * scale
        q_ref[0, h, :, 0:MLA_NOPE] = (qn * inv * gqn[:, :MLA_NOPE]).astype(BF16)
        q_ref[0, h, :, MLA_NOPE:MLA_QK] = (rope(qr * gqn[:, MLA_NOPE:]) * inv).astype(BF16)

        kn = kv[:, h * 256:h * 256 + MLA_NOPE]
        ssq = jnp.sum(kn * kn, axis=-1, keepdims=True) + kpe_ssq
        inv = lax.rsqrt(ssq * (1.0 / MLA_QK) + RMS_EPS)
        k_ref[0, h, :, 0:MLA_NOPE] = (kn * inv * gkn[:, :MLA_NOPE]).astype(BF16)
        k_ref[0, h, :, MLA_NOPE:MLA_QK] = (kpe_rot * inv).astype(BF16)
        v_ref[0, h, :, 0:MLA_V] = kv[:, h * 256 + MLA_NOPE:(h + 1) * 256].astype(BF16)
        v_ref[0, h, :, MLA_V:] = jnp.ones((kv.shape[0], MLA_V), BF16)


def _mla_prep(pm, pos, invf, sgn, g_qa, w_uq, g_kva, w_ukv, g_qn, g_kn, tm):
    b, t, _ = pm.shape
    tm = min(tm, t)
    full = lambda a: pl.BlockSpec(a.shape, lambda i, j: (0,) * a.ndim)
    qk_shape = jax.ShapeDtypeStruct((b, MLA_HEADS, t, MLA_QK), BF16)
    return pl.pallas_call(
        _mla_prep_kernel,
        grid=(b, t // tm),
        in_specs=[pl.BlockSpec((1, tm, MLA_COLS_PAD), lambda i, j: (i, j, 0)),
                  pl.BlockSpec((1, tm, 1), lambda i, j: (i, j, 0)),
                  full(invf), full(sgn), full(g_qa), full(w_uq), full(g_kva), full(w_ukv),
                  full(g_qn), full(g_kn)],
        out_specs=[pl.BlockSpec((1, MLA_HEADS, tm, MLA_QK), lambda i, j: (i, 0, j, 0)),
                   pl.BlockSpec((1, MLA_HEADS, tm, MLA_QK), lambda i, j: (i, 0, j, 0)),
                   pl.BlockSpec((1, MLA_HEADS, tm, 2 * MLA_V), lambda i, j: (i, 0, j, 0))],
        out_shape=[qk_shape, qk_shape, jax.ShapeDtypeStruct((b, MLA_HEADS, t, 2 * MLA_V), BF16)],
        compiler_params=_cparams(2),
        name="mla_prep",
    )(pm, pos, invf, sgn, g_qa, w_uq, g_kva, w_ukv, g_qn, g_kn)


def _attn_kernel(q_ref, k_ref, v_ref, o_ref, m_ref, acc_ref, *, tq, tk, nh):
    qi = pl.program_id(2)
    m_ref[...] = jnp.full(m_ref.shape, -jnp.inf, F32)
    acc_ref[...] = jnp.zeros(acc_ref.shape, F32)
    heads = range(nh)
    def scores(j, r0, nr):
        start = pl.multiple_of(j * tk, tk)
        return [_dot_nt(q_ref[0, h, r0:r0 + nr, :], k_ref[0, h, pl.ds(start, tk), :]) for h in heads]

    def update(j, r0, nr, ss, triangle):
        start = pl.multiple_of(j * tk, tk)
        rows = slice(r0, r0 + nr)
        for h in heads:
            s = ss[h]
            if triangle:
                keep = (lax.broadcasted_iota(jnp.int32, (nr, tk), 1)
                        <= lax.broadcasted_iota(jnp.int32, (nr, tk), 0))
                s = jnp.where(keep, s, -jnp.inf)
            m_old = m_ref[h, rows, :]
            m_new = jnp.maximum(m_old, jnp.max(s, axis=-1, keepdims=True))
            alpha = jnp.exp2(m_old - m_new)
            p = jnp.exp2(s - jnp.concatenate([m_new] * (tk // 128), axis=1))
            pv = _dot(p.astype(BF16), v_ref[0, h, pl.ds(start, tk), :])
            acc_ref[h, rows, :] = jnp.concatenate([alpha, alpha], axis=1) * acc_ref[h, rows, :] + pv
            m_ref[h, rows, :] = m_new

    per = tq // tk
    n_full = qi * per

    def body(jj, c):
        ss = [scores(jj * per + d, 0, tq) for d in range(per)]
        for d in range(per):
            update(jj * per + d, 0, tq, ss[d], False)
        return c

    lax.fori_loop(0, qi, body, 0)
    blocks = []
    for d in range(per):
        blocks.append((n_full + d, d * tk, tk, True))
        if d + 1 < per:
            blocks.append((n_full + d, (d + 1) * tk, tq - (d + 1) * tk, False))
    ss = [scores(j, r0, nr) for j, r0, nr, _ in blocks]
    for (j, r0, nr, tri), s in zip(blocks, ss):
        update(j, r0, nr, s, tri)
    for h in heads:
        acc = acc_ref[h]
        o_ref[0, :, h * MLA_V:(h + 1) * MLA_V] = (acc[:, :MLA_V] / acc[:, MLA_V:]).astype(o_ref.dtype)


def _attention(q, k, v, tq, tk, nh):
    b, h, t, _ = q.shape
    tq = min(tq, t)
    tk = min(tk, tq)
    return pl.pallas_call(
        functools.partial(_attn_kernel, tq=tq, tk=tk, nh=nh),
        grid=(b, h // nh, t // tq),
        in_specs=[pl.BlockSpec((1, nh, tq, MLA_QK), lambda i, j, n: (i, j, n, 0)),
                  pl.BlockSpec((1, nh, t, MLA_QK), lambda i, j, n: (i, j, 0, 0)),
                  pl.BlockSpec((1, nh, t, 2 * MLA_V), lambda i, j, n: (i, j, 0, 0))],
        out_specs=pl.BlockSpec((1, tq, nh * MLA_V), lambda i, j, n: (i, n, j)),
        out_shape=jax.ShapeDtypeStruct((b, t, h * MLA_V), BF16),
        scratch_shapes=[pltpu.VMEM((nh, tq, 128), F32), pltpu.VMEM((nh, tq, 2 * MLA_V), F32)],
        compiler_params=_cparams(3),
        name="mla_attention",
    )(q, k, v)


def _rw_prep_kernel(r_ref, k_ref, v_ref, l_ref, rp_ref, kp_ref, vp_ref, lp_ref, mu_ref, w0_ref,
                    ww2_ref, a0_ref, aw2_ref, gw2_ref, kk_ref, ka_ref, bd_ref,
                    r_o, w_o, k_o, v_o, kk_o, b_o, g_o):
    first = pl.program_id(1) == 0
    c = RW_C

    def shifted(ref, pref, mu):
        z = ref[0].astype(F32)
        prev = pref[0][15:16, :].astype(F32)
        prev = jnp.where(first, jnp.zeros_like(prev), prev)
        row = lax.broadcasted_iota(jnp.int32, z.shape, 0)
        zs = jnp.where(row == 0, prev, pltpu.roll(z, 1, axis=0))
        return z + (zs - z) * mu

    o = MLA_COLS_PAD
    r = shifted(r_ref, rp_ref, mu_ref[:, o:o + c])
    k = shifted(k_ref, kp_ref, mu_ref[:, o + c:o + 2 * c])
    v = shifted(v_ref, vp_ref, mu_ref[:, o + 2 * c:o + 3 * c])
    lz = shifted(l_ref, lp_ref, mu_ref[:, o + 3 * c:])
    wl = lz[:, 0:128]
    al = lz[:, 128:256]
    gl = lz[:, 256:512]

    f = w0_ref[...] + _dot(jnp.tanh(wl).astype(BF16), ww2_ref[...])
    softplus_neg_f = jnp.maximum(-f, 0.0) + jnp.log(1.0 + jnp.exp(-jnp.abs(f)))
    w = -softplus_neg_f - 0.5
    w_o[0] = -jnp.exp(w)
    a = jax.nn.sigmoid(a0_ref[...] + _dot(al.astype(BF16), aw2_ref[...]))
    g_o[0] = _dot(jax.nn.sigmoid(gl).astype(BF16), gw2_ref[...]).astype(g_o.dtype)

    kk = k * kk_ref[...]
    hi, lo = _split2(kk * kk)
    ssq = _dot(hi, bd_ref[...]) + _dot(lo, bd_ref[...])
    kk = kk * lax.rsqrt(jnp.maximum(ssq, 1e-24))
    r_o[0] = r.astype(r_o.dtype)
    k_o[0] = (k * (1.0 + (a - 1.0) * ka_ref[...])).astype(k_o.dtype)
    v_o[0] = v.astype(v_o.dtype)
    kk_o[0] = kk.astype(kk_o.dtype)
    b_o[0] = (kk * a).astype(b_o.dtype)


def _rw_prep(proj, mu, w0, ww2, a0, aw2, gw2, k_k, k_a, bd, tm):
    b, t, _ = proj.shape
    tm = min(tm, t)
    c = RW_C
    full = lambda a: pl.BlockSpec(a.shape, lambda i, j: (0,) * a.ndim)
    cur = lambda w, cb: pl.BlockSpec((1, tm, w), lambda i, j: (i, j, cb))
    prev = lambda w, cb: pl.BlockSpec((1, 16, w), lambda i, j: (i, jnp.maximum(j * (tm // 16) - 1, 0), cb))
    lora_cb = (MLA_COLS_PAD + 3 * c) // RW_LORA_PAD
    groups = [(c, 1), (c, 2), (c, 3), (RW_LORA_PAD, lora_cb)]
    out = [jax.ShapeDtypeStruct((b, t, c), F32 if i == 1 else BF16) for i in range(7)]
    ospec = pl.BlockSpec((1, tm, c), lambda i, j: (i, j, 0))
    return pl.pallas_call(
        _rw_prep_kernel,
        grid=(b, t // tm),
        in_specs=[cur(*g) for g in groups] + [prev(*g) for g in groups]
        + [full(mu), full(w0), full(ww2), full(a0), full(aw2), full(gw2), full(k_k), full(k_a), full(bd)],
        out_specs=[ospec] * 7,
        out_shape=out,
        compiler_params=_cparams(2),
        name="rwkv_prep",
    )(*([proj] * 8), mu, w0, ww2, a0, aw2, gw2, k_k, k_a, bd)


def _rw_chunk_kernel(r_ref, w_ref, k_ref, v_ref, kk_ref, b_ref, g_ref, rk_ref, lnw_ref, lnb_ref,
                     o_ref, s_ref):
    cs = RW_CHUNK
    hd = RW_HEAD

    @pl.when(pl.program_id(1) == 0)
    def _():
        s_ref[...] = jnp.zeros(s_ref.shape, F32)

    r = r_ref[0].astype(F32)
    w = w_ref[0]
    k = k_ref[0].astype(F32)
    v = v_ref[0].astype(F32)
    kk = kk_ref[0].astype(F32)
    b = b_ref[0].astype(F32)
    pairs = range(RW_HEADS // 2)
    pw = 2 * hd

    row = lax.broadcasted_iota(jnp.int32, (cs, pw), 0)
    col = lax.broadcasted_iota(jnp.int32, (cs, pw), 1)
    colh = jnp.bitwise_and(col, hd - 1)
    lo = col < hd
    strict = row > colh
    eye = jnp.where(row == colh, 1.0, 0.0).astype(F32)
    row2 = lax.broadcasted_iota(jnp.int32, (cs, 2 * pw), 0)
    col2 = lax.broadcasted_iota(jnp.int32, (cs, 2 * pw), 1)
    incl2 = row2 >= jnp.bitwise_and(col2, hd - 1)
    row3 = lax.broadcasted_iota(jnp.int32, (cs, 3 * cs), 0)
    col3 = lax.broadcasted_iota(jnp.int32, (cs, 3 * cs), 1)
    tri3 = jnp.where(row3 >= jnp.bitwise_and(col3, cs - 1), 1.0, 0.0).astype(BF16)

    def bdiag(x):
        zero = jnp.zeros_like(x)
        return jnp.concatenate([jnp.where(lo, x, zero), jnp.where(lo, zero, x)], axis=0)

    def cat3_lhs(x):
        hi = x.astype(BF16)
        return jnp.concatenate([hi, (x - hi.astype(F32)).astype(BF16), hi], axis=1)

    def cat3_rhs(x):
        hi, lw = _split2(x)
        bh = bdiag(hi)
        return jnp.concatenate([bh, bh, bdiag(lw)], axis=0)

    w1 = w.astype(BF16)
    w2 = (w - w1.astype(F32)).astype(BF16)
    w3 = (w - w1.astype(F32) - w2.astype(F32)).astype(BF16)
    logp = _dot(tri3, jnp.concatenate([w1, w2, w3], axis=0))
    logp_end = logp[cs - 1:cs, :]
    p_tail = jnp.exp(logp_end - logp)
    p_end = jnp.exp(logp_end)
    p_inv = jnp.exp(-logp)

    a_t = -kk * jnp.exp(logp - w)
    r_t = r * jnp.exp(logp)
    b_t = b * p_inv
    k_t = k * p_inv
    b_h = b * p_tail
    k_h = k * p_tail

    sls = [slice(p * pw, (p + 1) * pw) for p in pairs]
    lhs = [jnp.concatenate([a_t[:, sl], r_t[:, sl]], axis=0).astype(BF16) for sl in sls]
    rhs = [jnp.concatenate([bdiag(b_t[:, sl].astype(BF16)), bdiag(k_t[:, sl].astype(BF16))], axis=0)
           for sl in sls]
    gm = [_dot_nt(lhs[p], rhs[p]) for p in pairs]
    s0 = [s_ref[p] for p in pairs]
    ls = [_dot_nt(lhs[p], bdiag(s0[p].astype(BF16))) for p in pairs]
    vb = [v[:, sl].astype(BF16) for sl in sls]
    bv = [bdiag(vb[p]) for p in pairs]
    a_ak = [jnp.where(strict, gm[p][:cs, pw:], 0.0).astype(BF16) for p in pairs]
    rhs_u = [ls[p][:cs] + _dot(a_ak[p], bv[p]) for p in pairs]

    a_ab = [jnp.where(strict, gm[p][:cs, :pw], 0.0) for p in pairs]
    rb = lambda m: jnp.right_shift(row, m)
    cb = lambda m: jnp.right_shift(colh, m)
    x = [jnp.where(rb(3) == cb(3), a_ab[p], 0.0) for p in pairs]
    tm = [eye + x[p] for p in pairs]
    x = [_dot(cat3_lhs(x[p]), cat3_rhs(x[p])) for p in pairs]
    both = [_dot(cat3_lhs(jnp.concatenate([x[p], tm[p]], axis=0)), cat3_rhs(x[p])) for p in pairs]
    x = [both[p][:cs] for p in pairs]
    tm = [tm[p] + both[p][cs:] for p in pairs]
    tm = [tm[p] + _dot(cat3_lhs(tm[p]), cat3_rhs(x[p])) for p in pairs]
    for m in (3, 4, 5):
        off = jnp.logical_and(rb(m + 1) == cb(m + 1),
                              jnp.logical_and(jnp.bitwise_and(rb(m), 1) == 1, jnp.bitwise_and(cb(m), 1) == 0))
        bt = [_dot(cat3_lhs(jnp.where(off, a_ab[p], 0.0)), cat3_rhs(tm[p])) for p in pairs]
        tm = [tm[p] + _dot(cat3_lhs(tm[p]), cat3_rhs(bt[p])) for p in pairs]

    u = [_dot(cat3_lhs(tm[p]), cat3_rhs(rhs_u[p])) for p in pairs]
    ub = [u[p].astype(BF16) for p in pairs]
    a_r = [jnp.where(incl2, gm[p][cs:, :], 0.0).astype(BF16) for p in pairs]
    y = [ls[p][cs:] + _dot(a_r[p], jnp.concatenate([bdiag(ub[p]), bv[p]], axis=0)) for p in pairs]
    for p in pairs:
        uv = jnp.concatenate([ub[p], vb[p]], axis=0)
        bk = jnp.concatenate([b_h[:, sls[p]], k_h[:, sls[p]]], axis=0).astype(BF16)
        full = _dot_tn(uv, bk)
        s_ref[p] = s0[p] * p_end[:, sls[p]] + jnp.where(lo, full[:hd], full[hd:])

    def head_sum(t):
        s_lo = jnp.sum(jnp.where(lo, t, 0.0), axis=-1, keepdims=True)
        s_all = jnp.sum(t, axis=-1, keepdims=True)
        return jnp.where(lo, s_lo, s_all - s_lo)

    rk_all = r * k * rk_ref[...]
    g = g_ref[0].astype(F32)
    for p in pairs:
        sl = sls[p]
        yc = y[p] - head_sum(y[p]) * (1.0 / hd)
        var = head_sum(yc * yc) * (1.0 / hd)
        yn = yc * lax.rsqrt(var + RW_GN_EPS) * lnw_ref[:, sl] + lnb_ref[:, sl]
        bonus = head_sum(rk_all[:, sl]) * v[:, sl]
        o_ref[0, :, sl] = ((yn + bonus) * g[:, sl]).astype(o_ref.dtype)


def _rw_chunk(r, w, k, v, kk, bb, g, r_k, ln_w, ln_b):
    b, t, c = r.shape
    cs = RW_CHUNK
    full = lambda a: pl.BlockSpec(a.shape, lambda i, j: (0,) * a.ndim)
    spec = pl.BlockSpec((1, cs, c), lambda i, j: (i, j, 0))
    return pl.pallas_call(
        _rw_chunk_kernel,
        grid=(b, t // cs),
        in_specs=[spec] * 7 + [full(r_k), full(ln_w), full(ln_b)],
        out_specs=spec,
        out_shape=jax.ShapeDtypeStruct((b, t, c), BF16),
        scratch_shapes=[pltpu.VMEM((RW_HEADS // 2, RW_HEAD, 2 * RW_HEAD), F32)],
        compiler_params=_cparams(2),
        name="rwkv_chunk",
    )(r, w, k, v, kk, bb, g, r_k, ln_w, ln_b)


def _pack_pair(a, b):
    ua = lax.bitcast_convert_type(a.astype(BF16).astype(F32), jnp.uint32)
    ub = lax.bitcast_convert_type(b.astype(BF16).astype(F32), jnp.uint32)
    return lax.shift_right_logical(ua, jnp.uint32(16)) | (ub & jnp.uint32(0xFFFF0000))


def _unpack_pair(u):
    lo = lax.bitcast_convert_type(lax.shift_left(u, jnp.uint32(16)), F32)
    hi = lax.bitcast_convert_type(u & jnp.uint32(0xFFFF0000), F32)
    return lo, hi


def _store_rows(ref, packed):
    m = packed.shape[0]
    for s in range(ROW_SUB):
        ref[pl.ds(s, m, stride=ROW_SUB), :] = packed[:, s * 128:(s + 1) * 128]


def _load_rows(ref):
    m = ref.shape[-2] // ROW_SUB
    los, his = [], []
    for s in range(ROW_SUB):
        lo, hi = _unpack_pair(ref[pl.ds(s, m, stride=ROW_SUB), :])
        los.append(lo.astype(BF16))
        his.append(hi.astype(BF16))
    return jnp.concatenate(los, axis=1), jnp.concatenate(his, axis=1)


def _out_proj_kernel(ya_ref, yr_ref, x_ref, w_ref, g_ref, wrh_ref, wrl_ref, h_ref, hn_ref, lg_ref):
    half = ya_ref.shape[1]
    acc = _dot(ya_ref[...], w_ref[0:half, :]) + _dot(yr_ref[...], w_ref[half:, :])
    h = x_ref[...] + acc
    h_ref[...] = h
    hn = _rms(h, g_ref[...])
    d2 = hn.shape[1] // 2
    _store_rows(hn_ref, _pack_pair(hn[:, :d2], hn[:, d2:]))
    hh, hl = _split2(hn)
    lg_ref[...] = (_dot_nt(wrh_ref[...], hh) + _dot_nt(wrh_ref[...], hl)
                   + _dot_nt(wrl_ref[...], hh))


def _out_proj(ya, yr, x, w, g, wr, tm):
    wr_hi, wr_lo = _split2(wr.T)
    m, d = x.shape
    tm = min(tm, m)
    full = lambda a: pl.BlockSpec(a.shape, lambda i: (0,) * a.ndim)
    return pl.pallas_call(
        _out_proj_kernel,
        grid=(m // tm,),
        in_specs=[pl.BlockSpec((tm, ya.shape[1]), lambda i: (i, 0)),
                  pl.BlockSpec((tm, yr.shape[1]), lambda i: (i, 0)),
                  pl.BlockSpec((tm, d), lambda i: (i, 0)),
                  full(w), full(g), full(wr_hi), full(wr_lo)],
        out_specs=[pl.BlockSpec((tm, d), lambda i: (i, 0)),
                   pl.BlockSpec((tm * ROW_SUB, 128), lambda i: (i, 0)),
                   pl.BlockSpec((N_EXPERTS, tm), lambda i: (0, i))],
        out_shape=[jax.ShapeDtypeStruct((m, d), F32), jax.ShapeDtypeStruct((m * ROW_SUB, 128), jnp.uint32),
                   jax.ShapeDtypeStruct((N_EXPERTS, m), F32)],
        compiler_params=_cparams(1),
        name="out_proj_router_logits",
    )(ya, yr, x, w, g, wr_hi, wr_lo)


def _router_kernel(lg_ref, bias_ref, e_ref, r_ref, w_ref, cnt_ref, carry_ref):
    tm = lg_ref.shape[1]
    gsz = N_EXPERTS // N_GROUPS

    @pl.when(pl.program_id(0) == 0)
    def _():
        carry_ref[...] = jnp.zeros(carry_ref.shape, F32)

    scores = jax.nn.sigmoid(lg_ref[...]).reshape(N_GROUPS, gsz, tm)
    biased = scores + bias_ref[...].reshape(N_GROUPS, gsz, 1)
    neg = -jnp.inf

    eidx = lax.broadcasted_iota(jnp.int32, biased.shape, 1)
    m1 = jnp.max(biased, axis=1, keepdims=True)
    first = jnp.min(jnp.where(biased == m1, eidx, gsz), axis=1, keepdims=True)
    m2 = jnp.max(jnp.where(eidx == first, neg, biased), axis=1, keepdims=True)
    gscore = m1 + m2

    gidx = lax.broadcasted_iota(jnp.int32, gscore.shape, 0)
    gsel = jnp.zeros(gscore.shape, jnp.bool_)
    work = gscore
    for _ in range(TOPK_GROUPS):
        best = jnp.max(work, axis=0, keepdims=True)
        pick = jnp.min(jnp.where(work == best, gidx, N_GROUPS), axis=0, keepdims=True)
        hit = gidx == pick
        gsel = jnp.logical_or(gsel, hit)
        work = jnp.where(hit, neg, work)

    flat = lax.broadcasted_iota(jnp.int32, biased.shape, 0) * gsz + eidx
    work = jnp.where(gsel, biased, neg)
    hits = []
    picks = []
    for _ in range(TOP_K):
        best = jnp.max(jnp.max(work, axis=1, keepdims=True), axis=0, keepdims=True)
        cand = jnp.where(work == best, flat, N_EXPERTS)
        pick = jnp.min(jnp.min(cand, axis=1, keepdims=True), axis=0, keepdims=True)
        hit = flat == pick
        hits.append(hit)
        picks.append(pick)
        work = jnp.where(hit, neg, work)

    def pick_value(hit, val):
        s = jnp.sum(jnp.sum(jnp.where(hit, val, 0.0), axis=1, keepdims=True), axis=0, keepdims=True)
        return s.reshape(1, tm)

    sel = jnp.zeros(biased.shape, F32)
    for hit in hits:
        sel = jnp.where(hit, 1.0, sel)
    sel2 = sel.reshape(N_EXPERTS, tm)
    r_i = lax.broadcasted_iota(jnp.int32, (tm, tm), 0)
    c_i = lax.broadcasted_iota(jnp.int32, (tm, tm), 1)
    upper = jnp.where(r_i < c_i, 1.0, 0.0).astype(BF16)
    rank = _dot(sel2.astype(BF16), upper) + carry_ref[...]
    carry = carry_ref[...] + jnp.sum(sel2, axis=1, keepdims=True)
    carry_ref[...] = carry
    cnt_ref[...] = carry
    rank3 = rank.reshape(N_GROUPS, gsz, tm)

    raw = [pick_value(hit, scores) for hit in hits]
    tot = raw[0]
    for x in raw[1:]:
        tot = tot + x
    wrow = lax.broadcasted_iota(jnp.int32, (N_EXPERTS, tm), 0)
    wt = jnp.zeros((N_EXPERTS, tm), F32)
    for j in range(TOP_K):
        e_ref[j:j + 1, :] = picks[j].reshape(1, tm)
        r_ref[j:j + 1, :] = pick_value(hits[j], rank3).astype(jnp.int32)
        wt = jnp.where(wrow == j, raw[j] / tot * ROUTED_SCALE, wt)
    w_ref[...] = wt.T


def _router(lg_t, bias, tm):
    e, m = lg_t.shape
    tm = min(tm, m)
    row = pl.BlockSpec((TOP_K, tm), lambda i: (0, i))
    return pl.pallas_call(
        _router_kernel,
        grid=(m // tm,),
        in_specs=[pl.BlockSpec((e, tm), lambda i: (0, i)),
                  pl.BlockSpec((e, 1), lambda i: (0, 0))],
        out_specs=[row, row, pl.BlockSpec((tm, e), lambda i: (i, 0)),
                   pl.BlockSpec((e, 1), lambda i: (0, 0))],
        out_shape=[jax.ShapeDtypeStruct((TOP_K, m), jnp.int32),
                   jax.ShapeDtypeStruct((TOP_K, m), jnp.int32),
                   jax.ShapeDtypeStruct((m, e), F32),
                   jax.ShapeDtypeStruct((e, 1), F32)],
        scratch_shapes=[pltpu.VMEM((e, 1), F32)],
        compiler_params=_cparams(1),
        name="router_topk",
    )(lg_t, bias)


def _dest_kernel(off_ref, e_ref, r_ref, d_ref):
    e = e_ref[...]
    acc = r_ref[...]
    for x in range(N_EXPERTS):
        acc = acc + jnp.where(e == x, off_ref[x], 0)
    d_ref[...] = acc * ROW_SUB


def _dest(off, eidx, rank, tm):
    k, m = eidx.shape
    tm = min(tm, m)
    spec = pl.BlockSpec((k, tm), lambda i: (0, i))
    return pl.pallas_call(
        _dest_kernel,
        grid=(m // tm,),
        in_specs=[pl.BlockSpec(memory_space=pltpu.SMEM), spec, spec],
        out_specs=spec,
        out_shape=jax.ShapeDtypeStruct((k, m), jnp.int32),
        compiler_params=_cparams(1),
        name="moe_dest_rows",
    )(off, eidx, rank)


def _scatter_kernel(zt_ref, d_ref, x_ref, z_ref, h_ref, sg_ref, su_ref, sd_ref, xs_hbm, hs_ref, sem,
                    *, tm, tile):
    i = pl.program_id(0)
    rs = ROW_SUB

    @pl.when(i == 0)
    def _():
        def zero(e, c):
            start = pl.multiple_of(zt_ref[e] * (tile * rs), tile * rs)
            pltpu.make_async_copy(z_ref, xs_hbm.at[pl.ds(start, tile * rs)], sem).start()
            return c

        lax.fori_loop(0, N_EXPERTS, zero, 0)

        def zwait(e, c):
            pltpu.make_async_copy(z_ref, xs_hbm.at[pl.ds(0, tile * rs)], sem).wait()
            return c

        lax.fori_loop(0, N_EXPERTS, zwait, 0)

    def body(n, c):
        src = x_ref.at[pl.ds(pl.multiple_of(n * rs, rs), rs)]
        for j in range(TOP_K):
            dst = xs_hbm.at[pl.ds(pl.multiple_of(d_ref[n * TOP_K + j], rs), rs)]
            pltpu.make_async_copy(src, dst, sem).start(priority=j % 2)
        return c

    q4 = tm // 4
    lax.fori_loop(0, q4, body, 0)
    lo, hi = _load_rows(x_ref)
    d2 = lo.shape[1]
    gate = _dot(lo, sg_ref[0:d2, :]) + _dot(hi, sg_ref[d2:, :])
    lax.fori_loop(q4, 2 * q4, body, 0)
    up = _dot(lo, su_ref[0:d2, :]) + _dot(hi, su_ref[d2:, :])
    mid = (gate * jax.nn.sigmoid(gate) * up).astype(BF16)
    lax.fori_loop(2 * q4, 3 * q4, body, 0)
    hs_ref[...] = h_ref[...] + _dot(mid, sd_ref[...])
    lax.fori_loop(3 * q4, tm, body, 0)

    for j in range(TOP_K):
        pltpu.make_async_copy(x_ref, xs_hbm.at[pl.ds(0, tm * rs)], sem).wait()


def _scatter_rows(zero_tile, dest, xp, h, sg, su, sd, rows, tile, tm):
    m, d = h.shape
    tm = min(tm, m)
    full = lambda a: pl.BlockSpec(a.shape, lambda i, zt: (0,) * a.ndim)
    return pl.pallas_call(
        functools.partial(_scatter_kernel, tm=tm, tile=tile),
        grid_spec=pltpu.PrefetchScalarGridSpec(
            num_scalar_prefetch=1,
            grid=(m // tm,),
            in_specs=[pl.BlockSpec((tm * TOP_K,), lambda i, zt: (i,), memory_space=pltpu.SMEM),
                      pl.BlockSpec((tm * ROW_SUB, 128), lambda i, zt: (i, 0)),
                      pl.BlockSpec((tile * ROW_SUB, 128), lambda i, zt: (0, 0)),
                      pl.BlockSpec((tm, d), lambda i, zt: (i, 0)),
                      full(sg), full(su), full(sd)],
            out_specs=[pl.BlockSpec(memory_space=pl.ANY), pl.BlockSpec((tm, d), lambda i, zt: (i, 0))],
            scratch_shapes=[pltpu.SemaphoreType.DMA(())],
        ),
        out_shape=[jax.ShapeDtypeStruct((rows * ROW_SUB, 128), jnp.uint32),
                   jax.ShapeDtypeStruct((m, d), F32)],
        compiler_params=_cparams(1),
        name="moe_scatter_rows_shared",
    )(zero_tile, dest, xp, jnp.zeros((tile * ROW_SUB, 128), jnp.uint32), h, sg, su, sd)


def _expert_kernel(te_ref, nx_ref, gi_ref, nu_ref, x_ref, wg_hbm, wu_hbm, wd_hbm, o_ref,
                   wgf, wuf, wdf, wgb, wub, wdb, sem):
    t = pl.program_id(0)
    tc = jnp.minimum(t, nu_ref[0] - 1)
    prev = jnp.maximum(tc - 1, 0)
    new_expert = jnp.logical_or(t == 0, te_ref[tc] != te_ref[prev])
    slot = jnp.bitwise_and(gi_ref[tc], 1)

    def copies(e, s):
        return (pltpu.make_async_copy(wg_hbm.at[e], wgf.at[s], sem.at[s]),
                pltpu.make_async_copy(wu_hbm.at[e], wuf.at[s], sem.at[s]),
                pltpu.make_async_copy(wd_hbm.at[e], wdf.at[s], sem.at[s]))

    @pl.when(t == 0)
    def _():
        for c in copies(te_ref[0], 0):
            c.start()

    @pl.when(jnp.logical_and(new_expert, t < nu_ref[0]))
    def _():
        for c in copies(te_ref[tc], slot):
            c.wait()
        wgb[...] = wgf[slot].astype(BF16)
        wub[...] = wuf[slot].astype(BF16)
        wdb[...] = wdf[slot].astype(BF16)

        @pl.when(nx_ref[tc] >= 0)
        def _():
            for c in copies(nx_ref[tc], 1 - slot):
                c.start()

    @pl.when(t < nu_ref[0])
    def _():
        lo, hi = _load_rows(x_ref)
        d2 = lo.shape[1]
        gate = _dot(lo, wgb[0:d2, :]) + _dot(hi, wgb[d2:, :])
        up = _dot(lo, wub[0:d2, :]) + _dot(hi, wub[d2:, :])
        mid = (gate * jax.nn.sigmoid(gate) * up).astype(BF16)
        y = _dot(mid, wdb[...])
        _store_rows(o_ref, _pack_pair(y[:, :d2], y[:, d2:]))


def _experts(tile_expert, next_expert, group_idx, n_used, xs, wg, wu, wd, tile):
    rows = xs.shape[0] // ROW_SUB
    d = wg.shape[1]
    blk = (tile * ROW_SUB, 128)
    tile_map = lambda t, te, nx, gi, nu: (jnp.minimum(t, nu[0] - 1), 0)
    hbm = pl.BlockSpec(memory_space=pl.ANY)
    return pl.pallas_call(
        _expert_kernel,
        grid_spec=pltpu.PrefetchScalarGridSpec(
            num_scalar_prefetch=4,
            grid=(rows // tile,),
            in_specs=[pl.BlockSpec(blk, tile_map), hbm, hbm, hbm],
            out_specs=pl.BlockSpec(blk, tile_map),
            scratch_shapes=[pltpu.VMEM((2, d, EXPERT_DIM), F32), pltpu.VMEM((2, d, EXPERT_DIM), F32),
                            pltpu.VMEM((2, EXPERT_DIM, d), F32),
                            pltpu.VMEM((d, EXPERT_DIM), BF16), pltpu.VMEM((d, EXPERT_DIM), BF16),
                            pltpu.VMEM((EXPERT_DIM, d), BF16), pltpu.SemaphoreType.DMA((2,))],
        ),
        out_shape=jax.ShapeDtypeStruct(xs.shape, jnp.uint32),
        compiler_params=_cparams(1),
        name="moe_experts",
    )(tile_expert, next_expert, group_idx, n_used, xs, wg, wu, wd)


def _combine_kernel(d_ref, dn_ref, ys_hbm, w_ref, h_ref, o_ref, buf, sem, *, tm, nt):
    i = pl.program_id(0)
    slot = jnp.bitwise_and(i, 1)
    nxt = 1 - slot
    rs = ROW_SUB

    def start(idx_ref, n, s):
        for j in range(TOP_K):
            src = ys_hbm.at[pl.ds(pl.multiple_of(idx_ref[n * TOP_K + j], rs), rs)]
            pltpu.make_async_copy(src, buf.at[s, j, pl.ds(n * rs, rs)], sem.at[s]).start(priority=j % 2)

    def wait_all(s):
        for j in range(TOP_K):
            pltpu.make_async_copy(ys_hbm.at[pl.ds(0, tm * rs)], buf.at[s, j], sem.at[s]).wait()

    @pl.when(i == 0)
    def _():
        def body(n, c):
            for j in range(TOP_K):
                src = ys_hbm.at[pl.ds(pl.multiple_of(d_ref[n * TOP_K + j], rs), rs)]
                dst = buf.at[0, j, pl.ds(pl.multiple_of(n * rs, rs), rs)]
                pltpu.make_async_copy(src, dst, sem.at[0]).start(priority=j % 2)
            return c

        lax.fori_loop(0, tm, body, 0)

    per = tm // (rs * TOP_K)
    batches = iter(range(0, tm, per))

    def start_batch():
        n0 = next(batches)
        for n in range(n0, n0 + per):
            start(dn_ref, n, nxt)

    d2 = h_ref.shape[1] // 2
    o_ref[...] = h_ref[...]
    wait_all(slot)
    wts = w_ref[...]
    wj = [wts[:, j:j + 1] for j in range(TOP_K)]
    for s in range(rs):
        r_lo = jnp.zeros((tm, 128), F32)
        r_hi = jnp.zeros((tm, 128), F32)
        for j in range(TOP_K):
            start_batch()
            ylo, yhi = _unpack_pair(buf[slot, j, pl.ds(s, tm, stride=rs), :])
            r_lo = r_lo + wj[j] * ylo
            r_hi = r_hi + wj[j] * yhi
        o_ref[:, s * 128:(s + 1) * 128] += r_lo
        o_ref[:, d2 + s * 128:d2 + (s + 1) * 128] += r_hi

    @pl.when(i == nt - 1)
    def _():
        wait_all(nxt)


def _combine(dest, ys, wcol, h, tm):
    m, d = h.shape
    tm = min(tm, m)
    nt = m // tm
    idx = lambda f: pl.BlockSpec((tm * TOP_K,), f, memory_space=pltpu.SMEM)
    return pl.pallas_call(
        functools.partial(_combine_kernel, tm=tm, nt=nt),
        grid=(nt,),
        in_specs=[idx(lambda i: (i,)), idx(lambda i: (jnp.minimum(i + 1, nt - 1),)),
                  pl.BlockSpec(memory_space=pl.ANY),
                  pl.BlockSpec((tm, N_EXPERTS), lambda i: (i, 0)),
                  pl.BlockSpec((tm, d), lambda i: (i, 0))],
        out_specs=pl.BlockSpec((tm, d), lambda i: (i, 0)),
        out_shape=jax.ShapeDtypeStruct((m, d), F32),
        scratch_shapes=[pltpu.VMEM((2, TOP_K, tm * ROW_SUB, 128), jnp.uint32),
                        pltpu.SemaphoreType.DMA((2,))],
        compiler_params=_cparams(1),
        name="moe_combine",
    )(dest, dest, ys, wcol, h)


def _moe(h1, hnp, lg_t, router_bias, w_gate, w_up, w_down, ws_gate, ws_up, ws_down):
    n = h1.shape[0]
    tile = MOE_TILE
    eidx, rank, wcol, cnt = _router(lg_t, router_bias.reshape(-1, 1), 512)
    counts = cnt[:, 0].astype(jnp.int32)
    tiles_per = (counts + tile - 1) // tile
    tile_end = jnp.cumsum(tiles_per)
    tile_start = tile_end - tiles_per
    n_tiles = (n * TOP_K) // tile + N_EXPERTS
    rows = n_tiles * tile
    tile_expert = jnp.minimum(jnp.sum(tile_end[None, :] <= jnp.arange(n_tiles)[:, None], axis=1),
                              N_EXPERTS - 1).astype(jnp.int32)
    n_used = tile_end[-1:].astype(jnp.int32)
    ids = jnp.arange(N_EXPERTS)
    later = jnp.logical_and(ids[None, :] > ids[:, None], tiles_per[None, :] > 0)
    nxt_of = jnp.min(jnp.where(later, ids[None, :], N_EXPERTS), axis=1)
    nxt_of = jnp.where(nxt_of == N_EXPERTS, -1, nxt_of)
    ord_of = jnp.cumsum((tiles_per > 0).astype(jnp.int32)) - 1
    onehot = tile_expert[:, None] == ids[None, :]
    next_expert = jnp.sum(jnp.where(onehot, nxt_of[None, :], 0), axis=1).astype(jnp.int32)
    group_idx = jnp.sum(jnp.where(onehot, ord_of[None, :], 0), axis=1).astype(jnp.int32)
    last_tile = jnp.clip(tile_end - 1, 0, n_tiles - 1).astype(jnp.int32)
    dest = _dest((tile_start * tile).astype(jnp.int32), eidx, rank, 2048)
    dest = dest.T.reshape(-1)
    xs, hs = _scatter_rows(last_tile, dest, hnp, h1, ws_gate.astype(BF16), ws_up.astype(BF16),
                           ws_down.astype(BF16), rows, tile, 512)
    ys = _experts(tile_expert, next_expert, group_idx, n_used, xs, w_gate, w_up, w_down, tile)
    return _combine(dest, ys, wcol, hs, 256)


def _ple_kernel(h_ref, p_ref, gin_ref, wg_ref, bg_ref, wp_ref, gout_ref, o_ref):
    h = h_ref[...]
    gate = jax.nn.sigmoid(_dot(_rms(h, gin_ref[...]).astype(BF16), wg_ref[...]) + bg_ref[...])
    pp = _dot(p_ref[...].astype(BF16), wp_ref[...])
    o_ref[...] = h + _rms(pp * gate, gout_ref[...])


def _ple(h, p, g_in, w_g, b_g, w_p, g_out, tm):
    m, d = h.shape
    tm = min(tm, m)
    full = lambda a: pl.BlockSpec(a.shape, lambda i: (0,) * a.ndim)
    return pl.pallas_call(
        _ple_kernel,
        grid=(m // tm,),
        in_specs=[pl.BlockSpec((tm, d), lambda i: (i, 0)),
                  pl.BlockSpec((tm, p.shape[1]), lambda i: (i, 0)),
                  full(g_in), full(w_g), full(b_g), full(w_p), full(g_out)],
        out_specs=pl.BlockSpec((tm, d), lambda i: (i, 0)),
        out_shape=jax.ShapeDtypeStruct((m, d), F32),
        compiler_params=_cparams(1),
        name="ple",
    )(h, p, g_in, w_g, b_g, w_p, g_out)


def _pad_rows(a, rows):
    return jnp.pad(a, ((0, rows - a.shape[0]), (0, 0)))


def _row(a):
    return a.reshape(1, -1)


def _proj_layout(a):
    def padded(x, width):
        return jnp.pad(x, [(0, 0)] * (x.ndim - 1) + [(0, width - x.shape[-1])])

    parts = [padded(a[..., :MLA_COLS], MLA_COLS_PAD), a[..., MLA_COLS:MLA_COLS + 3 * RW_C]]
    off = MLA_COLS + 3 * RW_C
    for width, wide in ((RW_DECAY_LORA, 128), (RW_A_LORA, 128), (RW_GATE_LORA, 256)):
        parts.append(padded(a[..., off:off + width], wide))
        off += width
    return jnp.concatenate(parts, axis=-1)


def _mla_from_proj(proj, positions, g_qa, w_uq, g_kva, w_ukv, g_qn, g_kn):
    b, t, _ = proj.shape
    wq = w_uq.reshape(MLA_Q_LORA, MLA_HEADS, MLA_QK)
    wq = jnp.concatenate([wq[:, :, :MLA_NOPE].reshape(MLA_Q_LORA, -1),
                          wq[:, :, MLA_NOPE:].reshape(MLA_Q_LORA, -1)], axis=1).astype(BF16)
    half = MLA_ROPE // 2
    inv = ROPE_THETA ** (-jnp.arange(half, dtype=F32) / half)
    invf = _row(jnp.concatenate([inv, inv]))
    sgn = _row(jnp.concatenate([-jnp.ones(half, F32), jnp.ones(half, F32)]))
    q, k, v = _mla_prep(proj, positions.reshape(b, t, 1), invf, sgn, _row(g_qa), wq, _row(g_kva),
                        w_ukv.astype(BF16), _row(g_qn), _row(g_kn), 512)
    return _attention(q, k, v, 1024, 512, 2)


def _rwkv_from_proj(proj, mu, w0, w_w2, a0, a_w2, g_w2, k_k, k_a, r_k, ln_w, ln_b):
    head_of = jnp.arange(RW_C) // RW_HEAD
    bd = (head_of[:, None] == head_of[None, :]).astype(BF16)
    mu_all = jnp.concatenate([jnp.zeros((MLA_COLS,), F32), mu])
    rr, ww, kx, vx, kkx, bx, gx = _rw_prep(
        proj, _row(_proj_layout(mu_all)), _row(w0), _pad_rows(w_w2, 128).astype(BF16), _row(a0),
        _pad_rows(a_w2, 128).astype(BF16), _pad_rows(g_w2, 256).astype(BF16), _row(k_k), _row(k_a),
        bd, 256)
    return _rw_chunk(rr, ww, kx, vx, kkx, bx, gx, _row(r_k), _row(ln_w), _row(ln_b))


def _layer(h, p, positions, g_mix, w_in, mla_g_qa, mla_w_uq, mla_g_kva, mla_w_ukv, mla_g_qn,
           mla_g_kn, rw_mu, rw_w0, rw_w_w2, rw_a0, rw_a_w2, rw_g_w2, rw_k_k, rw_k_a, rw_r_k,
           rw_ln_w, rw_ln_b, w_out, g_ffn, w_router, router_bias, w_gate, w_up, w_down, ws_gate,
           ws_up, ws_down, g_ple_in, w_ple_gate, b_ple_gate, w_ple_proj, g_ple_out):
    b, t, d = h.shape
    n = b * t
    x2 = h.reshape(n, d)

    proj = _in_proj(x2, _row(g_mix), _proj_layout(w_in).astype(BF16), 1024).reshape(b, t, -1)
    y_mla = _mla_from_proj(proj, positions, mla_g_qa, mla_w_uq, mla_g_kva, mla_w_ukv, mla_g_qn, mla_g_kn)
    y_rw = _rwkv_from_proj(proj, rw_mu, rw_w0, rw_w_w2, rw_a0, rw_a_w2, rw_g_w2, rw_k_k, rw_k_a, rw_r_k,
                           rw_ln_w, rw_ln_b)

    h1, hnp, lg_t = _out_proj(y_mla.reshape(n, -1), y_rw.reshape(n, -1), x2, w_out.astype(BF16),
                              _row(g_ffn), w_router, 512)
    h2 = _moe(h1, hnp, lg_t, router_bias, w_gate, w_up, w_down, ws_gate, ws_up, ws_down)

    h3 = _ple(h2, p.reshape(n, -1), _row(g_ple_in), w_ple_gate.astype(BF16), _row(b_ple_gate),
              w_ple_proj.astype(BF16), _row(g_ple_out), 512)
    return h3.reshape(b, t, d)


def kernel(x, p, positions, g_mix, w_in, mla_g_qa, mla_w_uq, mla_g_kva, mla_w_ukv, mla_g_qn, mla_g_kn, rw_mu, rw_w0, rw_w_w2, rw_a0, rw_a_w2, rw_g_w2, rw_k_k, rw_k_a, rw_r_k, rw_ln_w, rw_ln_b, w_out, g_ffn, w_router, router_bias, w_gate, w_up, w_down, ws_gate, ws_up, ws_down, g_ple_in, w_ple_gate, b_ple_gate, w_ple_proj, g_ple_out):
    params = (g_mix, w_in, mla_g_qa, mla_w_uq, mla_g_kva, mla_w_ukv, mla_g_qn, mla_g_kn, rw_mu,
              rw_w0, rw_w_w2, rw_a0, rw_a_w2, rw_g_w2, rw_k_k, rw_k_a, rw_r_k, rw_ln_w, rw_ln_b,
              w_out, g_ffn, w_router, router_bias, w_gate, w_up, w_down, ws_gate, ws_up, ws_down,
              g_ple_in, w_ple_gate, b_ple_gate, w_ple_proj, g_ple_out)
    h = x
    for i in range(g_mix.shape[0]):
        h = _layer(h, p[i], positions, *[a[i] for a in params])
    return h
```

```python
import functools

import jax
import jax.numpy as jnp
from jax import lax
from jax.experimental import pallas as pl
from jax.experimental.pallas import tpu as pltpu

F32 = jnp.float32
BF16 = jnp.bfloat16

D_MODEL = 2048
PLE_DIM = 256
RMS_EPS = 1e-6

MLA_HEADS = 8
MLA_NOPE = 128
MLA_ROPE = 64
MLA_QK = MLA_NOPE + MLA_ROPE
MLA_V = 128
MLA_Q_LORA = 512
MLA_KV_LORA = 256
ROPE_THETA = 10000.0
MLA_COLS = MLA_Q_LORA + MLA_KV_LORA + MLA_ROPE
MLA_COLS_PAD = 1024

RW_HEADS = 16
RW_HEAD = 64
RW_C = RW_HEADS * RW_HEAD
RW_DECAY_LORA = 64
RW_A_LORA = 64
RW_GATE_LORA = 160
RW_GN_EPS = 64e-5
RW_LORA_PAD = 512
PROJ_COLS = MLA_COLS_PAD + 3 * RW_C + RW_LORA_PAD
PROJ_TN = 1536
RW_CHUNK = 64

N_EXPERTS = 64
TOP_K = 8
N_GROUPS = 8
TOPK_GROUPS = 4
EXPERT_DIM = 512
ROUTED_SCALE = 2.5
MOE_TILE = 512
ROW_SUB = D_MODEL // 2 // 128

VMEM_LIMIT = 56 * 1024 * 1024

TM_IN_PROJ = 1024
TM_MLA_PREP = 512
ATTN_TQ = 1024
ATTN_TK = 512
ATTN_HEADS_PER_STEP = 2
TM_RW_PREP = 256
TM_OUT_PROJ = 512
TM_ROUTER = 512
TM_DEST = 2048
TM_SCATTER = 512
TM_COMBINE = 256
TM_PLE = 512


def _cparams(n_axes):
    return pltpu.CompilerParams(dimension_semantics=("arbitrary",) * n_axes,
                                vmem_limit_bytes=VMEM_LIMIT)


def _rms(x, g):
    ms = jnp.mean(x * x, axis=-1, keepdims=True)
    return x * lax.rsqrt(ms + RMS_EPS) * g


def _dot(a, b):
    return jnp.dot(a, b, preferred_element_type=F32)


def _dot_nt(a, b):
    return lax.dot_general(a, b, (((1,), (1,)), ((), ())), preferred_element_type=F32)


def _dot_tn(a, b):
    return lax.dot_general(a, b, (((0,), (0,)), ((), ())), preferred_element_type=F32)


def _split2(x):
    hi = x.astype(BF16)
    lo = (x - hi.astype(F32)).astype(BF16)
    return hi, lo


def _in_proj_kernel(x_ref, g_ref, w_ref, o_ref, xn_ref):
    @pl.when(pl.program_id(1) == 0)
    def _():
        xn_ref[...] = _rms(x_ref[...], g_ref[...]).astype(BF16)

    o_ref[...] = _dot(xn_ref[...], w_ref[...]).astype(o_ref.dtype)


def _in_proj(x, g, w, tm):
    m, k = x.shape
    n = w.shape[1]
    tn = PROJ_TN
    tm = min(tm, m)
    return pl.pallas_call(
        _in_proj_kernel,
        grid=(m // tm, n // tn),
        in_specs=[pl.BlockSpec((tm, k), lambda i, j: (i, 0)),
                  pl.BlockSpec((1, k), lambda i, j: (0, 0)),
                  pl.BlockSpec((k, tn), lambda i, j: (0, j))],
        out_specs=pl.BlockSpec((tm, tn), lambda i, j: (i, j)),
        out_shape=jax.ShapeDtypeStruct((m, n), BF16),
        scratch_shapes=[pltpu.VMEM((tm, k), BF16)],
        compiler_params=_cparams(2),
        name="in_proj",
    )(x, g, w)


def _mla_prep_kernel(pm_ref, pos_ref, invf_ref, sgn_ref, gqa_ref, wuq_ref, gkva_ref, wukv_ref,
                     gqn_ref, gkn_ref, q_ref, k_ref, v_ref):
    pm = pm_ref[0].astype(F32)
    ang = pos_ref[0].astype(F32) * invf_ref[...]
    cos = jnp.cos(ang)
    sin = jnp.sin(ang) * sgn_ref[...]

    def rope(xr):
        half = MLA_ROPE // 2
        swapped = jnp.concatenate([xr[:, half:], xr[:, :half]], axis=1)
        return xr * cos + swapped * sin

    scale = MLA_QK ** -0.5 * 1.4426950408889634
    gqn = gqn_ref[...]
    gkn = gkn_ref[...]

    cq = _rms(pm[:, :MLA_Q_LORA], gqa_ref[...]).astype(BF16)
    q = _dot(cq, wuq_ref[...])
    ckv = _rms(pm[:, MLA_Q_LORA:MLA_Q_LORA + MLA_KV_LORA], gkva_ref[...]).astype(BF16)
    kv = _dot(ckv, wukv_ref[...])
    kpe = pm[:, MLA_Q_LORA + MLA_KV_LORA:MLA_COLS]
    kpe_ssq = jnp.sum(kpe * kpe, axis=-1, keepdims=True)
    kpe_rot = rope(kpe * gkn[:, MLA_NOPE:])

    for h in range(MLA_HEADS):
        qn = q[:, h * MLA_NOPE:(h + 1) * MLA_NOPE]
        qr = q[:, MLA_HEADS * MLA_NOPE + h * MLA_ROPE:MLA_HEADS * MLA_NOPE + (h + 1) * MLA_ROPE]
        ssq = jnp.sum(qn * qn, axis=-1, keepdims=True) + jnp.sum(qr * qr, axis=-1, keepdims=True)
        inv = lax.rsqrt(ssq * (1.0 / MLA_QK) + RMS_EPS) * scale
        q_ref[0, h, :, 0:MLA_NOPE] = (qn * inv * gqn[:, :MLA_NOPE]).astype(BF16)
        q_ref[0, h, :, MLA_NOPE:MLA_QK] = (rope(qr * gqn[:, MLA_NOPE:]) * inv).astype(BF16)

        kn = kv[:, h * 256:h * 256 + MLA_NOPE]
        ssq = jnp.sum(kn * kn, axis=-1, keepdims=True) + kpe_ssq
        inv = lax.rsqrt(ssq * (1.0 / MLA_QK) + RMS_EPS)
        k_ref[0, h, :, 0:MLA_NOPE] = (kn * inv * gkn[:, :MLA_NOPE]).astype(BF16)
        k_ref[0, h, :, MLA_NOPE:MLA_QK] = (kpe_rot * inv).astype(BF16)
        v_ref[0, h, :, 0:MLA_V] = kv[:, h * 256 + MLA_NOPE:(h + 1) * 256].astype(BF16)
        v_ref[0, h, :, MLA_V:] = jnp.ones((kv.shape[0], MLA_V), BF16)


def _mla_prep(pm, pos, invf, sgn, g_qa, w_uq, g_kva, w_ukv, g_qn, g_kn, tm):
    b, t, _ = pm.shape
    tm = min(tm, t)
    full = lambda a: pl.BlockSpec(a.shape, lambda i, j: (0,) * a.ndim)
    qk_shape = jax.ShapeDtypeStruct((b, MLA_HEADS, t, MLA_QK), BF16)
    return pl.pallas_call(
        _mla_prep_kernel,
        grid=(b, t // tm),
        in_specs=[pl.BlockSpec((1, tm, MLA_COLS_PAD), lambda i, j: (i, j, 0)),
                  pl.BlockSpec((1, tm, 1), lambda i, j: (i, j, 0)),
                  full(invf), full(sgn), full(g_qa), full(w_uq), full(g_kva), full(w_ukv),
                  full(g_qn), full(g_kn)],
        out_specs=[pl.BlockSpec((1, MLA_HEADS, tm, MLA_QK), lambda i, j: (i, 0, j, 0)),
                   pl.BlockSpec((1, MLA_HEADS, tm, MLA_QK), lambda i, j: (i, 0, j, 0)),
                   pl.BlockSpec((1, MLA_HEADS, tm, 2 * MLA_V), lambda i, j: (i, 0, j, 0))],
        out_shape=[qk_shape, qk_shape, jax.ShapeDtypeStruct((b, MLA_HEADS, t, 2 * MLA_V), BF16)],
        compiler_params=_cparams(2),
        name="mla_prep",
    )(pm, pos, invf, sgn, g_qa, w_uq, g_kva, w_ukv, g_qn, g_kn)


def _attn_kernel(q_ref, k_ref, v_ref, o_ref, m_ref, acc_ref, *, tq, tk, nh):
    qi = pl.program_id(2)
    m_ref[...] = jnp.full(m_ref.shape, -jnp.inf, F32)
    acc_ref[...] = jnp.zeros(acc_ref.shape, F32)
    heads = range(nh)
    def scores(j, r0, nr):
        start = pl.multiple_of(j * tk, tk)
        return [_dot_nt(q_ref[0, h, r0:r0 + nr, :], k_ref[0, h, pl.ds(start, tk), :]) for h in heads]

    def update(j, r0, nr, ss, triangle):
        start = pl.multiple_of(j * tk, tk)
        rows = slice(r0, r0 + nr)
        for h in heads:
            s = ss[h]
            if triangle:
                keep = (lax.broadcasted_iota(jnp.int32, (nr, tk), 1)
                        <= lax.broadcasted_iota(jnp.int32, (nr, tk), 0))
                s = jnp.where(keep, s, -jnp.inf)
            m_old = m_ref[h, rows, :]
            m_new = jnp.maximum(m_old, jnp.max(s, axis=-1, keepdims=True))
            alpha = jnp.exp2(m_old - m_new)
            p = jnp.exp2(s - jnp.concatenate([m_new] * (tk // 128), axis=1))
            pv = _dot(p.astype(BF16), v_ref[0, h, pl.ds(start, tk), :])
            acc_ref[h, rows, :] = jnp.concatenate([alpha, alpha], axis=1) * acc_ref[h, rows, :] + pv
            m_ref[h, rows, :] = m_new

    per = tq // tk
    n_full = qi * per

    def body(jj, c):
        ss = [scores(jj * per + d, 0, tq) for d in range(per)]
        for d in range(per):
            update(jj * per + d, 0, tq, ss[d], False)
        return c

    lax.fori_loop(0, qi, body, 0)
    blocks = []
    for d in range(per):
        blocks.append((n_full + d, d * tk, tk, True))
        if d + 1 < per:
            blocks.append((n_full + d, (d + 1) * tk, tq - (d + 1) * tk, False))
    ss = [scores(j, r0, nr) for j, r0, nr, _ in blocks]
    for (j, r0, nr, tri), s in zip(blocks, ss):
        update(j, r0, nr, s, tri)
    for h in heads:
        acc = acc_ref[h]
        o_ref[0, :, h * MLA_V:(h + 1) * MLA_V] = (acc[:, :MLA_V] / acc[:, MLA_V:]).astype(o_ref.dtype)


def _attention(q, k, v, tq, tk, nh):
    b, h, t, _ = q.shape
    tq = min(tq, t)
    tk = min(tk, tq)
    return pl.pallas_call(
        functools.partial(_attn_kernel, tq=tq, tk=tk, nh=nh),
        grid=(b, h // nh, t // tq),
        in_specs=[pl.BlockSpec((1, nh, tq, MLA_QK), lambda i, j, n: (i, j, n, 0)),
                  pl.BlockSpec((1, nh, t, MLA_QK), lambda i, j, n: (i, j, 0, 0)),
                  pl.BlockSpec((1, nh, t, 2 * MLA_V), lambda i, j, n: (i, j, 0, 0))],
        out_specs=pl.BlockSpec((1, tq, nh * MLA_V), lambda i, j, n: (i, n, j)),
        out_shape=jax.ShapeDtypeStruct((b, t, h * MLA_V), BF16),
        scratch_shapes=[pltpu.VMEM((nh, tq, 128), F32), pltpu.VMEM((nh, tq, 2 * MLA_V), F32)],
        compiler_params=_cparams(3),
        name="mla_attention",
    )(q, k, v)


def _rw_prep_kernel(r_ref, k_ref, v_ref, l_ref, rp_ref, kp_ref, vp_ref, lp_ref, mu_ref, w0_ref,
                    ww2_ref, a0_ref, aw2_ref, gw2_ref, kk_ref, ka_ref, bd_ref,
                    r_o, w_o, k_o, v_o, kk_o, b_o, g_o):
    first = pl.program_id(1) == 0
    c = RW_C

    def shifted(ref, pref, mu):
        z = ref[0].astype(F32)
        prev = pref[0][15:16, :].astype(F32)
        prev = jnp.where(first, jnp.zeros_like(prev), prev)
        row = lax.broadcasted_iota(jnp.int32, z.shape, 0)
        zs = jnp.where(row == 0, prev, pltpu.roll(z, 1, axis=0))
        return z + (zs - z) * mu

    o = MLA_COLS_PAD
    r = shifted(r_ref, rp_ref, mu_ref[:, o:o + c])
    k = shifted(k_ref, kp_ref, mu_ref[:, o + c:o + 2 * c])
    v = shifted(v_ref, vp_ref, mu_ref[:, o + 2 * c:o + 3 * c])
    lz = shifted(l_ref, lp_ref, mu_ref[:, o + 3 * c:])
    wl = lz[:, 0:128]
    al = lz[:, 128:256]
    gl = lz[:, 256:512]

    f = w0_ref[...] + _dot(jnp.tanh(wl).astype(BF16), ww2_ref[...])
    softplus_neg_f = jnp.maximum(-f, 0.0) + jnp.log(1.0 + jnp.exp(-jnp.abs(f)))
    w = -softplus_neg_f - 0.5
    w_o[0] = -jnp.exp(w)
    a = jax.nn.sigmoid(a0_ref[...] + _dot(al.astype(BF16), aw2_ref[...]))
    g_o[0] = _dot(jax.nn.sigmoid(gl).astype(BF16), gw2_ref[...]).astype(g_o.dtype)

    kk = k * kk_ref[...]
    hi, lo = _split2(kk * kk)
    ssq = _dot(hi, bd_ref[...]) + _dot(lo, bd_ref[...])
    kk = kk * lax.rsqrt(jnp.maximum(ssq, 1e-24))
    r_o[0] = r.astype(r_o.dtype)
    k_o[0] = (k * (1.0 + (a - 1.0) * ka_ref[...])).astype(k_o.dtype)
    v_o[0] = v.astype(v_o.dtype)
    kk_o[0] = kk.astype(kk_o.dtype)
    b_o[0] = (kk * a).astype(b_o.dtype)


def _rw_prep(proj, mu, w0, ww2, a0, aw2, gw2, k_k, k_a, bd, tm):
    b, t, _ = proj.shape
    tm = min(tm, t)
    c = RW_C
    full = lambda a: pl.BlockSpec(a.shape, lambda i, j: (0,) * a.ndim)
    cur = lambda w, cb: pl.BlockSpec((1, tm, w), lambda i, j: (i, j, cb))
    prev = lambda w, cb: pl.BlockSpec((1, 16, w), lambda i, j: (i, jnp.maximum(j * (tm // 16) - 1, 0), cb))
    lora_cb = (MLA_COLS_PAD + 3 * c) // RW_LORA_PAD
    groups = [(c, 1), (c, 2), (c, 3), (RW_LORA_PAD, lora_cb)]
    out = [jax.ShapeDtypeStruct((b, t, c), F32 if i == 1 else BF16) for i in range(7)]
    ospec = pl.BlockSpec((1, tm, c), lambda i, j: (i, j, 0))
    return pl.pallas_call(
        _rw_prep_kernel,
        grid=(b, t // tm),
        in_specs=[cur(*g) for g in groups] + [prev(*g) for g in groups]
        + [full(mu), full(w0), full(ww2), full(a0), full(aw2), full(gw2), full(k_k), full(k_a), full(bd)],
        out_specs=[ospec] * 7,
        out_shape=out,
        compiler_params=_cparams(2),
        name="rwkv_prep",
    )(*([proj] * 8), mu, w0, ww2, a0, aw2, gw2, k_k, k_a, bd)


def _rw_chunk_kernel(r_ref, w_ref, k_ref, v_ref, kk_ref, b_ref, g_ref, rk_ref, lnw_ref, lnb_ref,
                     o_ref, s_ref):
    cs = RW_CHUNK
    hd = RW_HEAD

    @pl.when(pl.program_id(1) == 0)
    def _():
        s_ref[...] = jnp.zeros(s_ref.shape, F32)

    r = r_ref[0].astype(F32)
    w = w_ref[0]
    k = k_ref[0].astype(F32)
    v = v_ref[0].astype(F32)
    kk = kk_ref[0].astype(F32)
    b = b_ref[0].astype(F32)
    pairs = range(RW_HEADS // 2)
    pw = 2 * hd

    row = lax.broadcasted_iota(jnp.int32, (cs, pw), 0)
    col = lax.broadcasted_iota(jnp.int32, (cs, pw), 1)
    colh = jnp.bitwise_and(col, hd - 1)
    lo = col < hd
    strict = row > colh
    eye = jnp.where(row == colh, 1.0, 0.0).astype(F32)
    row2 = lax.broadcasted_iota(jnp.int32, (cs, 2 * pw), 0)
    col2 = lax.broadcasted_iota(jnp.int32, (cs, 2 * pw), 1)
    incl2 = row2 >= jnp.bitwise_and(col2, hd - 1)
    row3 = lax.broadcasted_iota(jnp.int32, (cs, 3 * cs), 0)
    col3 = lax.broadcasted_iota(jnp.int32, (cs, 3 * cs), 1)
    tri3 = jnp.where(row3 >= jnp.bitwise_and(col3, cs - 1), 1.0, 0.0).astype(BF16)

    def bdiag(x):
        zero = jnp.zeros_like(x)
        return jnp.concatenate([jnp.where(lo, x, zero), jnp.where(lo, zero, x)], axis=0)

    def cat3_lhs(x):
        hi = x.astype(BF16)
        return jnp.concatenate([hi, (x - hi.astype(F32)).astype(BF16), hi], axis=1)

    def cat3_rhs(x):
        hi, lw = _split2(x)
        bh = bdiag(hi)
        return jnp.concatenate([bh, bh, bdiag(lw)], axis=0)

    w1 = w.astype(BF16)
    w2 = (w - w1.astype(F32)).astype(BF16)
    w3 = (w - w1.astype(F32) - w2.astype(F32)).astype(BF16)
    logp = _dot(tri3, jnp.concatenate([w1, w2, w3], axis=0))
    logp_end = logp[cs - 1:cs, :]
    p_tail = jnp.exp(logp_end - logp)
    p_end = jnp.exp(logp_end)
    p_inv = jnp.exp(-logp)

    a_t = -kk * jnp.exp(logp - w)
    r_t = r * jnp.exp(logp)
    b_t = b * p_inv
    k_t = k * p_inv
    b_h = b * p_tail
    k_h = k * p_tail

    sls = [slice(p * pw, (p + 1) * pw) for p in pairs]
    lhs = [jnp.concatenate([a_t[:, sl], r_t[:, sl]], axis=0).astype(BF16) for sl in sls]
    rhs = [jnp.concatenate([bdiag(b_t[:, sl].astype(BF16)), bdiag(k_t[:, sl].astype(BF16))], axis=0)
           for sl in sls]
    gm = [_dot_nt(lhs[p], rhs[p]) for p in pairs]
    s0 = [s_ref[p] for p in pairs]
    ls = [_dot_nt(lhs[p], bdiag(s0[p].astype(BF16))) for p in pairs]
    vb = [v[:, sl].astype(BF16) for sl in sls]
    bv = [bdiag(vb[p]) for p in pairs]
    a_ak = [jnp.where(strict, gm[p][:cs, pw:], 0.0).astype(BF16) for p in pairs]
    rhs_u = [ls[p][:cs] + _dot(a_ak[p], bv[p]) for p in pairs]

    a_ab = [jnp.where(strict, gm[p][:cs, :pw], 0.0) for p in pairs]
    rb = lambda m: jnp.right_shift(row, m)
    cb = lambda m: jnp.right_shift(colh, m)
    x = [jnp.where(rb(3) == cb(3), a_ab[p], 0.0) for p in pairs]
    tm = [eye + x[p] for p in pairs]
    x = [_dot(cat3_lhs(x[p]), cat3_rhs(x[p])) for p in pairs]
    both = [_dot(cat3_lhs(jnp.concatenate([x[p], tm[p]], axis=0)), cat3_rhs(x[p])) for p in pairs]
    x = [both[p][:cs] for p in pairs]
    tm = [tm[p] + both[p][cs:] for p in pairs]
    tm = [tm[p] + _dot(cat3_lhs(tm[p]), cat3_rhs(x[p])) for p in pairs]
    for m in (3, 4, 5):
        off = jnp.logical_and(rb(m + 1) == cb(m + 1),
                              jnp.logical_and(jnp.bitwise_and(rb(m), 1) == 1, jnp.bitwise_and(cb(m), 1) == 0))
        bt = [_dot(cat3_lhs(jnp.where(off, a_ab[p], 0.0)), cat3_rhs(tm[p])) for p in pairs]
        tm = [tm[p] + _dot(cat3_lhs(tm[p]), cat3_rhs(bt[p])) for p in pairs]

    u = [_dot(cat3_lhs(tm[p]), cat3_rhs(rhs_u[p])) for p in pairs]
    ub = [u[p].astype(BF16) for p in pairs]
    a_r = [jnp.where(incl2, gm[p][cs:, :], 0.0).astype(BF16) for p in pairs]
    y = [ls[p][cs:] + _dot(a_r[p], jnp.concatenate([bdiag(ub[p]), bv[p]], axis=0)) for p in pairs]
    for p in pairs:
        uv = jnp.concatenate([ub[p], vb[p]], axis=0)
        bk = jnp.concatenate([b_h[:, sls[p]], k_h[:, sls[p]]], axis=0).astype(BF16)
        full = _dot_tn(uv, bk)
        s_ref[p] = s0[p] * p_end[:, sls[p]] + jnp.where(lo, full[:hd], full[hd:])

    def head_sum(t):
        s_lo = jnp.sum(jnp.where(lo, t, 0.0), axis=-1, keepdims=True)
        s_all = jnp.sum(t, axis=-1, keepdims=True)
        return jnp.where(lo, s_lo, s_all - s_lo)

    rk_all = r * k * rk_ref[...]
    g = g_ref[0].astype(F32)
    for p in pairs:
        sl = sls[p]
        yc = y[p] - head_sum(y[p]) * (1.0 / hd)
        var = head_sum(yc * yc) * (1.0 / hd)
        yn = yc * lax.rsqrt(var + RW_GN_EPS) * lnw_ref[:, sl] + lnb_ref[:, sl]
        bonus = head_sum(rk_all[:, sl]) * v[:, sl]
        o_ref[0, :, sl] = ((yn + bonus) * g[:, sl]).astype(o_ref.dtype)


def _rw_chunk(r, w, k, v, kk, bb, g, r_k, ln_w, ln_b):
    b, t, c = r.shape
    cs = RW_CHUNK
    full = lambda a: pl.BlockSpec(a.shape, lambda i, j: (0,) * a.ndim)
    spec = pl.BlockSpec((1, cs, c), lambda i, j: (i, j, 0))
    return pl.pallas_call(
        _rw_chunk_kernel,
        grid=(b, t // cs),
        in_specs=[spec] * 7 + [full(r_k), full(ln_w), full(ln_b)],
        out_specs=spec,
        out_shape=jax.ShapeDtypeStruct((b, t, c), BF16),
        scratch_shapes=[pltpu.VMEM((RW_HEADS // 2, RW_HEAD, 2 * RW_HEAD), F32)],
        compiler_params=_cparams(2),
        name="rwkv_chunk",
    )(r, w, k, v, kk, bb, g, r_k, ln_w, ln_b)


def _pack_pair(a, b):
    ua = lax.bitcast_convert_type(a.astype(BF16).astype(F32), jnp.uint32)
    ub = lax.bitcast_convert_type(b.astype(BF16).astype(F32), jnp.uint32)
    return lax.shift_right_logical(ua, jnp.uint32(16)) | (ub & jnp.uint32(0xFFFF0000))


def _unpack_pair(u):
    lo = lax.bitcast_convert_type(lax.shift_left(u, jnp.uint32(16)), F32)
    hi = lax.bitcast_convert_type(u & jnp.uint32(0xFFFF0000), F32)
    return lo, hi


def _store_rows(ref, packed):
    m = packed.shape[0]
    for s in range(ROW_SUB):
        ref[pl.ds(s, m, stride=ROW_SUB), :] = packed[:, s * 128:(s + 1) * 128]


def _load_rows(ref):
    m = ref.shape[-2] // ROW_SUB
    los, his = [], []
    for s in range(ROW_SUB):
        lo, hi = _unpack_pair(ref[pl.ds(s, m, stride=ROW_SUB), :])
        los.append(lo.astype(BF16))
        his.append(hi.astype(BF16))
    return jnp.concatenate(los, axis=1), jnp.concatenate(his, axis=1)


def _out_proj_kernel(ya_ref, yr_ref, x_ref, w_ref, g_ref, wrh_ref, wrl_ref, h_ref, hn_ref, lg_ref):
    half = ya_ref.shape[1]
    acc = _dot(ya_ref[...], w_ref[0:half, :]) + _dot(yr_ref[...], w_ref[half:, :])
    h = x_ref[...] + acc
    h_ref[...] = h
    hn = _rms(h, g_ref[...])
    d2 = hn.shape[1] // 2
    _store_rows(hn_ref, _pack_pair(hn[:, :d2], hn[:, d2:]))
    hh, hl = _split2(hn)
    lg_ref[...] = (_dot_nt(wrh_ref[...], hh) + _dot_nt(wrh_ref[...], hl)
                   + _dot_nt(wrl_ref[...], hh))


def _out_proj(ya, yr, x, w, g, wr, tm):
    wr_hi, wr_lo = _split2(wr.T)
    m, d = x.shape
    tm = min(tm, m)
    full = lambda a: pl.BlockSpec(a.shape, lambda i: (0,) * a.ndim)
    return pl.pallas_call(
        _out_proj_kernel,
        grid=(m // tm,),
        in_specs=[pl.BlockSpec((tm, ya.shape[1]), lambda i: (i, 0)),
                  pl.BlockSpec((tm, yr.shape[1]), lambda i: (i, 0)),
                  pl.BlockSpec((tm, d), lambda i: (i, 0)),
                  full(w), full(g), full(wr_hi), full(wr_lo)],
        out_specs=[pl.BlockSpec((tm, d), lambda i: (i, 0)),
                   pl.BlockSpec((tm * ROW_SUB, 128), lambda i: (i, 0)),
                   pl.BlockSpec((N_EXPERTS, tm), lambda i: (0, i))],
        out_shape=[jax.ShapeDtypeStruct((m, d), F32), jax.ShapeDtypeStruct((m * ROW_SUB, 128), jnp.uint32),
                   jax.ShapeDtypeStruct((N_EXPERTS, m), F32)],
        compiler_params=_cparams(1),
        name="out_proj_router_logits",
    )(ya, yr, x, w, g, wr_hi, wr_lo)


def _router_kernel(lg_ref, bias_ref, e_ref, r_ref, w_ref, cnt_ref, carry_ref):
    tm = lg_ref.shape[1]
    gsz = N_EXPERTS // N_GROUPS

    @pl.when(pl.program_id(0) == 0)
    def _():
        carry_ref[...] = jnp.zeros(carry_ref.shape, F32)

    scores = jax.nn.sigmoid(lg_ref[...]).reshape(N_GROUPS, gsz, tm)
    biased = scores + bias_ref[...].reshape(N_GROUPS, gsz, 1)
    neg = -jnp.inf

    eidx = lax.broadcasted_iota(jnp.int32, biased.shape, 1)
    m1 = jnp.max(biased, axis=1, keepdims=True)
    first = jnp.min(jnp.where(biased == m1, eidx, gsz), axis=1, keepdims=True)
    m2 = jnp.max(jnp.where(eidx == first, neg, biased), axis=1, keepdims=True)
    gscore = m1 + m2

    gidx = lax.broadcasted_iota(jnp.int32, gscore.shape, 0)
    gsel = jnp.zeros(gscore.shape, jnp.bool_)
    work = gscore
    for _ in range(TOPK_GROUPS):
        best = jnp.max(work, axis=0, keepdims=True)
        pick = jnp.min(jnp.where(work == best, gidx, N_GROUPS), axis=0, keepdims=True)
        hit = gidx == pick
        gsel = jnp.logical_or(gsel, hit)
        work = jnp.where(hit, neg, work)

    flat = lax.broadcasted_iota(jnp.int32, biased.shape, 0) * gsz + eidx
    work = jnp.where(gsel, biased, neg)
    hits = []
    picks = []
    for _ in range(TOP_K):
        best = jnp.max(jnp.max(work, axis=1, keepdims=True), axis=0, keepdims=True)
        cand = jnp.where(work == best, flat, N_EXPERTS)
        pick = jnp.min(jnp.min(cand, axis=1, keepdims=True), axis=0, keepdims=True)
        hit = flat == pick
        hits.append(hit)
        picks.append(pick)
        work = jnp.where(hit, neg, work)

    def pick_value(hit, val):
        s = jnp.sum(jnp.sum(jnp.where(hit, val, 0.0), axis=1, keepdims=True), axis=0, keepdims=True)
        return s.reshape(1, tm)

    sel = jnp.zeros(biased.shape, F32)
    for hit in hits:
        sel = jnp.where(hit, 1.0, sel)
    sel2 = sel.reshape(N_EXPERTS, tm)
    r_i = lax.broadcasted_iota(jnp.int32, (tm, tm), 0)
    c_i = lax.broadcasted_iota(jnp.int32, (tm, tm), 1)
    upper = jnp.where(r_i < c_i, 1.0, 0.0).astype(BF16)
    rank = _dot(sel2.astype(BF16), upper) + carry_ref[...]
    carry = carry_ref[...] + jnp.sum(sel2, axis=1, keepdims=True)
    carry_ref[...] = carry
    cnt_ref[...] = carry
    rank3 = rank.reshape(N_GROUPS, gsz, tm)

    raw = [pick_value(hit, scores) for hit in hits]
    tot = raw[0]
    for x in raw[1:]:
        tot = tot + x
    wrow = lax.broadcasted_iota(jnp.int32, (N_EXPERTS, tm), 0)
    wt = jnp.zeros((N_EXPERTS, tm), F32)
    for j in range(TOP_K):
        e_ref[j:j + 1, :] = picks[j].reshape(1, tm)
        r_ref[j:j + 1, :] = pick_value(hits[j], rank3).astype(jnp.int32)
        wt = jnp.where(wrow == j, raw[j] / tot * ROUTED_SCALE, wt)
    w_ref[...] = wt.T


def _router(lg_t, bias, tm):
    e, m = lg_t.shape
    tm = min(tm, m)
    row = pl.BlockSpec((TOP_K, tm), lambda i: (0, i))
    return pl.pallas_call(
        _router_kernel,
        grid=(m // tm,),
        in_specs=[pl.BlockSpec((e, tm), lambda i: (0, i)),
                  pl.BlockSpec((e, 1), lambda i: (0, 0))],
        out_specs=[row, row, pl.BlockSpec((tm, e), lambda i: (i, 0)),
                   pl.BlockSpec((e, 1), lambda i: (0, 0))],
        out_shape=[jax.ShapeDtypeStruct((TOP_K, m), jnp.int32),
                   jax.ShapeDtypeStruct((TOP_K, m), jnp.int32),
                   jax.ShapeDtypeStruct((m, e), F32),
                   jax.ShapeDtypeStruct((e, 1), F32)],
        scratch_shapes=[pltpu.VMEM((e, 1), F32)],
        compiler_params=_cparams(1),
        name="router_topk",
    )(lg_t, bias)


def _dest_kernel(off_ref, e_ref, r_ref, d_ref):
    e = e_ref[...]
    acc = r_ref[...]
    for x in range(N_EXPERTS):
        acc = acc + jnp.where(e == x, off_ref[x], 0)
    d_ref[...] = acc * ROW_SUB


def _dest(off, eidx, rank, tm):
    k, m = eidx.shape
    tm = min(tm, m)
    spec = pl.BlockSpec((k, tm), lambda i: (0, i))
    return pl.pallas_call(
        _dest_kernel,
        grid=(m // tm,),
        in_specs=[pl.BlockSpec(memory_space=pltpu.SMEM), spec, spec],
        out_specs=spec,
        out_shape=jax.ShapeDtypeStruct((k, m), jnp.int32),
        compiler_params=_cparams(1),
        name="moe_dest_rows",
    )(off, eidx, rank)


def _scatter_kernel(zt_ref, d_ref, x_ref, z_ref, h_ref, sg_ref, su_ref, sd_ref, xs_hbm, hs_ref, sem,
                    *, tm, tile):
    i = pl.program_id(0)
    rs = ROW_SUB

    @pl.when(i == 0)
    def _():
        def zero(e, c):
            start = pl.multiple_of(zt_ref[e] * (tile * rs), tile * rs)
            pltpu.make_async_copy(z_ref, xs_hbm.at[pl.ds(start, tile * rs)], sem).start()
            return c

        lax.fori_loop(0, N_EXPERTS, zero, 0)

        def zwait(e, c):
            pltpu.make_async_copy(z_ref, xs_hbm.at[pl.ds(0, tile * rs)], sem).wait()
            return c

        lax.fori_loop(0, N_EXPERTS, zwait, 0)

    def body(n, c):
        src = x_ref.at[pl.ds(pl.multiple_of(n * rs, rs), rs)]
        for j in range(TOP_K):
            dst = xs_hbm.at[pl.ds(pl.multiple_of(d_ref[n * TOP_K + j], rs), rs)]
            pltpu.make_async_copy(src, dst, sem).start(priority=j % 2)
        return c

    q4 = tm // 4
    lax.fori_loop(0, q4, body, 0)
    lo, hi = _load_rows(x_ref)
    d2 = lo.shape[1]
    gate = _dot(lo, sg_ref[0:d2, :]) + _dot(hi, sg_ref[d2:, :])
    lax.fori_loop(q4, 2 * q4, body, 0)
    up = _dot(lo, su_ref[0:d2, :]) + _dot(hi, su_ref[d2:, :])
    mid = (gate * jax.nn.sigmoid(gate) * up).astype(BF16)
    lax.fori_loop(2 * q4, 3 * q4, body, 0)
    hs_ref[...] = h_ref[...] + _dot(mid, sd_ref[...])
    lax.fori_loop(3 * q4, tm, body, 0)

    for j in range(TOP_K):
        pltpu.make_async_copy(x_ref, xs_hbm.at[pl.ds(0, tm * rs)], sem).wait()


def _scatter_rows(zero_tile, dest, xp, h, sg, su, sd, rows, tile, tm):
    m, d = h.shape
    tm = min(tm, m)
    full = lambda a: pl.BlockSpec(a.shape, lambda i, zt: (0,) * a.ndim)
    return pl.pallas_call(
        functools.partial(_scatter_kernel, tm=tm, tile=tile),
        grid_spec=pltpu.PrefetchScalarGridSpec(
            num_scalar_prefetch=1,
            grid=(m // tm,),
            in_specs=[pl.BlockSpec((tm * TOP_K,), lambda i, zt: (i,), memory_space=pltpu.SMEM),
                      pl.BlockSpec((tm * ROW_SUB, 128), lambda i, zt: (i, 0)),
                      pl.BlockSpec((tile * ROW_SUB, 128), lambda i, zt: (0, 0)),
                      pl.BlockSpec((tm, d), lambda i, zt: (i, 0)),
                      full(sg), full(su), full(sd)],
            out_specs=[pl.BlockSpec(memory_space=pl.ANY), pl.BlockSpec((tm, d), lambda i, zt: (i, 0))],
            scratch_shapes=[pltpu.SemaphoreType.DMA(())],
        ),
        out_shape=[jax.ShapeDtypeStruct((rows * ROW_SUB, 128), jnp.uint32),
                   jax.ShapeDtypeStruct((m, d), F32)],
        compiler_params=_cparams(1),
        name="moe_scatter_rows_shared",
    )(zero_tile, dest, xp, jnp.zeros((tile * ROW_SUB, 128), jnp.uint32), h, sg, su, sd)


def _expert_kernel(te_ref, nx_ref, gi_ref, nu_ref, x_ref, wg_hbm, wu_hbm, wd_hbm, o_ref,
                   wgf, wuf, wdf, wgb, wub, wdb, sem):
    t = pl.program_id(0)
    tc = jnp.minimum(t, nu_ref[0] - 1)
    prev = jnp.maximum(tc - 1, 0)
    new_expert = jnp.logical_or(t == 0, te_ref[tc] != te_ref[prev])
    slot = jnp.bitwise_and(gi_ref[tc], 1)

    def copies(e, s):
        return (pltpu.make_async_copy(wg_hbm.at[e], wgf.at[s], sem.at[s]),
                pltpu.make_async_copy(wu_hbm.at[e], wuf.at[s], sem.at[s]),
                pltpu.make_async_copy(wd_hbm.at[e], wdf.at[s], sem.at[s]))

    @pl.when(t == 0)
    def _():
        for c in copies(te_ref[0], 0):
            c.start()

    @pl.when(jnp.logical_and(new_expert, t < nu_ref[0]))
    def _():
        for c in copies(te_ref[tc], slot):
            c.wait()
        wgb[...] = wgf[slot].astype(BF16)
        wub[...] = wuf[slot].astype(BF16)
        wdb[...] = wdf[slot].astype(BF16)

        @pl.when(nx_ref[tc] >= 0)
        def _():
            for c in copies(nx_ref[tc], 1 - slot):
                c.start()

    @pl.when(t < nu_ref[0])
    def _():
        lo, hi = _load_rows(x_ref)
        d2 = lo.shape[1]
        gate = _dot(lo, wgb[0:d2, :]) + _dot(hi, wgb[d2:, :])
        up = _dot(lo, wub[0:d2, :]) + _dot(hi, wub[d2:, :])
        mid = (gate * jax.nn.sigmoid(gate) * up).astype(BF16)
        y = _dot(mid, wdb[...])
        _store_rows(o_ref, _pack_pair(y[:, :d2], y[:, d2:]))


def _experts(tile_expert, next_expert, group_idx, n_used, xs, wg, wu, wd, tile):
    rows = xs.shape[0] // ROW_SUB
    d = wg.shape[1]
    blk = (tile * ROW_SUB, 128)
    tile_map = lambda t, te, nx, gi, nu: (jnp.minimum(t, nu[0] - 1), 0)
    hbm = pl.BlockSpec(memory_space=pl.ANY)
    return pl.pallas_call(
        _expert_kernel,
        grid_spec=pltpu.PrefetchScalarGridSpec(
            num_scalar_prefetch=4,
            grid=(rows // tile,),
            in_specs=[pl.BlockSpec(blk, tile_map), hbm, hbm, hbm],
            out_specs=pl.BlockSpec(blk, tile_map),
            scratch_shapes=[pltpu.VMEM((2, d, EXPERT_DIM), F32), pltpu.VMEM((2, d, EXPERT_DIM), F32),
                            pltpu.VMEM((2, EXPERT_DIM, d), F32),
                            pltpu.VMEM((d, EXPERT_DIM), BF16), pltpu.VMEM((d, EXPERT_DIM), BF16),
                            pltpu.VMEM((EXPERT_DIM, d), BF16), pltpu.SemaphoreType.DMA((2,))],
        ),
        out_shape=jax.ShapeDtypeStruct(xs.shape, jnp.uint32),
        compiler_params=_cparams(1),
        name="moe_experts",
    )(tile_expert, next_expert, group_idx, n_used, xs, wg, wu, wd)


def _combine_kernel(d_ref, dn_ref, ys_hbm, w_ref, h_ref, o_ref, buf, sem, *, tm, nt):
    i = pl.program_id(0)
    slot = jnp.bitwise_and(i, 1)
    nxt = 1 - slot
    rs = ROW_SUB

    def start(idx_ref, n, s):
        for j in range(TOP_K):
            src = ys_hbm.at[pl.ds(pl.multiple_of(idx_ref[n * TOP_K + j], rs), rs)]
            pltpu.make_async_copy(src, buf.at[s, j, pl.ds(n * rs, rs)], sem.at[s]).start(priority=j % 2)

    def wait_all(s):
        for j in range(TOP_K):
            pltpu.make_async_copy(ys_hbm.at[pl.ds(0, tm * rs)], buf.at[s, j], sem.at[s]).wait()

    @pl.when(i == 0)
    def _():
        def body(n, c):
            for j in range(TOP_K):
                src = ys_hbm.at[pl.ds(pl.multiple_of(d_ref[n * TOP_K + j], rs), rs)]
                dst = buf.at[0, j, pl.ds(pl.multiple_of(n * rs, rs), rs)]
                pltpu.make_async_copy(src, dst, sem.at[0]).start(priority=j % 2)
            return c

        lax.fori_loop(0, tm, body, 0)

    per = tm // (rs * TOP_K)
    batches = iter(range(0, tm, per))

    def start_batch():
        n0 = next(batches)
        for n in range(n0, n0 + per):
            start(dn_ref, n, nxt)

    d2 = h_ref.shape[1] // 2
    o_ref[...] = h_ref[...]
    wait_all(slot)
    wts = w_ref[...]
    wj = [wts[:, j:j + 1] for j in range(TOP_K)]
    for s in range(rs):
        r_lo = jnp.zeros((tm, 128), F32)
        r_hi = jnp.zeros((tm, 128), F32)
        for j in range(TOP_K):
            start_batch()
            ylo, yhi = _unpack_pair(buf[slot, j, pl.ds(s, tm, stride=rs), :])
            r_lo = r_lo + wj[j] * ylo
            r_hi = r_hi + wj[j] * yhi
        o_ref[:, s * 128:(s + 1) * 128] += r_lo
        o_ref[:, d2 + s * 128:d2 + (s + 1) * 128] += r_hi

    @pl.when(i == nt - 1)
    def _():
        wait_all(nxt)


def _combine(dest, ys, wcol, h, tm):
    m, d = h.shape
    tm = min(tm, m)
    nt = m // tm
    idx = lambda f: pl.BlockSpec((tm * TOP_K,), f, memory_space=pltpu.SMEM)
    return pl.pallas_call(
        functools.partial(_combine_kernel, tm=tm, nt=nt),
        grid=(nt,),
        in_specs=[idx(lambda i: (i,)), idx(lambda i: (jnp.minimum(i + 1, nt - 1),)),
                  pl.BlockSpec(memory_space=pl.ANY),
                  pl.BlockSpec((tm, N_EXPERTS), lambda i: (i, 0)),
                  pl.BlockSpec((tm, d), lambda i: (i, 0))],
        out_specs=pl.BlockSpec((tm, d), lambda i: (i, 0)),
        out_shape=jax.ShapeDtypeStruct((m, d), F32),
        scratch_shapes=[pltpu.VMEM((2, TOP_K, tm * ROW_SUB, 128), jnp.uint32),
                        pltpu.SemaphoreType.DMA((2,))],
        compiler_params=_cparams(1),
        name="moe_combine",
    )(dest, dest, ys, wcol, h)


def _moe(h1, hnp, lg_t, router_bias, w_gate, w_up, w_down, ws_gate, ws_up, ws_down):
    n = h1.shape[0]
    tile = MOE_TILE
    eidx, rank, wcol, cnt = _router(lg_t, router_bias.reshape(-1, 1), TM_ROUTER)
    counts = cnt[:, 0].astype(jnp.int32)
    tiles_per = (counts + tile - 1) // tile
    tile_end = jnp.cumsum(tiles_per)
    tile_start = tile_end - tiles_per
    n_tiles = (n * TOP_K) // tile + N_EXPERTS
    rows = n_tiles * tile
    tile_expert = jnp.minimum(jnp.sum(tile_end[None, :] <= jnp.arange(n_tiles)[:, None], axis=1),
                              N_EXPERTS - 1).astype(jnp.int32)
    n_used = tile_end[-1:].astype(jnp.int32)
    ids = jnp.arange(N_EXPERTS)
    later = jnp.logical_and(ids[None, :] > ids[:, None], tiles_per[None, :] > 0)
    nxt_of = jnp.min(jnp.where(later, ids[None, :], N_EXPERTS), axis=1)
    nxt_of = jnp.where(nxt_of == N_EXPERTS, -1, nxt_of)
    ord_of = jnp.cumsum((tiles_per > 0).astype(jnp.int32)) - 1
    onehot = tile_expert[:, None] == ids[None, :]
    next_expert = jnp.sum(jnp.where(onehot, nxt_of[None, :], 0), axis=1).astype(jnp.int32)
    group_idx = jnp.sum(jnp.where(onehot, ord_of[None, :], 0), axis=1).astype(jnp.int32)
    last_tile = jnp.clip(tile_end - 1, 0, n_tiles - 1).astype(jnp.int32)
    dest = _dest((tile_start * tile).astype(jnp.int32), eidx, rank, TM_DEST)
    dest = dest.T.reshape(-1)
    xs, hs = _scatter_rows(last_tile, dest, hnp, h1, ws_gate.astype(BF16), ws_up.astype(BF16),
                           ws_down.astype(BF16), rows, tile, TM_SCATTER)
    ys = _experts(tile_expert, next_expert, group_idx, n_used, xs, w_gate, w_up, w_down, tile)
    return _combine(dest, ys, wcol, hs, TM_COMBINE)


def _ple_kernel(h_ref, p_ref, gin_ref, wg_ref, bg_ref, wp_ref, gout_ref, o_ref):
    h = h_ref[...]
    gate = jax.nn.sigmoid(_dot(_rms(h, gin_ref[...]).astype(BF16), wg_ref[...]) + bg_ref[...])
    pp = _dot(p_ref[...].astype(BF16), wp_ref[...])
    o_ref[...] = h + _rms(pp * gate, gout_ref[...])


def _ple(h, p, g_in, w_g, b_g, w_p, g_out, tm):
    m, d = h.shape
    tm = min(tm, m)
    full = lambda a: pl.BlockSpec(a.shape, lambda i: (0,) * a.ndim)
    return pl.pallas_call(
        _ple_kernel,
        grid=(m // tm,),
        in_specs=[pl.BlockSpec((tm, d), lambda i: (i, 0)),
                  pl.BlockSpec((tm, p.shape[1]), lambda i: (i, 0)),
                  full(g_in), full(w_g), full(b_g), full(w_p), full(g_out)],
        out_specs=pl.BlockSpec((tm, d), lambda i: (i, 0)),
        out_shape=jax.ShapeDtypeStruct((m, d), F32),
        compiler_params=_cparams(1),
        name="ple",
    )(h, p, g_in, w_g, b_g, w_p, g_out)


def _pad_rows(a, rows):
    return jnp.pad(a, ((0, rows - a.shape[0]), (0, 0)))


def _row(a):
    return a.reshape(1, -1)


def _proj_layout(a):
    def padded(x, width):
        return jnp.pad(x, [(0, 0)] * (x.ndim - 1) + [(0, width - x.shape[-1])])

    parts = [padded(a[..., :MLA_COLS], MLA_COLS_PAD), a[..., MLA_COLS:MLA_COLS + 3 * RW_C]]
    off = MLA_COLS + 3 * RW_C
    for width, wide in ((RW_DECAY_LORA, 128), (RW_A_LORA, 128), (RW_GATE_LORA, 256)):
        parts.append(padded(a[..., off:off + width], wide))
        off += width
    return jnp.concatenate(parts, axis=-1)


def _mla_from_proj(proj, positions, g_qa, w_uq, g_kva, w_ukv, g_qn, g_kn):
    b, t, _ = proj.shape
    wq = w_uq.reshape(MLA_Q_LORA, MLA_HEADS, MLA_QK)
    wq = jnp.concatenate([wq[:, :, :MLA_NOPE].reshape(MLA_Q_LORA, -1),
                          wq[:, :, MLA_NOPE:].reshape(MLA_Q_LORA, -1)], axis=1).astype(BF16)
    half = MLA_ROPE // 2
    inv = ROPE_THETA ** (-jnp.arange(half, dtype=F32) / half)
    invf = _row(jnp.concatenate([inv, inv]))
    sgn = _row(jnp.concatenate([-jnp.ones(half, F32), jnp.ones(half, F32)]))
    q, k, v = _mla_prep(proj, positions.reshape(b, t, 1), invf, sgn, _row(g_qa), wq, _row(g_kva),
                        w_ukv.astype(BF16), _row(g_qn), _row(g_kn), TM_MLA_PREP)
    return _attention(q, k, v, ATTN_TQ, ATTN_TK, ATTN_HEADS_PER_STEP)


def _rwkv_from_proj(proj, mu, w0, w_w2, a0, a_w2, g_w2, k_k, k_a, r_k, ln_w, ln_b):
    head_of = jnp.arange(RW_C) // RW_HEAD
    bd = (head_of[:, None] == head_of[None, :]).astype(BF16)
    mu_all = jnp.concatenate([jnp.zeros((MLA_COLS,), F32), mu])
    rr, ww, kx, vx, kkx, bx, gx = _rw_prep(
        proj, _row(_proj_layout(mu_all)), _row(w0), _pad_rows(w_w2, 128).astype(BF16), _row(a0),
        _pad_rows(a_w2, 128).astype(BF16), _pad_rows(g_w2, 256).astype(BF16), _row(k_k), _row(k_a),
        bd, TM_RW_PREP)
    return _rw_chunk(rr, ww, kx, vx, kkx, bx, gx, _row(r_k), _row(ln_w), _row(ln_b))


def _layer(h, p, positions, g_mix, w_in, mla_g_qa, mla_w_uq, mla_g_kva, mla_w_ukv, mla_g_qn,
           mla_g_kn, rw_mu, rw_w0, rw_w_w2, rw_a0, rw_a_w2, rw_g_w2, rw_k_k, rw_k_a, rw_r_k,
           rw_ln_w, rw_ln_b, w_out, g_ffn, w_router, router_bias, w_gate, w_up, w_down, ws_gate,
           ws_up, ws_down, g_ple_in, w_ple_gate, b_ple_gate, w_ple_proj, g_ple_out):
    b, t, d = h.shape
    n = b * t
    x2 = h.reshape(n, d)

    proj = _in_proj(x2, _row(g_mix), _proj_layout(w_in).astype(BF16), TM_IN_PROJ).reshape(b, t, -1)
    y_mla = _mla_from_proj(proj, positions, mla_g_qa, mla_w_uq, mla_g_kva, mla_w_ukv, mla_g_qn, mla_g_kn)
    y_rw = _rwkv_from_proj(proj, rw_mu, rw_w0, rw_w_w2, rw_a0, rw_a_w2, rw_g_w2, rw_k_k, rw_k_a, rw_r_k,
                           rw_ln_w, rw_ln_b)

    h1, hnp, lg_t = _out_proj(y_mla.reshape(n, -1), y_rw.reshape(n, -1), x2, w_out.astype(BF16),
                              _row(g_ffn), w_router, TM_OUT_PROJ)
    h2 = _moe(h1, hnp, lg_t, router_bias, w_gate, w_up, w_down, ws_gate, ws_up, ws_down)

    h3 = _ple(h2, p.reshape(n, -1), _row(g_ple_in), w_ple_gate.astype(BF16), _row(b_ple_gate),
              w_ple_proj.astype(BF16), _row(g_ple_out), TM_PLE)
    return h3.reshape(b, t, d)


def kernel(x, p, positions, g_mix, w_in, mla_g_qa, mla_w_uq, mla_g_kva, mla_w_ukv, mla_g_qn, mla_g_kn, rw_mu, rw_w0, rw_w_w2, rw_a0, rw_a_w2, rw_g_w2, rw_k_k, rw_k_a, rw_r_k, rw_ln_w, rw_ln_b, w_out, g_ffn, w_router, router_bias, w_gate, w_up, w_down, ws_gate, ws_up, ws_down, g_ple_in, w_ple_gate, b_ple_gate, w_ple_proj, g_ple_out):
    params = (g_mix, w_in, mla_g_qa, mla_w_uq, mla_g_kva, mla_w_ukv, mla_g_qn, mla_g_kn, rw_mu,
              rw_w0, rw_w_w2, rw_a0, rw_a_w2, rw_g_w2, rw_k_k, rw_k_a, rw_r_k, rw_ln_w, rw_ln_b,
              w_out, g_ffn, w_router, router_bias, w_gate, w_up, w_down, ws_gate, ws_up, ws_down,
              g_ple_in, w_ple_gate, b_ple_gate, w_ple_proj, g_ple_out)
    h = x
    for i in range(g_mix.shape[0]):
        h = _layer(h, p[i], positions, *[a[i] for a in params])
    return h
```

```python
import functools

import jax
import jax.numpy as jnp
from jax import lax
from jax.experimental import pallas as pl
from jax.experimental.pallas import tpu as pltpu

F32 = jnp.float32
BF16 = jnp.bfloat16

D_MODEL = 2048
PLE_DIM = 256
RMS_EPS = 1e-6

MLA_HEADS = 8
MLA_NOPE = 128
MLA_ROPE = 64
MLA_QK = MLA_NOPE + MLA_ROPE
MLA_V = 128
MLA_Q_LORA = 512
MLA_KV_LORA = 256
ROPE_THETA = 10000.0
MLA_COLS = MLA_Q_LORA + MLA_KV_LORA + MLA_ROPE
MLA_COLS_PAD = 1024

RW_HEADS = 16
RW_HEAD = 64
RW_C = RW_HEADS * RW_HEAD
RW_DECAY_LORA = 64
RW_A_LORA = 64
RW_GATE_LORA = 160
RW_GN_EPS = 64e-5
RW_LORA_PAD = 512
PROJ_COLS = MLA_COLS_PAD + 3 * RW_C + RW_LORA_PAD
PROJ_TN = 1536
RW_CHUNK = 64

N_EXPERTS = 64
TOP_K = 8
N_GROUPS = 8
TOPK_GROUPS = 4
EXPERT_DIM = 512
ROUTED_SCALE = 2.5
MOE_TILE = 512
ROW_SUB = D_MODEL // 2 // 128

VMEM_LIMIT = 56 * 1024 * 1024

TM_IN_PROJ = 1024
TM_MLA_PREP = 512
ATTN_TQ = 1024
ATTN_TK = 1024
ATTN_HEADS_PER_STEP = 2
TM_RW_PREP = 256
TM_OUT_PROJ = 512
TM_ROUTER = 512
TM_DEST = 2048
TM_SCATTER = 512
TM_COMBINE = 256
TM_PLE = 512


def _cparams(n_axes):
    return pltpu.CompilerParams(dimension_semantics=("arbitrary",) * n_axes,
                                vmem_limit_bytes=VMEM_LIMIT)


def _rms(x, g):
    ms = jnp.mean(x * x, axis=-1, keepdims=True)
    return x * lax.rsqrt(ms + RMS_EPS) * g


def _dot(a, b):
    return jnp.dot(a, b, preferred_element_type=F32)


def _dot_nt(a, b):
    return lax.dot_general(a, b, (((1,), (1,)), ((), ())), preferred_element_type=F32)


def _dot_tn(a, b):
    return lax.dot_general(a, b, (((0,), (0,)), ((), ())), preferred_element_type=F32)


def _split2(x):
    hi = x.astype(BF16)
    lo = (x - hi.astype(F32)).astype(BF16)
    return hi, lo


def _in_proj_kernel(x_ref, g_ref, w_ref, o_ref, xn_ref):
    @pl.when(pl.program_id(1) == 0)
    def _():
        xn_ref[...] = _rms(x_ref[...], g_ref[...]).astype(BF16)

    o_ref[...] = _dot(xn_ref[...], w_ref[...]).astype(o_ref.dtype)


def _in_proj(x, g, w, tm):
    m, k = x.shape
    n = w.shape[1]
    tn = PROJ_TN
    tm = min(tm, m)
    return pl.pallas_call(
        _in_proj_kernel,
        grid=(m // tm, n // tn),
        in_specs=[pl.BlockSpec((tm, k), lambda i, j: (i, 0)),
                  pl.BlockSpec((1, k), lambda i, j: (0, 0)),
                  pl.BlockSpec((k, tn), lambda i, j: (0, j))],
        out_specs=pl.BlockSpec((tm, tn), lambda i, j: (i, j)),
        out_shape=jax.ShapeDtypeStruct((m, n), BF16),
        scratch_shapes=[pltpu.VMEM((tm, k), BF16)],
        compiler_params=_cparams(2),
        name="in_proj",
    )(x, g, w)


def _mla_prep_kernel(pm_ref, pos_ref, invf_ref, sgn_ref, gqa_ref, wuq_ref, gkva_ref, wukv_ref,
                     gqn_ref, gkn_ref, q_ref, k_ref, v_ref):
    pm = pm_ref[0].astype(F32)
    ang = pos_ref[0].astype(F32) * invf_ref[...]
    cos = jnp.cos(ang)
    sin = jnp.sin(ang) * sgn_ref[...]

    def rope(xr):
        half = MLA_ROPE // 2
        swapped = jnp.concatenate([xr[:, half:], xr[:, :half]], axis=1)
        return xr * cos + swapped * sin

    scale = MLA_QK ** -0.5 * 1.4426950408889634
    gqn = gqn_ref[...]
    gkn = gkn_ref[...]

    cq = _rms(pm[:, :MLA_Q_LORA], gqa_ref[...]).astype(BF16)
    q = _dot(cq, wuq_ref[...])
    ckv = _rms(pm[:, MLA_Q_LORA:MLA_Q_LORA + MLA_KV_LORA], gkva_ref[...]).astype(BF16)
    kv = _dot(ckv, wukv_ref[...])
    kpe = pm[:, MLA_Q_LORA + MLA_KV_LORA:MLA_COLS]
    kpe_ssq = jnp.sum(kpe * kpe, axis=-1, keepdims=True)
    kpe_rot = rope(kpe * gkn[:, MLA_NOPE:])

    for h in range(MLA_HEADS):
        qn = q[:, h * MLA_NOPE:(h + 1) * MLA_NOPE]
        qr = q[:, MLA_HEADS * MLA_NOPE + h * MLA_ROPE:MLA_HEADS * MLA_NOPE + (h + 1) * MLA_ROPE]
        ssq = jnp.sum(qn * qn, axis=-1, keepdims=True) + jnp.sum(qr * qr, axis=-1, keepdims=True)
        inv = lax.rsqrt(ssq * (1.0 / MLA_QK) + RMS_EPS) * scale
        q_ref[0, h, :, 0:MLA_NOPE] = (qn * inv * gqn[:, :MLA_NOPE]).astype(BF16)
        q_ref[0, h, :, MLA_NOPE:MLA_QK] = (rope(qr * gqn[:, MLA_NOPE:]) * inv).astype(BF16)

        kn = kv[:, h * 256:h * 256 + MLA_NOPE]
        ssq = jnp.sum(kn * kn, axis=-1, keepdims=True) + kpe_ssq
        inv = lax.rsqrt(ssq * (1.0 / MLA_QK) + RMS_EPS)
        k_ref[0, h, :, 0:MLA_NOPE] = (kn * inv * gkn[:, :MLA_NOPE]).astype(BF16)
        k_ref[0, h, :, MLA_NOPE:MLA_QK] = (kpe_rot * inv).astype(BF16)
        v_ref[0, h, :, 0:MLA_V] = kv[:, h * 256 + MLA_NOPE:(h + 1) * 256].astype(BF16)
        v_ref[0, h, :, MLA_V:] = jnp.ones((kv.shape[0], MLA_V), BF16)


def _mla_prep(pm, pos, invf, sgn, g_qa, w_uq, g_kva, w_ukv, g_qn, g_kn, tm):
    b, t, _ = pm.shape
    tm = min(tm, t)
    full = lambda a: pl.BlockSpec(a.shape, lambda i, j: (0,) * a.ndim)
    qk_shape = jax.ShapeDtypeStruct((b, MLA_HEADS, t, MLA_QK), BF16)
    return pl.pallas_call(
        _mla_prep_kernel,
        grid=(b, t // tm),
        in_specs=[pl.BlockSpec((1, tm, MLA_COLS_PAD), lambda i, j: (i, j, 0)),
                  pl.BlockSpec((1, tm, 1), lambda i, j: (i, j, 0)),
                  full(invf), full(sgn), full(g_qa), full(w_uq), full(g_kva), full(w_ukv),
                  full(g_qn), full(g_kn)],
        out_specs=[pl.BlockSpec((1, MLA_HEADS, tm, MLA_QK), lambda i, j: (i, 0, j, 0)),
                   pl.BlockSpec((1, MLA_HEADS, tm, MLA_QK), lambda i, j: (i, 0, j, 0)),
                   pl.BlockSpec((1, MLA_HEADS, tm, 2 * MLA_V), lambda i, j: (i, 0, j, 0))],
        out_shape=[qk_shape, qk_shape, jax.ShapeDtypeStruct((b, MLA_HEADS, t, 2 * MLA_V), BF16)],
        compiler_params=_cparams(2),
        name="mla_prep",
    )(pm, pos, invf, sgn, g_qa, w_uq, g_kva, w_ukv, g_qn, g_kn)


def _attn_kernel(q_ref, k_ref, v_ref, o_ref, m_ref, acc_ref, *, tq, tk, nh):
    qi = pl.program_id(2)
    m_ref[...] = jnp.full(m_ref.shape, -jnp.inf, F32)
    acc_ref[...] = jnp.zeros(acc_ref.shape, F32)
    heads = range(nh)
    def scores(j, r0, nr):
        start = pl.multiple_of(j * tk, tk)
        return [_dot_nt(q_ref[0, h, r0:r0 + nr, :], k_ref[0, h, pl.ds(start, tk), :]) for h in heads]

    def update(j, r0, nr, ss, triangle):
        start = pl.multiple_of(j * tk, tk)
        rows = slice(r0, r0 + nr)
        for h in heads:
            s = ss[h]
            if triangle:
                keep = (lax.broadcasted_iota(jnp.int32, (nr, tk), 1)
                        <= lax.broadcasted_iota(jnp.int32, (nr, tk), 0))
                s = jnp.where(keep, s, -jnp.inf)
            m_old = m_ref[h, rows, :]
            m_new = jnp.maximum(m_old, jnp.max(s, axis=-1, keepdims=True))
            alpha = jnp.exp2(m_old - m_new)
            p = jnp.exp2(s - jnp.concatenate([m_new] * (tk // 128), axis=1))
            pv = _dot(p.astype(BF16), v_ref[0, h, pl.ds(start, tk), :])
            acc_ref[h, rows, :] = jnp.concatenate([alpha, alpha], axis=1) * acc_ref[h, rows, :] + pv
            m_ref[h, rows, :] = m_new

    per = tq // tk
    n_full = qi * per

    def body(jj, c):
        ss = [scores(jj * per + d, 0, tq) for d in range(per)]
        for d in range(per):
            update(jj * per + d, 0, tq, ss[d], False)
        return c

    lax.fori_loop(0, qi, body, 0)
    blocks = []
    for d in range(per):
        blocks.append((n_full + d, d * tk, tk, True))
        if d + 1 < per:
            blocks.append((n_full + d, (d + 1) * tk, tq - (d + 1) * tk, False))
    ss = [scores(j, r0, nr) for j, r0, nr, _ in blocks]
    for (j, r0, nr, tri), s in zip(blocks, ss):
        update(j, r0, nr, s, tri)
    for h in heads:
        acc = acc_ref[h]
        o_ref[0, :, h * MLA_V:(h + 1) * MLA_V] = (acc[:, :MLA_V] / acc[:, MLA_V:]).astype(o_ref.dtype)


def _attention(q, k, v, tq, tk, nh):
    b, h, t, _ = q.shape
    tq = min(tq, t)
    tk = min(tk, tq)
    return pl.pallas_call(
        functools.partial(_attn_kernel, tq=tq, tk=tk, nh=nh),
        grid=(b, h // nh, t // tq),
        in_specs=[pl.BlockSpec((1, nh, tq, MLA_QK), lambda i, j, n: (i, j, n, 0)),
                  pl.BlockSpec((1, nh, t, MLA_QK), lambda i, j, n: (i, j, 0, 0)),
                  pl.BlockSpec((1, nh, t, 2 * MLA_V), lambda i, j, n: (i, j, 0, 0))],
        out_specs=pl.BlockSpec((1, tq, nh * MLA_V), lambda i, j, n: (i, n, j)),
        out_shape=jax.ShapeDtypeStruct((b, t, h * MLA_V), BF16),
        scratch_shapes=[pltpu.VMEM((nh, tq, 128), F32), pltpu.VMEM((nh, tq, 2 * MLA_V), F32)],
        compiler_params=_cparams(3),
        name="mla_attention",
    )(q, k, v)


def _rw_prep_kernel(r_ref, k_ref, v_ref, l_ref, rp_ref, kp_ref, vp_ref, lp_ref, mu_ref, w0_ref,
                    ww2_ref, a0_ref, aw2_ref, gw2_ref, kk_ref, ka_ref, bd_ref,
                    r_o, w_o, k_o, v_o, kk_o, b_o, g_o):
    first = pl.program_id(1) == 0
    c = RW_C

    def shifted(ref, pref, mu):
        z = ref[0].astype(F32)
        prev = pref[0][15:16, :].astype(F32)
        prev = jnp.where(first, jnp.zeros_like(prev), prev)
        row = lax.broadcasted_iota(jnp.int32, z.shape, 0)
        zs = jnp.where(row == 0, prev, pltpu.roll(z, 1, axis=0))
        return z + (zs - z) * mu

    o = MLA_COLS_PAD
    r = shifted(r_ref, rp_ref, mu_ref[:, o:o + c])
    k = shifted(k_ref, kp_ref, mu_ref[:, o + c:o + 2 * c])
    v = shifted(v_ref, vp_ref, mu_ref[:, o + 2 * c:o + 3 * c])
    lz = shifted(l_ref, lp_ref, mu_ref[:, o + 3 * c:])
    wl = lz[:, 0:128]
    al = lz[:, 128:256]
    gl = lz[:, 256:512]

    f = w0_ref[...] + _dot(jnp.tanh(wl).astype(BF16), ww2_ref[...])
    softplus_neg_f = jnp.maximum(-f, 0.0) + jnp.log(1.0 + jnp.exp(-jnp.abs(f)))
    w = -softplus_neg_f - 0.5
    w_o[0] = -jnp.exp(w)
    a = jax.nn.sigmoid(a0_ref[...] + _dot(al.astype(BF16), aw2_ref[...]))
    g_o[0] = _dot(jax.nn.sigmoid(gl).astype(BF16), gw2_ref[...]).astype(g_o.dtype)

    kk = k * kk_ref[...]
    hi, lo = _split2(kk * kk)
    ssq = _dot(hi, bd_ref[...]) + _dot(lo, bd_ref[...])
    kk = kk * lax.rsqrt(jnp.maximum(ssq, 1e-24))
    r_o[0] = r.astype(r_o.dtype)
    k_o[0] = (k * (1.0 + (a - 1.0) * ka_ref[...])).astype(k_o.dtype)
    v_o[0] = v.astype(v_o.dtype)
    kk_o[0] = kk.astype(kk_o.dtype)
    b_o[0] = (kk * a).astype(b_o.dtype)


def _rw_prep(proj, mu, w0, ww2, a0, aw2, gw2, k_k, k_a, bd, tm):
    b, t, _ = proj.shape
    tm = min(tm, t)
    c = RW_C
    full = lambda a: pl.BlockSpec(a.shape, lambda i, j: (0,) * a.ndim)
    cur = lambda w, cb: pl.BlockSpec((1, tm, w), lambda i, j: (i, j, cb))
    prev = lambda w, cb: pl.BlockSpec((1, 16, w), lambda i, j: (i, jnp.maximum(j * (tm // 16) - 1, 0), cb))
    lora_cb = (MLA_COLS_PAD + 3 * c) // RW_LORA_PAD
    groups = [(c, 1), (c, 2), (c, 3), (RW_LORA_PAD, lora_cb)]
    out = [jax.ShapeDtypeStruct((b, t, c), F32 if i == 1 else BF16) for i in range(7)]
    ospec = pl.BlockSpec((1, tm, c), lambda i, j: (i, j, 0))
    return pl.pallas_call(
        _rw_prep_kernel,
        grid=(b, t // tm),
        in_specs=[cur(*g) for g in groups] + [prev(*g) for g in groups]
        + [full(mu), full(w0), full(ww2), full(a0), full(aw2), full(gw2), full(k_k), full(k_a), full(bd)],
        out_specs=[ospec] * 7,
        out_shape=out,
        compiler_params=_cparams(2),
        name="rwkv_prep",
    )(*([proj] * 8), mu, w0, ww2, a0, aw2, gw2, k_k, k_a, bd)


def _rw_chunk_kernel(r_ref, w_ref, k_ref, v_ref, kk_ref, b_ref, g_ref, rk_ref, lnw_ref, lnb_ref,
                     o_ref, s_ref):
    cs = RW_CHUNK
    hd = RW_HEAD

    @pl.when(pl.program_id(1) == 0)
    def _():
        s_ref[...] = jnp.zeros(s_ref.shape, F32)

    r = r_ref[0].astype(F32)
    w = w_ref[0]
    k = k_ref[0].astype(F32)
    v = v_ref[0].astype(F32)
    kk = kk_ref[0].astype(F32)
    b = b_ref[0].astype(F32)
    pairs = range(RW_HEADS // 2)
    pw = 2 * hd

    row = lax.broadcasted_iota(jnp.int32, (cs, pw), 0)
    col = lax.broadcasted_iota(jnp.int32, (cs, pw), 1)
    colh = jnp.bitwise_and(col, hd - 1)
    lo = col < hd
    strict = row > colh
    eye = jnp.where(row == colh, 1.0, 0.0).astype(F32)
    row2 = lax.broadcasted_iota(jnp.int32, (cs, 2 * pw), 0)
    col2 = lax.broadcasted_iota(jnp.int32, (cs, 2 * pw), 1)
    incl2 = row2 >= jnp.bitwise_and(col2, hd - 1)
    row3 = lax.broadcasted_iota(jnp.int32, (cs, 3 * cs), 0)
    col3 = lax.broadcasted_iota(jnp.int32, (cs, 3 * cs), 1)
    tri3 = jnp.where(row3 >= jnp.bitwise_and(col3, cs - 1), 1.0, 0.0).astype(BF16)

    def bdiag(x):
        zero = jnp.zeros_like(x)
        return jnp.concatenate([jnp.where(lo, x, zero), jnp.where(lo, zero, x)], axis=0)

    def cat3_lhs(x):
        hi = x.astype(BF16)
        return jnp.concatenate([hi, (x - hi.astype(F32)).astype(BF16), hi], axis=1)

    def cat3_rhs(x):
        hi, lw = _split2(x)
        bh = bdiag(hi)
        return jnp.concatenate([bh, bh, bdiag(lw)], axis=0)

    w1 = w.astype(BF16)
    w2 = (w - w1.astype(F32)).astype(BF16)
    w3 = (w - w1.astype(F32) - w2.astype(F32)).astype(BF16)
    logp = _dot(tri3, jnp.concatenate([w1, w2, w3], axis=0))
    logp_end = logp[cs - 1:cs, :]
    p_tail = jnp.exp(logp_end - logp)
    p_end = jnp.exp(logp_end)
    p_inv = jnp.exp(-logp)

    a_t = -kk * jnp.exp(logp - w)
    r_t = r * jnp.exp(logp)
    b_t = b * p_inv
    k_t = k * p_inv
    b_h = b * p_tail
    k_h = k * p_tail

    sls = [slice(p * pw, (p + 1) * pw) for p in pairs]
    lhs = [jnp.concatenate([a_t[:, sl], r_t[:, sl]], axis=0).astype(BF16) for sl in sls]
    rhs = [jnp.concatenate([bdiag(b_t[:, sl].astype(BF16)), bdiag(k_t[:, sl].astype(BF16))], axis=0)
           for sl in sls]
    gm = [_dot_nt(lhs[p], rhs[p]) for p in pairs]
    s0 = [s_ref[p] for p in pairs]
    ls = [_dot_nt(lhs[p], bdiag(s0[p].astype(BF16))) for p in pairs]
    vb = [v[:, sl].astype(BF16) for sl in sls]
    bv = [bdiag(vb[p]) for p in pairs]
    a_ak = [jnp.where(strict, gm[p][:cs, pw:], 0.0).astype(BF16) for p in pairs]
    rhs_u = [ls[p][:cs] + _dot(a_ak[p], bv[p]) for p in pairs]

    a_ab = [jnp.where(strict, gm[p][:cs, :pw], 0.0) for p in pairs]
    rb = lambda m: jnp.right_shift(row, m)
    cb = lambda m: jnp.right_shift(colh, m)
    x = [jnp.where(rb(3) == cb(3), a_ab[p], 0.0) for p in pairs]
    tm = [eye + x[p] for p in pairs]
    x = [_dot(cat3_lhs(x[p]), cat3_rhs(x[p])) for p in pairs]
    both = [_dot(cat3_lhs(jnp.concatenate([x[p], tm[p]], axis=0)), cat3_rhs(x[p])) for p in pairs]
    x = [both[p][:cs] for p in pairs]
    tm = [tm[p] + both[p][cs:] for p in pairs]
    tm = [tm[p] + _dot(cat3_lhs(tm[p]), cat3_rhs(x[p])) for p in pairs]
    for m in (3, 4, 5):
        off = jnp.logical_and(rb(m + 1) == cb(m + 1),
                              jnp.logical_and(jnp.bitwise_and(rb(m), 1) == 1, jnp.bitwise_and(cb(m), 1) == 0))
        bt = [_dot(cat3_lhs(jnp.where(off, a_ab[p], 0.0)), cat3_rhs(tm[p])) for p in pairs]
        tm = [tm[p] + _dot(cat3_lhs(tm[p]), cat3_rhs(bt[p])) for p in pairs]

    u = [_dot(cat3_lhs(tm[p]), cat3_rhs(rhs_u[p])) for p in pairs]
    ub = [u[p].astype(BF16) for p in pairs]
    a_r = [jnp.where(incl2, gm[p][cs:, :], 0.0).astype(BF16) for p in pairs]
    y = [ls[p][cs:] + _dot(a_r[p], jnp.concatenate([bdiag(ub[p]), bv[p]], axis=0)) for p in pairs]
    for p in pairs:
        uv = jnp.concatenate([ub[p], vb[p]], axis=0)
        bk = jnp.concatenate([b_h[:, sls[p]], k_h[:, sls[p]]], axis=0).astype(BF16)
        full = _dot_tn(uv, bk)
        s_ref[p] = s0[p] * p_end[:, sls[p]] + jnp.where(lo, full[:hd], full[hd:])

    def head_sum(t):
        s_lo = jnp.sum(jnp.where(lo, t, 0.0), axis=-1, keepdims=True)
        s_all = jnp.sum(t, axis=-1, keepdims=True)
        return jnp.where(lo, s_lo, s_all - s_lo)

    rk_all = r * k * rk_ref[...]
    g = g_ref[0].astype(F32)
    for p in pairs:
        sl = sls[p]
        yc = y[p] - head_sum(y[p]) * (1.0 / hd)
        var = head_sum(yc * yc) * (1.0 / hd)
        yn = yc * lax.rsqrt(var + RW_GN_EPS) * lnw_ref[:, sl] + lnb_ref[:, sl]
        bonus = head_sum(rk_all[:, sl]) * v[:, sl]
        o_ref[0, :, sl] = ((yn + bonus) * g[:, sl]).astype(o_ref.dtype)


def _rw_chunk(r, w, k, v, kk, bb, g, r_k, ln_w, ln_b):
    b, t, c = r.shape
    cs = RW_CHUNK
    full = lambda a: pl.BlockSpec(a.shape, lambda i, j: (0,) * a.ndim)
    spec = pl.BlockSpec((1, cs, c), lambda i, j: (i, j, 0))
    return pl.pallas_call(
        _rw_chunk_kernel,
        grid=(b, t // cs),
        in_specs=[spec] * 7 + [full(r_k), full(ln_w), full(ln_b)],
        out_specs=spec,
        out_shape=jax.ShapeDtypeStruct((b, t, c), BF16),
        scratch_shapes=[pltpu.VMEM((RW_HEADS // 2, RW_HEAD, 2 * RW_HEAD), F32)],
        compiler_params=_cparams(2),
        name="rwkv_chunk",
    )(r, w, k, v, kk, bb, g, r_k, ln_w, ln_b)


def _pack_pair(a, b):
    ua = lax.bitcast_convert_type(a.astype(BF16).astype(F32), jnp.uint32)
    ub = lax.bitcast_convert_type(b.astype(BF16).astype(F32), jnp.uint32)
    return lax.shift_right_logical(ua, jnp.uint32(16)) | (ub & jnp.uint32(0xFFFF0000))


def _unpack_pair(u):
    lo = lax.bitcast_convert_type(lax.shift_left(u, jnp.uint32(16)), F32)
    hi = lax.bitcast_convert_type(u & jnp.uint32(0xFFFF0000), F32)
    return lo, hi


def _store_rows(ref, packed):
    m = packed.shape[0]
    for s in range(ROW_SUB):
        ref[pl.ds(s, m, stride=ROW_SUB), :] = packed[:, s * 128:(s + 1) * 128]


def _load_rows(ref):
    m = ref.shape[-2] // ROW_SUB
    los, his = [], []
    for s in range(ROW_SUB):
        lo, hi = _unpack_pair(ref[pl.ds(s, m, stride=ROW_SUB), :])
        los.append(lo.astype(BF16))
        his.append(hi.astype(BF16))
    return jnp.concatenate(los, axis=1), jnp.concatenate(his, axis=1)


def _out_proj_kernel(ya_ref, yr_ref, x_ref, w_ref, g_ref, wrh_ref, wrl_ref, h_ref, hn_ref, lg_ref):
    half = ya_ref.shape[1]
    acc = _dot(ya_ref[...], w_ref[0:half, :]) + _dot(yr_ref[...], w_ref[half:, :])
    h = x_ref[...] + acc
    h_ref[...] = h
    hn = _rms(h, g_ref[...])
    d2 = hn.shape[1] // 2
    _store_rows(hn_ref, _pack_pair(hn[:, :d2], hn[:, d2:]))
    hh, hl = _split2(hn)
    lg_ref[...] = (_dot_nt(wrh_ref[...], hh) + _dot_nt(wrh_ref[...], hl)
                   + _dot_nt(wrl_ref[...], hh))


def _out_proj(ya, yr, x, w, g, wr, tm):
    wr_hi, wr_lo = _split2(wr.T)
    m, d = x.shape
    tm = min(tm, m)
    full = lambda a: pl.BlockSpec(a.shape, lambda i: (0,) * a.ndim)
    return pl.pallas_call(
        _out_proj_kernel,
        grid=(m // tm,),
        in_specs=[pl.BlockSpec((tm, ya.shape[1]), lambda i: (i, 0)),
                  pl.BlockSpec((tm, yr.shape[1]), lambda i: (i, 0)),
                  pl.BlockSpec((tm, d), lambda i: (i, 0)),
                  full(w), full(g), full(wr_hi), full(wr_lo)],
        out_specs=[pl.BlockSpec((tm, d), lambda i: (i, 0)),
                   pl.BlockSpec((tm * ROW_SUB, 128), lambda i: (i, 0)),
                   pl.BlockSpec((N_EXPERTS, tm), lambda i: (0, i))],
        out_shape=[jax.ShapeDtypeStruct((m, d), F32), jax.ShapeDtypeStruct((m * ROW_SUB, 128), jnp.uint32),
                   jax.ShapeDtypeStruct((N_EXPERTS, m), F32)],
        compiler_params=_cparams(1),
        name="out_proj_router_logits",
    )(ya, yr, x, w, g, wr_hi, wr_lo)


def _router_kernel(lg_ref, bias_ref, e_ref, r_ref, w_ref, cnt_ref, carry_ref):
    tm = lg_ref.shape[1]
    gsz = N_EXPERTS // N_GROUPS

    @pl.when(pl.program_id(0) == 0)
    def _():
        carry_ref[...] = jnp.zeros(carry_ref.shape, F32)

    scores = jax.nn.sigmoid(lg_ref[...]).reshape(N_GROUPS, gsz, tm)
    biased = scores + bias_ref[...].reshape(N_GROUPS, gsz, 1)
    neg = -jnp.inf

    eidx = lax.broadcasted_iota(jnp.int32, biased.shape, 1)
    m1 = jnp.max(biased, axis=1, keepdims=True)
    first = jnp.min(jnp.where(biased == m1, eidx, gsz), axis=1, keepdims=True)
    m2 = jnp.max(jnp.where(eidx == first, neg, biased), axis=1, keepdims=True)
    gscore = m1 + m2

    gidx = lax.broadcasted_iota(jnp.int32, gscore.shape, 0)
    gsel = jnp.zeros(gscore.shape, jnp.bool_)
    work = gscore
    for _ in range(TOPK_GROUPS):
        best = jnp.max(work, axis=0, keepdims=True)
        pick = jnp.min(jnp.where(work == best, gidx, N_GROUPS), axis=0, keepdims=True)
        hit = gidx == pick
        gsel = jnp.logical_or(gsel, hit)
        work = jnp.where(hit, neg, work)

    flat = lax.broadcasted_iota(jnp.int32, biased.shape, 0) * gsz + eidx
    work = jnp.where(gsel, biased, neg)
    hits = []
    picks = []
    for _ in range(TOP_K):
        best = jnp.max(jnp.max(work, axis=1, keepdims=True), axis=0, keepdims=True)
        cand = jnp.where(work == best, flat, N_EXPERTS)
        pick = jnp.min(jnp.min(cand, axis=1, keepdims=True), axis=0, keepdims=True)
        hit = flat == pick
        hits.append(hit)
        picks.append(pick)
        work = jnp.where(hit, neg, work)

    def pick_value(hit, val):
        s = jnp.sum(jnp.sum(jnp.where(hit, val, 0.0), axis=1, keepdims=True), axis=0, keepdims=True)
        return s.reshape(1, tm)

    sel = jnp.zeros(biased.shape, F32)
    for hit in hits:
        sel = jnp.where(hit, 1.0, sel)
    sel2 = sel.reshape(N_EXPERTS, tm)
    r_i = lax.broadcasted_iota(jnp.int32, (tm, tm), 0)
    c_i = lax.broadcasted_iota(jnp.int32, (tm, tm), 1)
    upper = jnp.where(r_i < c_i, 1.0, 0.0).astype(BF16)
    rank = _dot(sel2.astype(BF16), upper) + carry_ref[...]
    carry = carry_ref[...] + jnp.sum(sel2, axis=1, keepdims=True)
    carry_ref[...] = carry
    cnt_ref[...] = carry
    rank3 = rank.reshape(N_GROUPS, gsz, tm)

    raw = [pick_value(hit, scores) for hit in hits]
    tot = raw[0]
    for x in raw[1:]:
        tot = tot + x
    wrow = lax.broadcasted_iota(jnp.int32, (N_EXPERTS, tm), 0)
    wt = jnp.zeros((N_EXPERTS, tm), F32)
    for j in range(TOP_K):
        e_ref[j:j + 1, :] = picks[j].reshape(1, tm)
        r_ref[j:j + 1, :] = pick_value(hits[j], rank3).astype(jnp.int32)
        wt = jnp.where(wrow == j, raw[j] / tot * ROUTED_SCALE, wt)
    w_ref[...] = wt.T


def _router(lg_t, bias, tm):
    e, m = lg_t.shape
    tm = min(tm, m)
    row = pl.BlockSpec((TOP_K, tm), lambda i: (0, i))
    return pl.pallas_call(
        _router_kernel,
        grid=(m // tm,),
        in_specs=[pl.BlockSpec((e, tm), lambda i: (0, i)),
                  pl.BlockSpec((e, 1), lambda i: (0, 0))],
        out_specs=[row, row, pl.BlockSpec((tm, e), lambda i: (i, 0)),
                   pl.BlockSpec((e, 1), lambda i: (0, 0))],
        out_shape=[jax.ShapeDtypeStruct((TOP_K, m), jnp.int32),
                   jax.ShapeDtypeStruct((TOP_K, m), jnp.int32),
                   jax.ShapeDtypeStruct((m, e), F32),
                   jax.ShapeDtypeStruct((e, 1), F32)],
        scratch_shapes=[pltpu.VMEM((e, 1), F32)],
        compiler_params=_cparams(1),
        name="router_topk",
    )(lg_t, bias)


def _dest_kernel(off_ref, e_ref, r_ref, d_ref):
    e = e_ref[...]
    acc = r_ref[...]
    for x in range(N_EXPERTS):
        acc = acc + jnp.where(e == x, off_ref[x], 0)
    d_ref[...] = acc * ROW_SUB


def _dest(off, eidx, rank, tm):
    k, m = eidx.shape
    tm = min(tm, m)
    spec = pl.BlockSpec((k, tm), lambda i: (0, i))
    return pl.pallas_call(
        _dest_kernel,
        grid=(m // tm,),
        in_specs=[pl.BlockSpec(memory_space=pltpu.SMEM), spec, spec],
        out_specs=spec,
        out_shape=jax.ShapeDtypeStruct((k, m), jnp.int32),
        compiler_params=_cparams(1),
        name="moe_dest_rows",
    )(off, eidx, rank)


def _scatter_kernel(zt_ref, d_ref, x_ref, z_ref, h_ref, sg_ref, su_ref, sd_ref, xs_hbm, hs_ref, sem,
                    *, tm, tile):
    i = pl.program_id(0)
    rs = ROW_SUB

    @pl.when(i == 0)
    def _():
        def zero(e, c):
            start = pl.multiple_of(zt_ref[e] * (tile * rs), tile * rs)
            pltpu.make_async_copy(z_ref, xs_hbm.at[pl.ds(start, tile * rs)], sem).start()
            return c

        lax.fori_loop(0, N_EXPERTS, zero, 0)

        def zwait(e, c):
            pltpu.make_async_copy(z_ref, xs_hbm.at[pl.ds(0, tile * rs)], sem).wait()
            return c

        lax.fori_loop(0, N_EXPERTS, zwait, 0)

    def body(n, c):
        src = x_ref.at[pl.ds(pl.multiple_of(n * rs, rs), rs)]
        for j in range(TOP_K):
            dst = xs_hbm.at[pl.ds(pl.multiple_of(d_ref[n * TOP_K + j], rs), rs)]
            pltpu.make_async_copy(src, dst, sem).start(priority=j % 2)
        return c

    q4 = tm // 4
    lax.fori_loop(0, q4, body, 0)
    lo, hi = _load_rows(x_ref)
    d2 = lo.shape[1]
    gate = _dot(lo, sg_ref[0:d2, :]) + _dot(hi, sg_ref[d2:, :])
    lax.fori_loop(q4, 2 * q4, body, 0)
    up = _dot(lo, su_ref[0:d2, :]) + _dot(hi, su_ref[d2:, :])
    mid = (gate * jax.nn.sigmoid(gate) * up).astype(BF16)
    lax.fori_loop(2 * q4, 3 * q4, body, 0)
    hs_ref[...] = h_ref[...] + _dot(mid, sd_ref[...])
    lax.fori_loop(3 * q4, tm, body, 0)

    for j in range(TOP_K):
        pltpu.make_async_copy(x_ref, xs_hbm.at[pl.ds(0, tm * rs)], sem).wait()


def _scatter_rows(zero_tile, dest, xp, h, sg, su, sd, rows, tile, tm):
    m, d = h.shape
    tm = min(tm, m)
    full = lambda a: pl.BlockSpec(a.shape, lambda i, zt: (0,) * a.ndim)
    return pl.pallas_call(
        functools.partial(_scatter_kernel, tm=tm, tile=tile),
        grid_spec=pltpu.PrefetchScalarGridSpec(
            num_scalar_prefetch=1,
            grid=(m // tm,),
            in_specs=[pl.BlockSpec((tm * TOP_K,), lambda i, zt: (i,), memory_space=pltpu.SMEM),
                      pl.BlockSpec((tm * ROW_SUB, 128), lambda i, zt: (i, 0)),
                      pl.BlockSpec((tile * ROW_SUB, 128), lambda i, zt: (0, 0)),
                      pl.BlockSpec((tm, d), lambda i, zt: (i, 0)),
                      full(sg), full(su), full(sd)],
            out_specs=[pl.BlockSpec(memory_space=pl.ANY), pl.BlockSpec((tm, d), lambda i, zt: (i, 0))],
            scratch_shapes=[pltpu.SemaphoreType.DMA(())],
        ),
        out_shape=[jax.ShapeDtypeStruct((rows * ROW_SUB, 128), jnp.uint32),
                   jax.ShapeDtypeStruct((m, d), F32)],
        compiler_params=_cparams(1),
        name="moe_scatter_rows_shared",
    )(zero_tile, dest, xp, jnp.zeros((tile * ROW_SUB, 128), jnp.uint32), h, sg, su, sd)


def _expert_kernel(te_ref, nx_ref, gi_ref, nu_ref, x_ref, wg_hbm, wu_hbm, wd_hbm, o_ref,
                   wgf, wuf, wdf, wgb, wub, wdb, sem):
    t = pl.program_id(0)
    tc = jnp.minimum(t, nu_ref[0] - 1)
    prev = jnp.maximum(tc - 1, 0)
    new_expert = jnp.logical_or(t == 0, te_ref[tc] != te_ref[prev])
    slot = jnp.bitwise_and(gi_ref[tc], 1)

    def copies(e, s):
        return (pltpu.make_async_copy(wg_hbm.at[e], wgf.at[s], sem.at[s]),
                pltpu.make_async_copy(wu_hbm.at[e], wuf.at[s], sem.at[s]),
                pltpu.make_async_copy(wd_hbm.at[e], wdf.at[s], sem.at[s]))

    @pl.when(t == 0)
    def _():
        for c in copies(te_ref[0], 0):
            c.start()

    @pl.when(jnp.logical_and(new_expert, t < nu_ref[0]))
    def _():
        for c in copies(te_ref[tc], slot):
            c.wait()
        wgb[...] = wgf[slot].astype(BF16)
        wub[...] = wuf[slot].astype(BF16)
        wdb[...] = wdf[slot].astype(BF16)

        @pl.when(nx_ref[tc] >= 0)
        def _():
            for c in copies(nx_ref[tc], 1 - slot):
                c.start()

    @pl.when(t < nu_ref[0])
    def _():
        lo, hi = _load_rows(x_ref)
        d2 = lo.shape[1]
        gate = _dot(lo, wgb[0:d2, :]) + _dot(hi, wgb[d2:, :])
        up = _dot(lo, wub[0:d2, :]) + _dot(hi, wub[d2:, :])
        mid = (gate * jax.nn.sigmoid(gate) * up).astype(BF16)
        y = _dot(mid, wdb[...])
        _store_rows(o_ref, _pack_pair(y[:, :d2], y[:, d2:]))


def _experts(tile_expert, next_expert, group_idx, n_used, xs, wg, wu, wd, tile):
    rows = xs.shape[0] // ROW_SUB
    d = wg.shape[1]
    blk = (tile * ROW_SUB, 128)
    tile_map = lambda t, te, nx, gi, nu: (jnp.minimum(t, nu[0] - 1), 0)
    hbm = pl.BlockSpec(memory_space=pl.ANY)
    return pl.pallas_call(
        _expert_kernel,
        grid_spec=pltpu.PrefetchScalarGridSpec(
            num_scalar_prefetch=4,
            grid=(rows // tile,),
            in_specs=[pl.BlockSpec(blk, tile_map), hbm, hbm, hbm],
            out_specs=pl.BlockSpec(blk, tile_map),
            scratch_shapes=[pltpu.VMEM((2, d, EXPERT_DIM), F32), pltpu.VMEM((2, d, EXPERT_DIM), F32),
                            pltpu.VMEM((2, EXPERT_DIM, d), F32),
                            pltpu.VMEM((d, EXPERT_DIM), BF16), pltpu.VMEM((d, EXPERT_DIM), BF16),
                            pltpu.VMEM((EXPERT_DIM, d), BF16), pltpu.SemaphoreType.DMA((2,))],
        ),
        out_shape=jax.ShapeDtypeStruct(xs.shape, jnp.uint32),
        compiler_params=_cparams(1),
        name="moe_experts",
    )(tile_expert, next_expert, group_idx, n_used, xs, wg, wu, wd)


def _combine_kernel(d_ref, dn_ref, ys_hbm, w_ref, h_ref, o_ref, buf, sem, *, tm, nt):
    i = pl.program_id(0)
    slot = jnp.bitwise_and(i, 1)
    nxt = 1 - slot
    rs = ROW_SUB

    def start(idx_ref, n, s):
        for j in range(TOP_K):
            src = ys_hbm.at[pl.ds(pl.multiple_of(idx_ref[n * TOP_K + j], rs), rs)]
            pltpu.make_async_copy(src, buf.at[s, j, pl.ds(n * rs, rs)], sem.at[s]).start(priority=j % 2)

    def wait_all(s):
        for j in range(TOP_K):
            pltpu.make_async_copy(ys_hbm.at[pl.ds(0, tm * rs)], buf.at[s, j], sem.at[s]).wait()

    @pl.when(i == 0)
    def _():
        def body(n, c):
            for j in range(TOP_K):
                src = ys_hbm.at[pl.ds(pl.multiple_of(d_ref[n * TOP_K + j], rs), rs)]
                dst = buf.at[0, j, pl.ds(pl.multiple_of(n * rs, rs), rs)]
                pltpu.make_async_copy(src, dst, sem.at[0]).start(priority=j % 2)
            return c

        lax.fori_loop(0, tm, body, 0)

    per = tm // (rs * TOP_K)
    batches = iter(range(0, tm, per))

    def start_batch():
        n0 = next(batches)
        for n in range(n0, n0 + per):
            start(dn_ref, n, nxt)

    d2 = h_ref.shape[1] // 2
    o_ref[...] = h_ref[...]
    wait_all(slot)
    wts = w_ref[...]
    wj = [wts[:, j:j + 1] for j in range(TOP_K)]
    for s in range(rs):
        r_lo = jnp.zeros((tm, 128), F32)
        r_hi = jnp.zeros((tm, 128), F32)
        for j in range(TOP_K):
            start_batch()
            ylo, yhi = _unpack_pair(buf[slot, j, pl.ds(s, tm, stride=rs), :])
            r_lo = r_lo + wj[j] * ylo
            r_hi = r_hi + wj[j] * yhi
        o_ref[:, s * 128:(s + 1) * 128] += r_lo
        o_ref[:, d2 + s * 128:d2 + (s + 1) * 128] += r_hi

    @pl.when(i == nt - 1)
    def _():
        wait_all(nxt)


def _combine(dest, ys, wcol, h, tm):
    m, d = h.shape
    tm = min(tm, m)
    nt = m // tm
    idx = lambda f: pl.BlockSpec((tm * TOP_K,), f, memory_space=pltpu.SMEM)
    return pl.pallas_call(
        functools.partial(_combine_kernel, tm=tm, nt=nt),
        grid=(nt,),
        in_specs=[idx(lambda i: (i,)), idx(lambda i: (jnp.minimum(i + 1, nt - 1),)),
                  pl.BlockSpec(memory_space=pl.ANY),
                  pl.BlockSpec((tm, N_EXPERTS), lambda i: (i, 0)),
                  pl.BlockSpec((tm, d), lambda i: (i, 0))],
        out_specs=pl.BlockSpec((tm, d), lambda i: (i, 0)),
        out_shape=jax.ShapeDtypeStruct((m, d), F32),
        scratch_shapes=[pltpu.VMEM((2, TOP_K, tm * ROW_SUB, 128), jnp.uint32),
                        pltpu.SemaphoreType.DMA((2,))],
        compiler_params=_cparams(1),
        name="moe_combine",
    )(dest, dest, ys, wcol, h)


def _moe(h1, hnp, lg_t, router_bias, w_gate, w_up, w_down, ws_gate, ws_up, ws_down):
    n = h1.shape[0]
    tile = MOE_TILE
    eidx, rank, wcol, cnt = _router(lg_t, router_bias.reshape(-1, 1), TM_ROUTER)
    counts = cnt[:, 0].astype(jnp.int32)
    tiles_per = (counts + tile - 1) // tile
    tile_end = jnp.cumsum(tiles_per)
    tile_start = tile_end - tiles_per
    n_tiles = (n * TOP_K) // tile + N_EXPERTS
    rows = n_tiles * tile
    tile_expert = jnp.minimum(jnp.sum(tile_end[None, :] <= jnp.arange(n_tiles)[:, None], axis=1),
                              N_EXPERTS - 1).astype(jnp.int32)
    n_used = tile_end[-1:].astype(jnp.int32)
    ids = jnp.arange(N_EXPERTS)
    later = jnp.logical_and(ids[None, :] > ids[:, None], tiles_per[None, :] > 0)
    nxt_of = jnp.min(jnp.where(later, ids[None, :], N_EXPERTS), axis=1)
    nxt_of = jnp.where(nxt_of == N_EXPERTS, -1, nxt_of)
    ord_of = jnp.cumsum((tiles_per > 0).astype(jnp.int32)) - 1
    onehot = tile_expert[:, None] == ids[None, :]
    next_expert = jnp.sum(jnp.where(onehot, nxt_of[None, :], 0), axis=1).astype(jnp.int32)
    group_idx = jnp.sum(jnp.where(onehot, ord_of[None, :], 0), axis=1).astype(jnp.int32)
    last_tile = jnp.clip(tile_end - 1, 0, n_tiles - 1).astype(jnp.int32)
    dest = _dest((tile_start * tile).astype(jnp.int32), eidx, rank, TM_DEST)
    dest = dest.T.reshape(-1)
    xs, hs = _scatter_rows(last_tile, dest, hnp, h1, ws_gate.astype(BF16), ws_up.astype(BF16),
                           ws_down.astype(BF16), rows, tile, TM_SCATTER)
    ys = _experts(tile_expert, next_expert, group_idx, n_used, xs, w_gate, w_up, w_down, tile)
    return _combine(dest, ys, wcol, hs, TM_COMBINE)


def _ple_kernel(h_ref, p_ref, gin_ref, wg_ref, bg_ref, wp_ref, gout_ref, o_ref):
    h = h_ref[...]
    gate = jax.nn.sigmoid(_dot(_rms(h, gin_ref[...]).astype(BF16), wg_ref[...]) + bg_ref[...])
    pp = _dot(p_ref[...].astype(BF16), wp_ref[...])
    o_ref[...] = h + _rms(pp * gate, gout_ref[...])


def _ple(h, p, g_in, w_g, b_g, w_p, g_out, tm):
    m, d = h.shape
    tm = min(tm, m)
    full = lambda a: pl.BlockSpec(a.shape, lambda i: (0,) * a.ndim)
    return pl.pallas_call(
        _ple_kernel,
        grid=(m // tm,),
        in_specs=[pl.BlockSpec((tm, d), lambda i: (i, 0)),
                  pl.BlockSpec((tm, p.shape[1]), lambda i: (i, 0)),
                  full(g_in), full(w_g), full(b_g), full(w_p), full(g_out)],
        out_specs=pl.BlockSpec((tm, d), lambda i: (i, 0)),
        out_shape=jax.ShapeDtypeStruct((m, d), F32),
        compiler_params=_cparams(1),
        name="ple",
    )(h, p, g_in, w_g, b_g, w_p, g_out)


def _pad_rows(a, rows):
    return jnp.pad(a, ((0, rows - a.shape[0]), (0, 0)))


def _row(a):
    return a.reshape(1, -1)


def _proj_layout(a):
    def padded(x, width):
        return jnp.pad(x, [(0, 0)] * (x.ndim - 1) + [(0, width - x.shape[-1])])

    parts = [padded(a[..., :MLA_COLS], MLA_COLS_PAD), a[..., MLA_COLS:MLA_COLS + 3 * RW_C]]
    off = MLA_COLS + 3 * RW_C
    for width, wide in ((RW_DECAY_LORA, 128), (RW_A_LORA, 128), (RW_GATE_LORA, 256)):
        parts.append(padded(a[..., off:off + width], wide))
        off += width
    return jnp.concatenate(parts, axis=-1)


def _mla_from_proj(proj, positions, g_qa, w_uq, g_kva, w_ukv, g_qn, g_kn):
    b, t, _ = proj.shape
    wq = w_uq.reshape(MLA_Q_LORA, MLA_HEADS, MLA_QK)
    wq = jnp.concatenate([wq[:, :, :MLA_NOPE].reshape(MLA_Q_LORA, -1),
                          wq[:, :, MLA_NOPE:].reshape(MLA_Q_LORA, -1)], axis=1).astype(BF16)
    half = MLA_ROPE // 2
    inv = ROPE_THETA ** (-jnp.arange(half, dtype=F32) / half)
    invf = _row(jnp.concatenate([inv, inv]))
    sgn = _row(jnp.concatenate([-jnp.ones(half, F32), jnp.ones(half, F32)]))
    q, k, v = _mla_prep(proj, positions.reshape(b, t, 1), invf, sgn, _row(g_qa), wq, _row(g_kva),
                        w_ukv.astype(BF16), _row(g_qn), _row(g_kn), TM_MLA_PREP)
    return _attention(q, k, v, ATTN_TQ, ATTN_TK, ATTN_HEADS_PER_STEP)


def _rwkv_from_proj(proj, mu, w0, w_w2, a0, a_w2, g_w2, k_k, k_a, r_k, ln_w, ln_b):
    head_of = jnp.arange(RW_C) // RW_HEAD
    bd = (head_of[:, None] == head_of[None, :]).astype(BF16)
    mu_all = jnp.concatenate([jnp.zeros((MLA_COLS,), F32), mu])
    rr, ww, kx, vx, kkx, bx, gx = _rw_prep(
        proj, _row(_proj_layout(mu_all)), _row(w0), _pad_rows(w_w2, 128).astype(BF16), _row(a0),
        _pad_rows(a_w2, 128).astype(BF16), _pad_rows(g_w2, 256).astype(BF16), _row(k_k), _row(k_a),
        bd, TM_RW_PREP)
    return _rw_chunk(rr, ww, kx, vx, kkx, bx, gx, _row(r_k), _row(ln_w), _row(ln_b))


def _layer(h, p, positions, g_mix, w_in, mla_g_qa, mla_w_uq, mla_g_kva, mla_w_ukv, mla_g_qn,
           mla_g_kn, rw_mu, rw_w0, rw_w_w2, rw_a0, rw_a_w2, rw_g_w2, rw_k_k, rw_k_a, rw_r_k,
           rw_ln_w, rw_ln_b, w_out, g_ffn, w_router, router_bias, w_gate, w_up, w_down, ws_gate,
           ws_up, ws_down, g_ple_in, w_ple_gate, b_ple_gate, w_ple_proj, g_ple_out):
    b, t, d = h.shape
    n = b * t
    x2 = h.reshape(n, d)

    proj = _in_proj(x2, _row(g_mix), _proj_layout(w_in).astype(BF16), TM_IN_PROJ).reshape(b, t, -1)
    y_mla = _mla_from_proj(proj, positions, mla_g_qa, mla_w_uq, mla_g_kva, mla_w_ukv, mla_g_qn, mla_g_kn)
    y_rw = _rwkv_from_proj(proj, rw_mu, rw_w0, rw_w_w2, rw_a0, rw_a_w2, rw_g_w2, rw_k_k, rw_k_a, rw_r_k,
                           rw_ln_w, rw_ln_b)

    h1, hnp, lg_t = _out_proj(y_mla.reshape(n, -1), y_rw.reshape(n, -1), x2, w_out.astype(BF16),
                              _row(g_ffn), w_router, TM_OUT_PROJ)
    h2 = _moe(h1, hnp, lg_t, router_bias, w_gate, w_up, w_down, ws_gate, ws_up, ws_down)

    h3 = _ple(h2, p.reshape(n, -1), _row(g_ple_in), w_ple_gate.astype(BF16), _row(b_ple_gate),
              w_ple_proj.astype(BF16), _row(g_ple_out), TM_PLE)
    return h3.reshape(b, t, d)


def kernel(x, p, positions, g_mix, w_in, mla_g_qa, mla_w_uq, mla_g_kva, mla_w_ukv, mla_g_qn, mla_g_kn, rw_mu, rw_w0, rw_w_w2, rw_a0, rw_a_w2, rw_g_w2, rw_k_k, rw_k_a, rw_r_k, rw_ln_w, rw_ln_b, w_out, g_ffn, w_router, router_bias, w_gate, w_up, w_down, ws_gate, ws_up, ws_down, g_ple_in, w_ple_gate, b_ple_gate, w_ple_proj, g_ple_out):
    params = (g_mix, w_in, mla_g_qa, mla_w_uq, mla_g_kva, mla_w_ukv, mla_g_qn, mla_g_kn, rw_mu,
              rw_w0, rw_w_w2, rw_a0, rw_a_w2, rw_g_w2, rw_k_k, rw_k_a, rw_r_k, rw_ln_w, rw_ln_b,
              w_out, g_ffn, w_router, router_bias, w_gate, w_up, w_down, ws_gate, ws_up, ws_down,
              g_ple_in, w_ple_gate, b_ple_gate, w_ple_proj, g_ple_out)
    h = x
    for i in range(g_mix.shape[0]):
        h = _layer(h, p[i], positions, *[a[i] for a in params])
    return h
```
